```python
import math
import jax, jax.numpy as jnp
from jax import lax
import numpy as np

D_MODEL = 1024
BATCH = 8
SEQ = 2048
DEPTH = 1
DEC_BATCH = 128
DEC_SEQ = 1
PAST_LEN = 2048
PAGE_SIZE = 128

C_CONV = D_MODEL // 2
CONV_WIDTH = 31
N_HEADS = 8
HEAD_DIM = (D_MODEL - C_CONV) // N_HEADS
N_KV_HEADS = 2
GROUP = N_HEADS // N_KV_HEADS
KV_W = N_KV_HEADS * HEAD_DIM
N_KV_SLOTS = 6
N_PAGED_SLOTS = 4
MIX_WIDTH = C_CONV + N_HEADS * HEAD_DIM
IN_COLS = 2 * C_CONV + N_HEADS * HEAD_DIM + N_KV_SLOTS * KV_W + 3 * N_HEADS
D_FF = 4 * D_MODEL
CMP_BLOCK = 32
SEL_BLOCK = 64
TOP_N = 16
N_LOCAL = 2
WINDOW = 512
WIN_Q_BLOCK = 128
SEL_Q_BLOCK = 64
FORCE_BONUS = float(GROUP + 1)
NEG_INF = -1e30
RMS_EPS = 1e-6
LN_EPS = 1e-5
ATTN_SCALE = HEAD_DIM ** -0.5

kernel_name = 'hymba_conformer_nsa_decoder_step'


def rms_norm(x, g):
    xf = x.astype(jnp.float32)
    y = xf * lax.rsqrt(jnp.mean(xf * xf, axis=-1, keepdims=True) + RMS_EPS)
    return (y * g.astype(jnp.float32)).astype(x.dtype)


def layer_norm(x, g, b):
    xf = x.astype(jnp.float32)
    mu = jnp.mean(xf, axis=-1, keepdims=True)
    var = jnp.mean(jnp.square(xf - mu), axis=-1, keepdims=True)
    y = (xf - mu) * lax.rsqrt(var + LN_EPS) * g.astype(jnp.float32) + b.astype(jnp.float32)
    return y.astype(x.dtype)


def masked_softmax(s, mask):
    p = jax.nn.softmax(jnp.where(mask, s, NEG_INF), axis=-1)
    return jnp.where(mask, p, 0.0)


def project(x, g, w_in):
    B, L, _ = x.shape
    z = rms_norm(x, g) @ w_in
    c0 = 2 * C_CONV
    c1 = c0 + N_HEADS * HEAD_DIM
    c2 = c1 + N_KV_SLOTS * KV_W
    u = z[..., :c0]
    q = z[..., c0:c1].reshape(B, L, N_KV_HEADS, GROUP, HEAD_DIM)
    kv = z[..., c1:c2].reshape(B, L, N_KV_SLOTS, N_KV_HEADS, HEAD_DIM)
    gl = z[..., c2:].reshape(B, L, N_HEADS, 3)
    return u, q, kv, gl


def conv_mixer(u, prev, w_dw, b_dw, ln_g, ln_b):
    a, b = jnp.split(u, 2, axis=-1)
    glu = a * jax.nn.sigmoid(b)
    ext = jnp.concatenate([prev.astype(glu.dtype), glu], axis=1)
    y = lax.conv_general_dilated(ext, w_dw[:, None, :].astype(glu.dtype), window_strides=(1,),
                                 padding='VALID', dimension_numbers=('NWC', 'WIO', 'NWC'),
                                 feature_group_count=C_CONV)
    y = layer_norm(y + b_dw, ln_g, ln_b)
    return jax.nn.silu(y), ext[:, -(CONV_WIDTH - 1):]


def compressed_selected(q, q_pos, k_cmp, v_cmp, k_sel, v_sel, w_ck, w_cv):
    B, Q = q.shape[0], q.shape[1]
    T = k_cmp.shape[1]
    t_pad = -(-T // SEL_BLOCK) * SEL_BLOCK
    padw = ((0, 0), (0, t_pad - T), (0, 0), (0, 0))
    k_cmp, v_cmp, k_sel, v_sel = [jnp.pad(a, padw) for a in (k_cmp, v_cmp, k_sel, v_sel)]
    n_cmp = t_pad // CMP_BLOCK
    n_blk = t_pad // SEL_BLOCK
    kc = jnp.einsum('bcjhd,jh->bchd', k_cmp.reshape(B, n_cmp, CMP_BLOCK, N_KV_HEADS, HEAD_DIM), w_ck)
    vc = jnp.einsum('bcjhd,jh->bchd', v_cmp.reshape(B, n_cmp, CMP_BLOCK, N_KV_HEADS, HEAD_DIM), w_cv)
    s = jnp.einsum('bqhgd,bchd->bqhgc', q, kc).astype(jnp.float32) * ATTN_SCALE
    c_end = (jnp.arange(n_cmp, dtype=jnp.int32) + 1) * CMP_BLOCK - 1
    cmask = c_end[None, :] <= q_pos[:, None]
    p = masked_softmax(s, cmask[None, :, None, None, :])
    o_cmp = jnp.einsum('bqhgc,bchd->bqhgd', p.astype(vc.dtype), vc)
    p_blk = p.sum(axis=3).reshape(B, Q, N_KV_HEADS, n_blk, SEL_BLOCK // CMP_BLOCK).sum(-1)
    blk = jnp.arange(n_blk, dtype=jnp.int32)
    back = (q_pos // SEL_BLOCK)[:, None] - blk[None, :]
    forced = (blk[None, :] == 0) | ((back >= 0) & (back < N_LOCAL))
    valid = back >= 0
    score = jnp.where(valid[None, :, None, :],
                      p_blk + jnp.where(forced, FORCE_BONUS, 0.0)[None, :, None, :], -jnp.inf)
    n_top = min(TOP_N, n_blk)
    _, idx = lax.top_k(score, n_top)
    kb = k_sel.reshape(B, n_blk, SEL_BLOCK, N_KV_HEADS, HEAD_DIM).transpose(0, 3, 1, 2, 4)
    vb = v_sel.reshape(B, n_blk, SEL_BLOCK, N_KV_HEADS, HEAD_DIM).transpose(0, 3, 1, 2, 4)
    qb = Q if Q <= SEL_Q_BLOCK else math.gcd(Q, SEL_Q_BLOCK)
    nqb = Q // qb
    take = jax.vmap(jax.vmap(lambda a, i: a[i]))
    n_keys = n_top * SEL_BLOCK

    def sel_block(args):
        q_b, i_b, pos_b = args
        it = jnp.swapaxes(i_b, 1, 2)
        kg = take(kb, it)
        vg = take(vb, it).reshape(B, N_KV_HEADS, qb, n_keys, HEAD_DIM)
        s_b = jnp.einsum('bqhgd,bhqnsd->bqhgns', q_b, kg).astype(jnp.float32) * ATTN_SCALE
        kpos = i_b[..., None] * SEL_BLOCK + jnp.arange(SEL_BLOCK, dtype=jnp.int32)
        m = (kpos <= pos_b[None, :, None, None, None]).reshape(B, qb, N_KV_HEADS, 1, n_keys)
        p_b = masked_softmax(s_b.reshape(B, qb, N_KV_HEADS, GROUP, n_keys), m)
        return jnp.einsum('bqhgk,bhqkd->bqhgd', p_b.astype(vg.dtype), vg)

    xs = (jnp.swapaxes(q.reshape(B, nqb, qb, N_KV_HEADS, GROUP, HEAD_DIM), 0, 1),
          jnp.swapaxes(idx.reshape(B, nqb, qb, N_KV_HEADS, n_top), 0, 1),
          q_pos.reshape(nqb, qb))
    o_sel = lax.map(sel_block, xs)
    o_sel = jnp.swapaxes(o_sel, 0, 1).reshape(B, Q, N_KV_HEADS, GROUP, HEAD_DIM)
    return o_cmp, o_sel


def window_attend(q, q_pos, k, v, k_pos):
    rel = q_pos[:, None] - k_pos[None, :]
    m = (rel >= 0) & (rel <= WINDOW) & (k_pos[None, :] >= 0)
    s = jnp.einsum('bqhgd,bkhd->bqhgk', q, k).astype(jnp.float32) * ATTN_SCALE
    p = masked_softmax(s, m[None, :, None, None, :])
    return jnp.einsum('bqhgk,bkhd->bqhgd', p.astype(v.dtype), v)


def window_prompt(q, k, v):
    B, S = q.shape[0], q.shape[1]
    qb = math.gcd(S, WIN_Q_BLOCK)
    padw = ((0, 0), (WINDOW, 0), (0, 0), (0, 0))
    kp, vp = jnp.pad(k, padw), jnp.pad(v, padw)
    r = jnp.arange(qb + WINDOW, dtype=jnp.int32)
    a = jnp.arange(qb, dtype=jnp.int32)

    def blk(start):
        return window_attend(lax.dynamic_slice_in_dim(q, start, qb, axis=1), start + a,
                             lax.dynamic_slice_in_dim(kp, start, qb + WINDOW, axis=1),
                             lax.dynamic_slice_in_dim(vp, start, qb + WINDOW, axis=1),
                             start - WINDOW + r)

    o = lax.map(blk, jnp.arange(S // qb, dtype=jnp.int32) * qb)
    return jnp.swapaxes(o, 0, 1).reshape(B, S, N_KV_HEADS, GROUP, HEAD_DIM)


def gate_merge(o_cmp, o_sel, o_win, gl):
    B, L = o_cmp.shape[0], o_cmp.shape[1]
    g = jax.nn.sigmoid(gl.astype(jnp.float32)).astype(o_cmp.dtype)
    hd = lambda o: o.reshape(B, L, N_HEADS, HEAD_DIM)
    o = hd(o_cmp) * g[..., 0:1] + hd(o_sel) * g[..., 1:2] + hd(o_win) * g[..., 2:3]
    return o.reshape(B, L, N_HEADS * HEAD_DIM)


def residual_block(x, conv_y, attn_y, w_out, g_mlp, w_up, w_down):
    h = x + jnp.concatenate([conv_y, attn_y], axis=-1) @ w_out
    m = rms_norm(h, g_mlp) @ w_up
    return h + jnp.square(jax.nn.relu(m)) @ w_down


def setup_inputs(seed: int = 0) -> dict:
    key = jax.random.key(seed)
    ks = jax.random.split(key, 24)
    f32 = jnp.float32
    nrm = lambda k, shape, s=1.0: jax.random.normal(k, shape, f32) * s
    n_pages = PAST_LEN // PAGE_SIZE
    n_used = DEC_BATCH * n_pages
    n_phys = n_used + (n_used + 3) // 4
    w_buf = min(WINDOW, PAST_LEN)
    page_table = jax.random.permutation(ks[5], n_phys)[:n_used].reshape(DEC_BATCH, n_pages).astype(jnp.int32)
    return {
        'x_prompt': nrm(ks[0], (BATCH, SEQ, D_MODEL)),
        'x_sample': nrm(ks[1], (DEC_BATCH, DEC_SEQ, D_MODEL)),
        'cache_kv': nrm(ks[2], (DEPTH, n_phys, PAGE_SIZE, N_PAGED_SLOTS, N_KV_HEADS, HEAD_DIM)),
        'cache_win': nrm(ks[3], (DEPTH, DEC_BATCH, w_buf, 2, N_KV_HEADS, HEAD_DIM)),
        'state_conv': nrm(ks[4], (DEPTH, DEC_BATCH, CONV_WIDTH - 1, C_CONV), 0.5),
        'page_table': page_table,
        'g_attn_norm': 1.0 + nrm(ks[6], (DEPTH, D_MODEL), 0.05),
        'w_in': nrm(ks[7], (DEPTH, D_MODEL, IN_COLS), D_MODEL ** -0.5),
        'w_dw': nrm(ks[8], (DEPTH, CONV_WIDTH, C_CONV), CONV_WIDTH ** -0.5),
        'b_dw': nrm(ks[9], (DEPTH, C_CONV), 0.02),
        'conv_ln_g': 1.0 + nrm(ks[10], (DEPTH, C_CONV), 0.05),
        'conv_ln_b': nrm(ks[11], (DEPTH, C_CONV), 0.02),
        'w_cmp_k': (1.0 + nrm(ks[12], (DEPTH, CMP_BLOCK, N_KV_HEADS), 0.3)) * CMP_BLOCK ** -0.5,
        'w_cmp_v': (1.0 + nrm(ks[13], (DEPTH, CMP_BLOCK, N_KV_HEADS), 0.3)) * CMP_BLOCK ** -0.5,
        'w_out': nrm(ks[14], (DEPTH, MIX_WIDTH, D_MODEL), MIX_WIDTH ** -0.5),
        'g_mlp_norm': 1.0 + nrm(ks[15], (DEPTH, D_MODEL), 0.05),
        'w_up': nrm(ks[16], (DEPTH, D_MODEL, D_FF), D_MODEL ** -0.5),
        'w_down': nrm(ks[17], (DEPTH, D_FF, D_MODEL), D_FF ** -0.5),
        'g_final': 1.0 + nrm(ks[18], (D_MODEL,), 0.05),
    }


def reference(x_prompt, x_sample, cache_kv, cache_win, state_conv, page_table, g_attn_norm, w_in, w_dw, b_dw,
              conv_ln_g, conv_ln_b, w_cmp_k, w_cmp_v, w_out, g_mlp_norm, w_up, w_down, g_final):
    hp, hs = x_prompt, x_sample
    w_buf = cache_win.shape[2]
    q_pos_p = jnp.arange(SEQ, dtype=jnp.int32)
    q_pos_s = PAST_LEN + jnp.arange(DEC_SEQ, dtype=jnp.int32)
    k_pos_win_s = PAST_LEN - w_buf + jnp.arange(w_buf + DEC_SEQ, dtype=jnp.int32)
    kv_p, win_p, conv_p, kv_s, win_s, conv_s = [], [], [], [], [], []
    for l in range(DEPTH):
        u, q, kv, gl = project(hp, g_attn_norm[l], w_in[l])
        zeros = jnp.zeros((BATCH, CONV_WIDTH - 1, C_CONV), u.dtype)
        conv_y, conv_st = conv_mixer(u, zeros, w_dw[l], b_dw[l], conv_ln_g[l], conv_ln_b[l])
        o_cmp, o_sel = compressed_selected(q, q_pos_p, kv[:, :, 0], kv[:, :, 1], kv[:, :, 2], kv[:, :, 3],
                                           w_cmp_k[l], w_cmp_v[l])
        o_win = window_prompt(q, kv[:, :, 4], kv[:, :, 5])
        hp = residual_block(hp, conv_y, gate_merge(o_cmp, o_sel, o_win, gl), w_out[l], g_mlp_norm[l], w_up[l], w_down[l])
        kv_p.append(kv[:, :, :N_PAGED_SLOTS])
        win_p.append(kv[:, -min(WINDOW, SEQ):, N_PAGED_SLOTS:])
        conv_p.append(conv_st)
        u, q, kv, gl = project(hs, g_attn_norm[l], w_in[l])
        conv_y, conv_st = conv_mixer(u, state_conv[l], w_dw[l], b_dw[l], conv_ln_g[l], conv_ln_b[l])
        past = jnp.take(cache_kv[l], page_table, axis=0).reshape(DEC_BATCH, PAST_LEN, N_PAGED_SLOTS, N_KV_HEADS, HEAD_DIM)
        full = jnp.concatenate([past, kv[:, :, :N_PAGED_SLOTS].astype(past.dtype)], axis=1)
        o_cmp, o_sel = compressed_selected(q, q_pos_s, full[:, :, 0], full[:, :, 1], full[:, :, 2], full[:, :, 3],
                                           w_cmp_k[l], w_cmp_v[l])
        win = jnp.concatenate([cache_win[l].astype(kv.dtype), kv[:, :, N_PAGED_SLOTS:]], axis=1)
        o_win = window_attend(q, q_pos_s, win[:, :, 0], win[:, :, 1], k_pos_win_s)
        hs = residual_block(hs, conv_y, gate_merge(o_cmp, o_sel, o_win, gl), w_out[l], g_mlp_norm[l], w_up[l], w_down[l])
        kv_s.append(kv[:, :, :N_PAGED_SLOTS])
        win_s.append(win[:, -w_buf:])
        conv_s.append(conv_st)
    y_prompt = rms_norm(hp, g_final)
    y_sample = rms_norm(hs, g_final)
    return (y_prompt, y_sample, jnp.stack(kv_p), jnp.stack(win_p), jnp.stack(conv_p),
            jnp.stack(kv_s), jnp.stack(win_s), jnp.stack(conv_s))
```

```python
import functools

import jax
import jax.numpy as jnp
from jax import lax
from jax.experimental import pallas as pl
from jax.experimental.pallas import tpu as pltpu

F32 = jnp.float32
BF16 = jnp.bfloat16

D_MODEL = 1024
C_CONV = 512
CONV_WIDTH = 31
N_HEADS = 8
HEAD_DIM = 64
N_KV_HEADS = 2
GROUP = N_HEADS // N_KV_HEADS
KV_W = N_KV_HEADS * HEAD_DIM
N_PAGED_SLOTS = 4
D_FF = 4 * D_MODEL
CMP_BLOCK = 32
SEL_BLOCK = 64
TOP_N = 16
N_LOCAL = 2
WINDOW = 512
FORCE_BONUS = float(GROUP + 1)
NEG_INF = -1e30
RMS_EPS = 1e-6
LN_EPS = 1e-5
ATTN_SCALE = HEAD_DIM ** -0.5

COL_Q = 2 * C_CONV
COL_KV = COL_Q + N_HEADS * HEAD_DIM
COL_WIN = COL_KV + N_PAGED_SLOTS * KV_W
COL_GATE = COL_WIN + 2 * KV_W
IN_COLS = COL_GATE + 3 * N_HEADS
LANES = 128
IN_COLS_PAD = -(-IN_COLS // LANES) * LANES

VMEM_LIMIT = 56 * 1024 * 1024


def _cparams(*sem):
    return pltpu.CompilerParams(dimension_semantics=sem, vmem_limit_bytes=VMEM_LIMIT)


def _const_spec(shape):
    return pl.BlockSpec(shape, lambda *_: (0,) * len(shape))


def _proj_kernel(x_ref, g_ref, w_ref, glu_ref, q_ref, kv_ref, kvw_ref, gate_ref):
    x = x_ref[...]
    ms = jnp.mean(x * x, axis=-1, keepdims=True)
    xn = (x * lax.rsqrt(ms + RMS_EPS) * g_ref[...]).astype(BF16)
    z = jnp.dot(xn, w_ref[...], preferred_element_type=F32)
    glu_ref[...] = z[:, :C_CONV] * jax.nn.sigmoid(z[:, C_CONV:COL_Q])
    q_ref[...] = z[:, COL_Q:COL_KV]
    kv_ref[...] = z[:, COL_KV:COL_WIN]
    kvw_ref[...] = z[:, COL_WIN:COL_GATE]
    gate_ref[...] = jax.nn.sigmoid(z[:, COL_GATE:IN_COLS_PAD])


def _project(x, g, w_pad, tm):
    n = x.shape[0]
    row = lambda w: pl.BlockSpec((tm, w), lambda i: (i, 0))
    widths = (C_CONV, N_HEADS * HEAD_DIM, N_PAGED_SLOTS * KV_W, 2 * KV_W, LANES)
    return pl.pallas_call(
        _proj_kernel,
        grid=(n // tm,),
        in_specs=[row(D_MODEL), _const_spec((1, D_MODEL)), _const_spec((D_MODEL, IN_COLS_PAD))],
        out_specs=[row(w) for w in widths],
        out_shape=[jax.ShapeDtypeStruct((n, w), F32) for w in widths],
        compiler_params=_cparams("parallel"),
        name="projection",
    )(x, g, w_pad)


def _ln_swish(y, lng, lnb):
    mu = jnp.mean(y, axis=-1, keepdims=True)
    d = y - mu
    var = jnp.mean(d * d, axis=-1, keepdims=True)
    y = d * lax.rsqrt(var + LN_EPS) * lng + lnb
    return y * jax.nn.sigmoid(y)


CONV_ROWS = 64
CONV_PAD = 32


def _conv_prompt_kernel(glu_ref, w_ref, b_ref, lng_ref, lnb_ref, y_ref, ext_ref):
    seq = glu_ref.shape[1]
    ext_ref[0:CONV_PAD, :] = jnp.zeros((CONV_PAD, C_CONV), F32)
    ext_ref[CONV_PAD:, :] = glu_ref[0]
    shift = CONV_PAD - (CONV_WIDTH - 1)

    def body(c, carry):
        base = pl.multiple_of(c * CONV_ROWS, CONV_ROWS)
        win = ext_ref[pl.ds(base, CONV_ROWS + CONV_PAD), :]
        acc = jnp.zeros((CONV_ROWS, C_CONV), F32)
        for j in range(CONV_WIDTH):
            acc = acc + w_ref[j:j + 1, :] * win[j + shift:j + shift + CONV_ROWS, :]
        y_ref[0, pl.ds(base, CONV_ROWS), :] = _ln_swish(acc + b_ref[...], lng_ref[...], lnb_ref[...])
        return carry

    lax.fori_loop(0, seq // CONV_ROWS, body, 0)


def _conv_prompt(glu, w_dw, b_dw, lng, lnb):
    nb, seq, _ = glu.shape
    blk = pl.BlockSpec((1, seq, C_CONV), lambda b: (b, 0, 0))
    return pl.pallas_call(
        _conv_prompt_kernel,
        grid=(nb,),
        in_specs=[blk, _const_spec((CONV_WIDTH, C_CONV))] + [_const_spec((1, C_CONV))] * 3,
        out_specs=blk,
        out_shape=jax.ShapeDtypeStruct(glu.shape, F32),
        scratch_shapes=[pltpu.VMEM((seq + CONV_PAD, C_CONV), F32)],
        compiler_params=_cparams("parallel"),
        name="conv_prompt",
    )(glu, w_dw, b_dw, lng, lnb)


def _conv_sample_kernel(st_ref, glu_ref, w_ref, b_ref, lng_ref, lnb_ref, y_ref, st_out_ref):
    n_prev = CONV_WIDTH - 1
    glu = glu_ref[...]
    acc = w_ref[n_prev:n_prev + 1, :] * glu
    for j in range(n_prev):
        acc = acc + w_ref[j:j + 1, :] * st_ref[:, j * C_CONV:(j + 1) * C_CONV]
    y_ref[...] = _ln_swish(acc + b_ref[...], lng_ref[...], lnb_ref[...])
    st_out_ref[:, :(n_prev - 1) * C_CONV] = st_ref[:, C_CONV:]
    st_out_ref[:, (n_prev - 1) * C_CONV:] = glu


def _conv_sample(state2d, glu, w_dw, b_dw, lng, lnb):
    nb, w = state2d.shape
    return pl.pallas_call(
        _conv_sample_kernel,
        grid=(1,),
        in_specs=[_const_spec((nb, w)), _const_spec((nb, C_CONV)), _const_spec((CONV_WIDTH, C_CONV))]
        + [_const_spec((1, C_CONV))] * 3,
        out_specs=[_const_spec((nb, C_CONV)), _const_spec((nb, w))],
        out_shape=[jax.ShapeDtypeStruct((nb, C_CONV), F32), jax.ShapeDtypeStruct((nb, w), F32)],
        compiler_params=_cparams("arbitrary"),
        name="conv_sample",
    )(state2d, glu, w_dw, b_dw, lng, lnb)


def _lane_iota(shape):
    return lax.broadcasted_iota(jnp.int32, shape, len(shape) - 1)


def _head_rows(q, kvh):
    rows = q.shape[0]
    half = _lane_iota((rows, KV_W)) // HEAD_DIM
    out = []
    for g in range(GROUP):
        h = kvh * GROUP + g
        slab = q[:, KV_W * (h // 2):KV_W * (h // 2 + 1)]
        if h % 2 != kvh:
            slab = pltpu.roll(slab, HEAD_DIM, 1)
        out.append(jnp.where(half == kvh, slab * ATTN_SCALE, 0.0))
    return out


def _dot_nt(a, b):
    return lax.dot_general(a, b, (((1,), (1,)), ((), ())), preferred_element_type=F32)


def _masked_softmax_pv(s, ok, v):
    s = jnp.where(ok, s, NEG_INF)
    m = jnp.max(s, axis=-1, keepdims=True)
    p = jnp.where(ok, jnp.exp(s - m), 0.0)
    l = jnp.sum(p, axis=-1, keepdims=True)
    inv = jnp.where(l > 0.0, 1.0 / l, 0.0)
    return p * inv, jnp.dot(p.astype(BF16), v, preferred_element_type=F32) * inv


def _pool_blocks(x, w):
    t = x.shape[0]
    return jnp.sum(x.reshape(t // CMP_BLOCK, CMP_BLOCK, KV_W) * w[None], axis=1)


def _select_blocks(psum, q_pos, n_blk):
    shape = psum.shape
    lane = _lane_iota(shape)
    even = (lane & 1) == 0
    pair = psum + jnp.where(even, pltpu.roll(psum, LANES - 1, 1), pltpu.roll(psum, 1, 1))
    blk = lane // (SEL_BLOCK // CMP_BLOCK)
    back = q_pos // SEL_BLOCK - blk
    forced = (blk == 0) | ((back >= 0) & (back < N_LOCAL))
    score = jnp.where(back >= 0, pair + jnp.where(forced, FORCE_BONUS, 0.0), -jnp.inf)
    rank = jnp.zeros(shape, jnp.int32)
    for i in range(n_blk):
        si = score[:, 2 * i:2 * i + 1]
        beats = (si > score) | ((si == score) & (i < blk))
        rank = rank + beats.astype(jnp.int32)
    return jnp.where((rank < min(TOP_N, n_blk)) & even, 1.0, 0.0)


def _block_expander(n_keys):
    li = lax.broadcasted_iota(jnp.int32, (LANES, n_keys), 0)
    ti = lax.broadcasted_iota(jnp.int32, (LANES, n_keys), 1)
    return jnp.where(((li & 1) == 0) & ((li >> 1) == ti // SEL_BLOCK), 1.0, 0.0).astype(BF16)


def _place_heads(o_by_kvh, gates):
    rows = gates.shape[0]
    low = _lane_iota((rows, KV_W)) < HEAD_DIM
    slabs = []
    for s in range(N_HEADS // 2):
        pair = []
        for h in (2 * s, 2 * s + 1):
            kvh, g = divmod(h, GROUP)
            o = sum(o_by_kvh[kvh][g][k] * gates[:, 3 * h + k:3 * h + k + 1] for k in range(3))
            if kvh != h % 2:
                o = pltpu.roll(o, HEAD_DIM, 1)
            pair.append(o)
        slabs.append(jnp.where(low, pair[0], pair[1]))
    return jnp.concatenate(slabs, axis=1)


QB = 128
KT = 256
WIN_KEYS = WINDOW + QB


def _attn_prompt_kernel(q_ref, kv_ref, kvw_ref, gate_ref, wck_ref, wcv_ref, out_ref,
                        kc_ref, vc_ref, ks_ref, vs_ref, kw_ref, vw_ref, exp_ref, mask_ref,
                        m_ref, l_ref, acc_ref):
    seq = kv_ref.shape[1]
    n_cmp = seq // CMP_BLOCK
    n_blk = seq // SEL_BLOCK
    n_kt = seq // KT
    qi = pl.program_id(1)
    qs = qi * QB

    @pl.when((pl.program_id(0) == 0) & (qi == 0))
    def _():
        exp_ref[...] = _block_expander(seq)

    @pl.when(qi == 0)
    def _():
        kc_ref[...] = jnp.zeros(kc_ref.shape, BF16)
        vc_ref[...] = jnp.zeros(vc_ref.shape, BF16)
        kc_ref[0:n_cmp, :] = _pool_blocks(kv_ref[0, :, 0:KV_W], wck_ref[...]).astype(BF16)
        vc_ref[0:n_cmp, :] = _pool_blocks(kv_ref[0, :, KV_W:2 * KV_W], wcv_ref[...]).astype(BF16)
        ks_ref[...] = kv_ref[0, :, 2 * KV_W:3 * KV_W].astype(BF16)
        vs_ref[...] = kv_ref[0, :, 3 * KV_W:4 * KV_W].astype(BF16)
        kw_ref[...] = kvw_ref[0, :, 0:KV_W].astype(BF16)
        vw_ref[...] = kvw_ref[0, :, KV_W:2 * KV_W].astype(BF16)

    q = q_ref[0]
    q_pos = qs + lax.broadcasted_iota(jnp.int32, (QB, 1), 0)
    rep = lambda a: jnp.concatenate([a] * GROUP, axis=0)
    q_pos_g = rep(q_pos)

    win_start = pl.multiple_of(jnp.maximum(qs - WINDOW, 0), QB)
    rel = q_pos - (win_start + _lane_iota((QB, WIN_KEYS)))
    win_ok = rep((rel >= 0) & (rel <= WINDOW))
    n_tiles = (qs + QB + KT - 1) // KT

    o_by_kvh = []
    for kvh in range(N_KV_HEADS):
        qp = jnp.concatenate(_head_rows(q, kvh), axis=0).astype(BF16)

        c_end = (_lane_iota((GROUP * QB, LANES)) + 1) * CMP_BLOCK - 1
        p_cmp, o_cmp = _masked_softmax_pv(_dot_nt(qp, kc_ref[...]), c_end <= q_pos_g, vc_ref[...])
        psum = sum(p_cmp[g * QB:(g + 1) * QB] for g in range(GROUP))

        sel = _select_blocks(psum, q_pos, n_blk).astype(BF16)
        selexp = jnp.dot(sel, exp_ref[...], preferred_element_type=F32)
        for t in range(n_kt):
            mask_ref[t] = selexp[:, t * KT:(t + 1) * KT]
        m_ref[...] = jnp.full(m_ref.shape, NEG_INF, F32)
        l_ref[...] = jnp.zeros(l_ref.shape, F32)
        acc_ref[...] = jnp.zeros(acc_ref.shape, F32)

        def sel_tile(t, carry):
            k0 = pl.multiple_of(t * KT, KT)
            s = _dot_nt(qp, ks_ref[pl.ds(k0, KT), :])
            ok = rep((mask_ref[t] > 0.5) & (k0 + _lane_iota((QB, KT)) <= q_pos))
            s = jnp.where(ok, s, NEG_INF)
            m_prev = m_ref[...]
            m_new = jnp.maximum(m_prev, jnp.max(s, axis=-1, keepdims=True))
            alpha = jnp.exp(m_prev - m_new)
            p = jnp.where(ok, jnp.exp(s - m_new), 0.0)
            l_ref[...] = alpha * l_ref[...] + jnp.sum(p, axis=-1, keepdims=True)
            acc_ref[...] = alpha * acc_ref[...] + jnp.dot(p.astype(BF16), vs_ref[pl.ds(k0, KT), :],
                                                          preferred_element_type=F32)
            m_ref[...] = m_new
            return carry

        lax.fori_loop(0, n_tiles, sel_tile, 0)
        l = l_ref[...]
        o_sel = acc_ref[...] * jnp.where(l > 0.0, 1.0 / l, 0.0)

        s_w = _dot_nt(qp, kw_ref[pl.ds(win_start, WIN_KEYS), :])
        _, o_win = _masked_softmax_pv(s_w, win_ok, vw_ref[pl.ds(win_start, WIN_KEYS), :])

        o_by_kvh.append([tuple(o[g * QB:(g + 1) * QB] for o in (o_cmp, o_sel, o_win)) for g in range(GROUP)])

    out_ref[0] = _place_heads(o_by_kvh, gate_ref[0])


def _attn_prompt(q, kv, kvw, gate, wck, wcv):
    nb, seq, _ = q.shape
    per_q = lambda w: pl.BlockSpec((1, QB, w), lambda b, i: (b, i, 0))
    per_b = lambda w: pl.BlockSpec((1, seq, w), lambda b, i: (b, 0, 0))
    wspec = pl.BlockSpec((CMP_BLOCK, KV_W), lambda b, i: (0, 0))
    rows = GROUP * QB
    return pl.pallas_call(
        _attn_prompt_kernel,
        grid=(nb, seq // QB),
        in_specs=[per_q(N_HEADS * HEAD_DIM), per_b(N_PAGED_SLOTS * KV_W), per_b(2 * KV_W), per_q(LANES), wspec, wspec],
        out_specs=per_q(N_HEADS * HEAD_DIM),
        out_shape=jax.ShapeDtypeStruct(q.shape, F32),
        scratch_shapes=[pltpu.VMEM((LANES, KV_W), BF16), pltpu.VMEM((LANES, KV_W), BF16)]
        + [pltpu.VMEM((seq, KV_W), BF16)] * 4
        + [pltpu.VMEM((LANES, seq), BF16), pltpu.VMEM((seq // KT, QB, KT), F32),
           pltpu.VMEM((rows, 1), F32), pltpu.VMEM((rows, 1), F32), pltpu.VMEM((rows, KV_W), F32)],
        compiler_params=_cparams("arbitrary", "arbitrary"),
        name="attn_prompt",
    )(q, kv, kvw, gate, wck, wcv)


def _attn_sample_kernel(pt_ref, q_ref, kvn_ref, kvwn_ref, gate_ref, wck_ref, wcv_ref, cache_ref, win_ref,
                        out_ref, win_out_ref, past_ref, exp_ref, sem):
    b = pl.program_id(0)
    n_pages = pt_ref.shape[1]
    page = cache_ref.shape[1]
    past_len = n_pages * page
    n_cmp = past_len // CMP_BLOCK
    n_blk = past_len // SEL_BLOCK + 1
    w_buf = win_ref.shape[1]

    def page_copy(p):
        return pltpu.make_async_copy(cache_ref.at[pt_ref[b, p]], past_ref.at[pl.ds(p * page, page)], sem.at[p])

    for p in range(n_pages):
        page_copy(p).start()

    @pl.when(b == 0)
    def _():
        exp_ref[...] = _block_expander(past_len)

    q = q_ref[0]
    q_pos = jnp.full((N_HEADS, 1), past_len, jnp.int32)
    qp_f = jnp.concatenate([r for kvh in range(N_KV_HEADS) for r in _head_rows(q, kvh)], axis=0)
    qp = qp_f.astype(BF16)
    kv_new = kvn_ref[0]
    kvw_new = kvwn_ref[0]

    def with_new_key(s_past, ok_past, v_past, k_new, v_new, ok_new):
        s_new = jnp.sum(qp_f * k_new, axis=-1, keepdims=True)
        s_past = jnp.where(ok_past, s_past, NEG_INF)
        s_new = jnp.where(ok_new, s_new, NEG_INF)
        m = jnp.maximum(jnp.max(s_past, axis=-1, keepdims=True), s_new)
        p_past = jnp.where(ok_past, jnp.exp(s_past - m), 0.0)
        p_new = jnp.where(ok_new, jnp.exp(s_new - m), 0.0)
        l = jnp.sum(p_past, axis=-1, keepdims=True) + p_new
        o = jnp.dot(p_past.astype(BF16), v_past, preferred_element_type=F32) + p_new * v_new
        return o * jnp.where(l > 0.0, 1.0 / l, 0.0)

    win = win_ref[0]
    rel = past_len - (past_len - w_buf + _lane_iota((N_HEADS, w_buf)))
    o_win = with_new_key(_dot_nt(qp, win[:, 0:KV_W].astype(BF16)), (rel >= 0) & (rel <= WINDOW),
                         win[:, KV_W:2 * KV_W].astype(BF16), kvw_new[:, 0:KV_W], kvw_new[:, KV_W:2 * KV_W],
                         jnp.full((N_HEADS, 1), True))
    row = lax.broadcasted_iota(jnp.int32, win.shape, 0)
    win_out_ref[0] = jnp.where(row == w_buf - 1, kvw_new, pltpu.roll(win, w_buf - 1, 0))

    for p in range(n_pages):
        page_copy(p).wait()

    kc = _pool_blocks(past_ref[:, 0:KV_W], wck_ref[...]).astype(BF16)
    vc = _pool_blocks(past_ref[:, KV_W:2 * KV_W], wcv_ref[...]).astype(BF16)
    kc = jnp.concatenate([kc, jnp.zeros((LANES - n_cmp, KV_W), BF16)], axis=0)
    vc = jnp.concatenate([vc, jnp.zeros((LANES - n_cmp, KV_W), BF16)], axis=0)
    c_end = (_lane_iota((N_HEADS, LANES)) + 1) * CMP_BLOCK - 1
    p_cmp, o_cmp = _masked_softmax_pv(_dot_nt(qp, kc), c_end <= q_pos, vc)
    psum = jnp.concatenate(
        [jnp.broadcast_to(jnp.sum(p_cmp[kvh * GROUP:(kvh + 1) * GROUP], axis=0, keepdims=True), (GROUP, LANES))
         for kvh in range(N_KV_HEADS)], axis=0)

    sel = _select_blocks(psum, q_pos, n_blk)
    selexp = jnp.dot(sel.astype(BF16), exp_ref[...], preferred_element_type=F32)
    new_blk = past_len // SEL_BLOCK
    ok_new = sel[:, 2 * new_blk:2 * new_blk + 1] > 0.5
    ok_past = (selexp > 0.5) & (_lane_iota((N_HEADS, past_len)) <= q_pos)
    o_sel = with_new_key(_dot_nt(qp, past_ref[:, 2 * KV_W:3 * KV_W].astype(BF16)), ok_past,
                         past_ref[:, 3 * KV_W:4 * KV_W].astype(BF16),
                         kv_new[:, 2 * KV_W:3 * KV_W], kv_new[:, 3 * KV_W:4 * KV_W], ok_new)

    o_by_kvh = [[tuple(o[kvh * GROUP + g:kvh * GROUP + g + 1] for o in (o_cmp, o_sel, o_win))
                 for g in range(GROUP)] for kvh in range(N_KV_HEADS)]
    out_ref[0] = _place_heads(o_by_kvh, gate_ref[0])


def _attn_sample(page_table, q, kv_new, kvw_new, gate, wck, wcv, cache, cache_win):
    nb, n_pages = page_table.shape
    page = cache.shape[1]
    w_buf = cache_win.shape[1]
    row3 = lambda w: pl.BlockSpec((1, 1, w), lambda b, pt: (b, 0, 0))
    wspec = pl.BlockSpec((CMP_BLOCK, KV_W), lambda b, pt: (0, 0))
    win_spec = pl.BlockSpec((1, w_buf, 2 * KV_W), lambda b, pt: (b, 0, 0))
    grid_spec = pltpu.PrefetchScalarGridSpec(
        num_scalar_prefetch=1,
        grid=(nb,),
        in_specs=[row3(N_HEADS * HEAD_DIM), row3(N_PAGED_SLOTS * KV_W), row3(2 * KV_W), row3(LANES), wspec, wspec,
                  pl.BlockSpec(memory_space=pl.ANY), win_spec],
        out_specs=[row3(N_HEADS * HEAD_DIM), win_spec],
        scratch_shapes=[pltpu.VMEM((n_pages * page, N_PAGED_SLOTS * KV_W), F32),
                        pltpu.VMEM((LANES, n_pages * page), BF16),
                        pltpu.SemaphoreType.DMA((n_pages,))],
    )
    r3 = lambda a: a.reshape(nb, 1, a.shape[-1])
    return pl.pallas_call(
        _attn_sample_kernel,
        grid_spec=grid_spec,
        out_shape=[jax.ShapeDtypeStruct((nb, 1, N_HEADS * HEAD_DIM), F32),
                   jax.ShapeDtypeStruct(cache_win.shape, F32)],
        compiler_params=_cparams("arbitrary"),
        name="attn_sample",
    )(page_table, r3(q), r3(kv_new), r3(kvw_new), r3(gate), wck, wcv, cache, cache_win)


FF_CHUNK = 1024


def _rms(x, g):
    return x * lax.rsqrt(jnp.mean(x * x, axis=-1, keepdims=True) + RMS_EPS) * g


def _mlp_kernel(x_ref, conv_ref, attn_ref, wout_ref, gm_ref, wup_ref, wdn_ref, gf_ref, y_ref, h_ref):
    h_ref[...] = (x_ref[...]
                  + jnp.dot(conv_ref[...].astype(BF16), wout_ref[0:C_CONV, :], preferred_element_type=F32)
                  + jnp.dot(attn_ref[...].astype(BF16), wout_ref[C_CONV:, :], preferred_element_type=F32))
    hn = _rms(h_ref[...], gm_ref[...]).astype(BF16)
    acc = None
    for c in range(D_FF // FF_CHUNK):
        m = jnp.dot(hn, wup_ref[:, c * FF_CHUNK:(c + 1) * FF_CHUNK], preferred_element_type=F32)
        m = jnp.maximum(m, 0.0)
        t = jnp.dot((m * m).astype(BF16), wdn_ref[c * FF_CHUNK:(c + 1) * FF_CHUNK, :], preferred_element_type=F32)
        acc = t if acc is None else acc + t
    y_ref[...] = _rms(h_ref[...] + acc, gf_ref[...])


def _mlp(x, conv_y, attn_y, w_out, g_mlp, w_up, w_down, g_final, tm):
    n = x.shape[0]
    row = lambda w: pl.BlockSpec((tm, w), lambda i: (i, 0))
    once = lambda shape: pl.BlockSpec(shape, lambda i: (0,) * len(shape), pipeline_mode=pl.Buffered(1))
    return pl.pallas_call(
        _mlp_kernel,
        grid=(n // tm,),
        in_specs=[row(D_MODEL), row(C_CONV), row(N_HEADS * HEAD_DIM), once((D_MODEL, D_MODEL)), once((1, D_MODEL)),
                  once((D_MODEL, D_FF)), once((D_FF, D_MODEL)), once((1, D_MODEL))],
        out_specs=row(D_MODEL),
        out_shape=jax.ShapeDtypeStruct((n, D_MODEL), F32),
        scratch_shapes=[pltpu.VMEM((tm, D_MODEL), F32)],
        compiler_params=_cparams("parallel"),
        name="mlp",
    )(x, conv_y, attn_y, w_out, g_mlp, w_up, w_down, g_final)


def kernel(x_prompt, x_sample, cache_kv, cache_win, state_conv, page_table, g_attn_norm, w_in, w_dw, b_dw,
           conv_ln_g, conv_ln_b, w_cmp_k, w_cmp_v, w_out, g_mlp_norm, w_up, w_down, g_final):
    depth = w_in.shape[0]
    assert depth == 1, "single-layer trunk"
    nb, seq, _ = x_prompt.shape
    db, dseq, _ = x_sample.shape
    assert dseq == 1
    n_phys, page = cache_kv.shape[1], cache_kv.shape[2]
    w_buf = cache_win.shape[2]

    w_pad = jnp.pad(w_in[0], ((0, 0), (0, IN_COLS_PAD - IN_COLS))).astype(BF16)
    wck = jnp.repeat(w_cmp_k[0], HEAD_DIM, axis=1)
    wcv = jnp.repeat(w_cmp_v[0], HEAD_DIM, axis=1)
    w_out_b, w_up_b, w_down_b = w_out[0].astype(BF16), w_up[0].astype(BF16), w_down[0].astype(BF16)
    row = lambda a: a.reshape(1, -1)
    g_attn, g_mlp, g_fin = row(g_attn_norm[0]), row(g_mlp_norm[0]), row(g_final)
    conv_w = (w_dw[0], row(b_dw[0]), row(conv_ln_g[0]), row(conv_ln_b[0]))
    mlp_w = (w_out_b, g_mlp, w_up_b, w_down_b, g_fin)

    xp = x_prompt.reshape(nb * seq, D_MODEL)
    glu, q, kv, kvw, gate = _project(xp, g_attn, w_pad, 512)
    per_b = lambda a: a.reshape(nb, seq, a.shape[-1])
    glu3, kvw3 = per_b(glu), per_b(kvw)
    conv_y = _conv_prompt(glu3, *conv_w)
    attn_y = _attn_prompt(per_b(q), per_b(kv), kvw3, per_b(gate), wck, wcv)
    y_prompt = _mlp(xp, conv_y.reshape(nb * seq, C_CONV), attn_y.reshape(nb * seq, -1), *mlp_w, 512)
    new_kv_p = kv.reshape(1, nb, seq, N_PAGED_SLOTS, N_KV_HEADS, HEAD_DIM)
    n_win = min(WINDOW, seq)
    new_win_p = kvw3[:, seq - n_win:].reshape(1, nb, n_win, 2, N_KV_HEADS, HEAD_DIM)
    new_conv_p = glu3[:, seq - (CONV_WIDTH - 1):][None]

    xs = x_sample.reshape(db, D_MODEL)
    glu_s, q_s, kv_s, kvw_s, gate_s = _project(xs, g_attn, w_pad, db)
    conv_y_s, conv_st = _conv_sample(state_conv[0].reshape(db, (CONV_WIDTH - 1) * C_CONV), glu_s, *conv_w)
    attn_s, new_win_s = _attn_sample(page_table, q_s, kv_s, kvw_s, gate_s, wck, wcv,
                                     cache_kv[0].reshape(n_phys, page, N_PAGED_SLOTS * KV_W),
                                     cache_win[0].reshape(db, w_buf, 2 * KV_W))
    y_sample = _mlp(xs, conv_y_s, attn_s.reshape(db, -1), *mlp_w, db)

    return (y_prompt.reshape(nb, seq, D_MODEL), y_sample.reshape(db, 1, D_MODEL), new_kv_p, new_win_p, new_conv_p,
            kv_s.reshape(1, db, 1, N_PAGED_SLOTS, N_KV_HEADS, HEAD_DIM),
            new_win_s.reshape(1, db, w_buf, 2, N_KV_HEADS, HEAD_DIM),
            conv_st.reshape(1, db, CONV_WIDTH - 1, C_CONV))
```

```python
import functools

import jax
import jax.numpy as jnp
from jax import lax
from jax.experimental import pallas as pl
from jax.experimental.pallas import tpu as pltpu

F32 = jnp.float32
BF16 = jnp.bfloat16

D_MODEL = 1024
C_CONV = 512
CONV_WIDTH = 31
N_HEADS = 8
HEAD_DIM = 64
N_KV_HEADS = 2
GROUP = N_HEADS // N_KV_HEADS
KV_W = N_KV_HEADS * HEAD_DIM
N_PAGED_SLOTS = 4
D_FF = 4 * D_MODEL
CMP_BLOCK = 32
SEL_BLOCK = 64
TOP_N = 16
N_LOCAL = 2
WINDOW = 512
FORCE_BONUS = float(GROUP + 1)
NEG_INF = -1e30
RMS_EPS = 1e-6
LN_EPS = 1e-5
ATTN_SCALE = HEAD_DIM ** -0.5

COL_Q = 2 * C_CONV
COL_KV = COL_Q + N_HEADS * HEAD_DIM
COL_WIN = COL_KV + N_PAGED_SLOTS * KV_W
COL_GATE = COL_WIN + 2 * KV_W
IN_COLS = COL_GATE + 3 * N_HEADS
LANES = 128
IN_COLS_PAD = -(-IN_COLS // LANES) * LANES

VMEM_LIMIT = 56 * 1024 * 1024


def _cparams(*sem):
    return pltpu.CompilerParams(dimension_semantics=sem, vmem_limit_bytes=VMEM_LIMIT)


def _const_spec(shape):
    return pl.BlockSpec(shape, lambda *_: (0,) * len(shape))


def _proj_kernel(x_ref, g_ref, w_ref, glu_ref, q_ref, kv_ref, kvw_ref, gate_ref):
    x = x_ref[...]
    ms = jnp.mean(x * x, axis=-1, keepdims=True)
    xn = (x * lax.rsqrt(ms + RMS_EPS) * g_ref[...]).astype(BF16)
    z = jnp.dot(xn, w_ref[...], preferred_element_type=F32)
    glu_ref[...] = z[:, :C_CONV] * jax.nn.sigmoid(z[:, C_CONV:COL_Q])
    q_ref[...] = z[:, COL_Q:COL_KV]
    kv_ref[...] = z[:, COL_KV:COL_WIN]
    kvw_ref[...] = z[:, COL_WIN:COL_GATE]
    gate_ref[...] = jax.nn.sigmoid(z[:, COL_GATE:IN_COLS_PAD])


def _project(x, g, w_pad, tm):
    n = x.shape[0]
    row = lambda w: pl.BlockSpec((tm, w), lambda i: (i, 0))
    widths = (C_CONV, N_HEADS * HEAD_DIM, N_PAGED_SLOTS * KV_W, 2 * KV_W, LANES)
    return pl.pallas_call(
        _proj_kernel,
        grid=(n // tm,),
        in_specs=[row(D_MODEL), _const_spec((1, D_MODEL)), _const_spec((D_MODEL, IN_COLS_PAD))],
        out_specs=[row(w) for w in widths],
        out_shape=[jax.ShapeDtypeStruct((n, w), F32) for w in widths],
        compiler_params=_cparams("parallel"),
        name="projection",
    )(x, g, w_pad)


def _ln_swish(y, lng, lnb):
    mu = jnp.mean(y, axis=-1, keepdims=True)
    d = y - mu
    var = jnp.mean(d * d, axis=-1, keepdims=True)
    y = d * lax.rsqrt(var + LN_EPS) * lng + lnb
    return y * jax.nn.sigmoid(y)


CONV_ROWS = 64
CONV_PAD = 32


def _conv_prompt_kernel(glu_ref, w_ref, b_ref, lng_ref, lnb_ref, y_ref, ext_ref):
    seq = glu_ref.shape[1]
    ext_ref[0:CONV_PAD, :] = jnp.zeros((CONV_PAD, C_CONV), F32)
    ext_ref[CONV_PAD:, :] = glu_ref[0]
    shift = CONV_PAD - (CONV_WIDTH - 1)

    def body(c, carry):
        base = pl.multiple_of(c * CONV_ROWS, CONV_ROWS)
        win = ext_ref[pl.ds(base, CONV_ROWS + CONV_PAD), :]
        acc = jnp.zeros((CONV_ROWS, C_CONV), F32)
        for j in range(CONV_WIDTH):
            acc = acc + w_ref[j:j + 1, :] * win[j + shift:j + shift + CONV_ROWS, :]
        y_ref[0, pl.ds(base, CONV_ROWS), :] = _ln_swish(acc + b_ref[...], lng_ref[...], lnb_ref[...])
        return carry

    lax.fori_loop(0, seq // CONV_ROWS, body, 0)


def _conv_prompt(glu, w_dw, b_dw, lng, lnb):
    nb, seq, _ = glu.shape
    blk = pl.BlockSpec((1, seq, C_CONV), lambda b: (b, 0, 0))
    return pl.pallas_call(
        _conv_prompt_kernel,
        grid=(nb,),
        in_specs=[blk, _const_spec((CONV_WIDTH, C_CONV))] + [_const_spec((1, C_CONV))] * 3,
        out_specs=blk,
        out_shape=jax.ShapeDtypeStruct(glu.shape, F32),
        scratch_shapes=[pltpu.VMEM((seq + CONV_PAD, C_CONV), F32)],
        compiler_params=_cparams("parallel"),
        name="conv_prompt",
    )(glu, w_dw, b_dw, lng, lnb)


def _conv_sample_kernel(st_ref, glu_ref, w_ref, b_ref, lng_ref, lnb_ref, y_ref, st_out_ref):
    n_prev = CONV_WIDTH - 1
    glu = glu_ref[...]
    acc = w_ref[n_prev:n_prev + 1, :] * glu
    for j in range(n_prev):
        acc = acc + w_ref[j:j + 1, :] * st_ref[:, j * C_CONV:(j + 1) * C_CONV]
    y_ref[...] = _ln_swish(acc + b_ref[...], lng_ref[...], lnb_ref[...])
    st_out_ref[:, :(n_prev - 1) * C_CONV] = st_ref[:, C_CONV:]
    st_out_ref[:, (n_prev - 1) * C_CONV:] = glu


def _conv_sample(state2d, glu, w_dw, b_dw, lng, lnb):
    nb, w = state2d.shape
    return pl.pallas_call(
        _conv_sample_kernel,
        grid=(1,),
        in_specs=[_const_spec((nb, w)), _const_spec((nb, C_CONV)), _const_spec((CONV_WIDTH, C_CONV))]
        + [_const_spec((1, C_CONV))] * 3,
        out_specs=[_const_spec((nb, C_CONV)), _const_spec((nb, w))],
        out_shape=[jax.ShapeDtypeStruct((nb, C_CONV), F32), jax.ShapeDtypeStruct((nb, w), F32)],
        compiler_params=_cparams("arbitrary"),
        name="conv_sample",
    )(state2d, glu, w_dw, b_dw, lng, lnb)


def _lane_iota(shape):
    return lax.broadcasted_iota(jnp.int32, shape, len(shape) - 1)


def _head_rows(q, kvh):
    rows = q.shape[0]
    half = _lane_iota((rows, KV_W)) // HEAD_DIM
    out = []
    for g in range(GROUP):
        h = kvh * GROUP + g
        slab = q[:, KV_W * (h // 2):KV_W * (h // 2 + 1)]
        if h % 2 != kvh:
            slab = pltpu.roll(slab, HEAD_DIM, 1)
        out.append(jnp.where(half == kvh, slab * ATTN_SCALE, 0.0))
    return out


def _dot_nt(a, b):
    return lax.dot_general(a, b, (((1,), (1,)), ((), ())), preferred_element_type=F32)


def _masked_softmax_pv(s, ok, v):
    s = jnp.where(ok, s, NEG_INF)
    m = jnp.max(s, axis=-1, keepdims=True)
    p = jnp.where(ok, jnp.exp(s - m), 0.0)
    l = jnp.sum(p, axis=-1, keepdims=True)
    inv = jnp.where(l > 0.0, 1.0 / l, 0.0)
    return p * inv, jnp.dot(p.astype(BF16), v, preferred_element_type=F32) * inv


def _pool_blocks(x, w):
    t = x.shape[0]
    return jnp.sum(x.reshape(t // CMP_BLOCK, CMP_BLOCK, KV_W) * w[None], axis=1)


def _select_blocks(psum, q_pos, n_blk):
    shape = psum.shape
    lane = _lane_iota(shape)
    even = (lane & 1) == 0
    pair = psum + jnp.where(even, pltpu.roll(psum, LANES - 1, 1), pltpu.roll(psum, 1, 1))
    blk = lane // (SEL_BLOCK // CMP_BLOCK)
    back = q_pos // SEL_BLOCK - blk
    forced = (blk == 0) | ((back >= 0) & (back < N_LOCAL))
    score = jnp.where(back >= 0, pair + jnp.where(forced, FORCE_BONUS, 0.0), -jnp.inf)
    rank = jnp.zeros(shape, jnp.int32)
    for i in range(n_blk):
        si = score[:, 2 * i:2 * i + 1]
        beats = (si > score) | ((si == score) & (i < blk))
        rank = rank + beats.astype(jnp.int32)
    return jnp.where((rank < min(TOP_N, n_blk)) & even, 1.0, 0.0)


def _block_expander(n_keys):
    li = lax.broadcasted_iota(jnp.int32, (LANES, n_keys), 0)
    ti = lax.broadcasted_iota(jnp.int32, (LANES, n_keys), 1)
    return jnp.where(((li & 1) == 0) & ((li >> 1) == ti // SEL_BLOCK), 1.0, 0.0).astype(BF16)


def _place_heads(o_by_kvh, gates):
    rows = gates.shape[0]
    low = _lane_iota((rows, KV_W)) < HEAD_DIM
    slabs = []
    for s in range(N_HEADS // 2):
        pair = []
        for h in (2 * s, 2 * s + 1):
            kvh, g = divmod(h, GROUP)
            o = sum(o_by_kvh[kvh][g][k] * gates[:, 3 * h + k:3 * h + k + 1] for k in range(3))
            if kvh != h % 2:
                o = pltpu.roll(o, HEAD_DIM, 1)
            pair.append(o)
        slabs.append(jnp.where(low, pair[0], pair[1]))
    return jnp.concatenate(slabs, axis=1)


QB = 128
KT = 256
WIN_KEYS = WINDOW + QB


def _select_blocks_t(p_t, q_pos, n_blk):
    nq = q_pos.shape[1]
    rows = 2 * n_blk
    psum = sum(p_t[:, g * nq:(g + 1) * nq] for g in range(GROUP))
    row = lax.broadcasted_iota(jnp.int32, psum.shape, 0)
    pair = psum + jnp.where((row & 1) == 0, pltpu.roll(psum, LANES - 1, 0), pltpu.roll(psum, 1, 0))
    pair = pair[0:rows]
    row = row[0:rows]
    even = (row & 1) == 0
    blk = row // (SEL_BLOCK // CMP_BLOCK)
    back = q_pos // SEL_BLOCK - blk
    forced = (blk == 0) | ((back >= 0) & (back < N_LOCAL))
    score = jnp.where(back >= 0, pair + jnp.where(forced, FORCE_BONUS, 0.0), -jnp.inf)
    rank = jnp.zeros(score.shape, jnp.int32)
    for i in range(n_blk):
        si = score[2 * i:2 * i + 1, :]
        beats = (si > score) | ((si == score) & (i < blk))
        rank = rank + beats.astype(jnp.int32)
    return jnp.where((rank < min(TOP_N, n_blk)) & even, 1.0, 0.0)


def _attn_prompt_kernel(q_ref, kv_ref, kvw_ref, gate_ref, wck_ref, wcv_ref, out_ref,
                        kc_ref, vc_ref, ks_ref, vs_ref, kw_ref, vw_ref, exp_ref,
                        bias_ref, s_ref, mrun_ref, mb_ref, lrun_ref, acc_ref):
    seq = kv_ref.shape[1]
    n_cmp = seq // CMP_BLOCK
    n_blk = seq // SEL_BLOCK
    n_kt = seq // KT
    rows = GROUP * QB
    qi = pl.program_id(1)
    qs = qi * QB

    @pl.when((pl.program_id(0) == 0) & (qi == 0))
    def _():
        exp_ref[...] = _block_expander(seq)

    @pl.when(qi == 0)
    def _():
        kc_ref[...] = jnp.zeros(kc_ref.shape, BF16)
        vc_ref[...] = jnp.zeros(vc_ref.shape, BF16)
        kc_ref[0:n_cmp, :] = _pool_blocks(kv_ref[0, :, 0:KV_W], wck_ref[...]).astype(BF16)
        vc_ref[0:n_cmp, :] = _pool_blocks(kv_ref[0, :, KV_W:2 * KV_W], wcv_ref[...]).astype(BF16)
        ks_ref[...] = kv_ref[0, :, 2 * KV_W:3 * KV_W].astype(BF16)
        vs_ref[...] = kv_ref[0, :, 3 * KV_W:4 * KV_W].astype(BF16)
        kw_ref[...] = kvw_ref[0, :, 0:KV_W].astype(BF16)
        vw_ref[...] = kvw_ref[0, :, KV_W:2 * KV_W].astype(BF16)

    q = q_ref[0]
    q_pos = qs + lax.broadcasted_iota(jnp.int32, (QB, 1), 0)
    q_pos_t = qs + lax.broadcasted_iota(jnp.int32, (1, QB), 1)
    rep = lambda a: jnp.concatenate([a] * GROUP, axis=0)
    n_tiles = (qs + QB + KT - 1) // KT
    qp = [jnp.concatenate(_head_rows(q, kvh), axis=0).astype(BF16) for kvh in range(N_KV_HEADS)]

    o_cmp = []
    c_end = (_lane_iota((rows, LANES)) + 1) * CMP_BLOCK - 1
    c_end_t = (lax.broadcasted_iota(jnp.int32, (LANES, rows), 0) + 1) * CMP_BLOCK - 1
    for kvh in range(N_KV_HEADS):
        _, o = _masked_softmax_pv(_dot_nt(qp[kvh], kc_ref[...]), c_end <= rep(q_pos), vc_ref[...])
        o_cmp.append(o)
        s_t = _dot_nt(kc_ref[...], qp[kvh])
        ok_t = c_end_t <= jnp.concatenate([q_pos_t] * GROUP, axis=1)
        s_t = jnp.where(ok_t, s_t, NEG_INF)
        p_t = jnp.where(ok_t, jnp.exp(s_t - jnp.max(s_t, axis=0, keepdims=True)), 0.0)
        l_t = jnp.sum(p_t, axis=0, keepdims=True)
        p_t = p_t * jnp.where(l_t > 0.0, 1.0 / l_t, 0.0)
        sel_t = _select_blocks_t(p_t, q_pos_t, n_blk)
        sel = jnp.concatenate([sel_t, jnp.zeros((LANES - 2 * n_blk, QB), F32)], axis=0).T
        selexp = jnp.dot(sel.astype(BF16), exp_ref[...], preferred_element_type=F32)
        for t in range(n_kt):
            @pl.when(t < n_tiles)
            def _():
                ok = (selexp[:, t * KT:(t + 1) * KT] > 0.5) & (t * KT + _lane_iota((QB, KT)) <= q_pos)
                bias_ref[kvh, t] = jnp.where(ok, 0.0, NEG_INF)

    mrun_ref[...] = jnp.full(mrun_ref.shape, NEG_INF, F32)

    def max_pass(t, carry):
        k_t = ks_ref[pl.ds(pl.multiple_of(t * KT, KT), KT), :]
        for kvh in range(N_KV_HEADS):
            s = _dot_nt(qp[kvh], k_t)
            s = (s.reshape(GROUP, QB, KT) + bias_ref[kvh, t][None]).reshape(rows, KT)
            s_ref[kvh, t] = s
            mrun_ref[kvh] = jnp.maximum(mrun_ref[kvh], jnp.maximum(s[:, :LANES], s[:, LANES:]))
        return carry

    lax.fori_loop(0, n_tiles, max_pass, 0)
    for kvh in range(N_KV_HEADS):
        mb_ref[kvh] = jnp.broadcast_to(jnp.max(mrun_ref[kvh], axis=-1, keepdims=True), (rows, LANES))
    lrun_ref[...] = jnp.zeros(lrun_ref.shape, F32)
    acc_ref[...] = jnp.zeros(acc_ref.shape, F32)

    def sum_pass(t, carry):
        v_t = vs_ref[pl.ds(pl.multiple_of(t * KT, KT), KT), :]
        for kvh in range(N_KV_HEADS):
            s = s_ref[kvh, t]
            mb = mb_ref[kvh]
            p = jnp.exp(jnp.concatenate([s[:, :LANES] - mb, s[:, LANES:] - mb], axis=1))
            lrun_ref[kvh] += p[:, :LANES] + p[:, LANES:]
            acc_ref[kvh] += jnp.dot(p.astype(BF16), v_t, preferred_element_type=F32)
        return carry

    lax.fori_loop(0, n_tiles, sum_pass, 0)

    win_start = pl.multiple_of(jnp.maximum(qs - WINDOW, 0), QB)
    rel = q_pos - (win_start + _lane_iota((QB, WIN_KEYS)))
    win_ok = rep((rel >= 0) & (rel <= WINDOW))

    o_by_kvh = []
    for kvh in range(N_KV_HEADS):
        l = jnp.sum(lrun_ref[kvh], axis=-1, keepdims=True)
        o_sel = acc_ref[kvh] * jnp.where(l > 0.0, 1.0 / l, 0.0)
        s_w = _dot_nt(qp[kvh], kw_ref[pl.ds(win_start, WIN_KEYS), :])
        _, o_win = _masked_softmax_pv(s_w, win_ok, vw_ref[pl.ds(win_start, WIN_KEYS), :])
        o_by_kvh.append([tuple(o[g * QB:(g + 1) * QB] for o in (o_cmp[kvh], o_sel, o_win)) for g in range(GROUP)])

    out_ref[0] = _place_heads(o_by_kvh, gate_ref[0])


def _attn_prompt(q, kv, kvw, gate, wck, wcv):
    nb, seq, _ = q.shape
    per_q = lambda w: pl.BlockSpec((1, QB, w), lambda b, i: (b, i, 0))
    per_b = lambda w: pl.BlockSpec((1, seq, w), lambda b, i: (b, 0, 0))
    wspec = pl.BlockSpec((CMP_BLOCK, KV_W), lambda b, i: (0, 0))
    rows = GROUP * QB
    return pl.pallas_call(
        _attn_prompt_kernel,
        grid=(nb, seq // QB),
        in_specs=[per_q(N_HEADS * HEAD_DIM), per_b(N_PAGED_SLOTS * KV_W), per_b(2 * KV_W), per_q(LANES), wspec, wspec],
        out_specs=per_q(N_HEADS * HEAD_DIM),
        out_shape=jax.ShapeDtypeStruct(q.shape, F32),
        scratch_shapes=[pltpu.VMEM((LANES, KV_W), BF16), pltpu.VMEM((LANES, KV_W), BF16)]
        + [pltpu.VMEM((seq, KV_W), BF16)] * 4
        + [pltpu.VMEM((LANES, seq), BF16), pltpu.VMEM((N_KV_HEADS, seq // KT, QB, KT), F32),
           pltpu.VMEM((N_KV_HEADS, seq // KT, rows, KT), F32)]
        + [pltpu.VMEM((N_KV_HEADS, rows, LANES), F32)] * 4,
        compiler_params=_cparams("arbitrary", "arbitrary"),
        name="attn_prompt",
    )(q, kv, kvw, gate, wck, wcv)


def _attn_sample_kernel(pt_ref, q_ref, kvn_ref, kvwn_ref, gate_ref, wck_ref, wcv_ref, cache_ref, win_ref,
                        out_ref, win_out_ref, past_ref, exp_ref, sem):
    b = pl.program_id(0)
    n_pages = pt_ref.shape[1]
    page = cache_ref.shape[1]
    past_len = n_pages * page
    n_cmp = past_len // CMP_BLOCK
    n_blk = past_len // SEL_BLOCK + 1
    w_buf = win_ref.shape[1]

    def page_copy(p):
        return pltpu.make_async_copy(cache_ref.at[pt_ref[b, p]], past_ref.at[pl.ds(p * page, page)], sem.at[p])

    for p in range(n_pages):
        page_copy(p).start()

    @pl.when(b == 0)
    def _():
        exp_ref[...] = _block_expander(past_len)

    q = q_ref[0]
    q_pos = jnp.full((N_HEADS, 1), past_len, jnp.int32)
    qp_f = jnp.concatenate([r for kvh in range(N_KV_HEADS) for r in _head_rows(q, kvh)], axis=0)
    qp = qp_f.astype(BF16)
    kv_new = kvn_ref[0]
    kvw_new = kvwn_ref[0]

    def with_new_key(s_past, ok_past, v_past, k_new, v_new, ok_new):
        s_new = jnp.sum(qp_f * k_new, axis=-1, keepdims=True)
        s_past = jnp.where(ok_past, s_past, NEG_INF)
        s_new = jnp.where(ok_new, s_new, NEG_INF)
        m = jnp.maximum(jnp.max(s_past, axis=-1, keepdims=True), s_new)
        p_past = jnp.where(ok_past, jnp.exp(s_past - m), 0.0)
        p_new = jnp.where(ok_new, jnp.exp(s_new - m), 0.0)
        l = jnp.sum(p_past, axis=-1, keepdims=True) + p_new
        o = jnp.dot(p_past.astype(BF16), v_past, preferred_element_type=F32) + p_new * v_new
        return o * jnp.where(l > 0.0, 1.0 / l, 0.0)

    win = win_ref[0]
    rel = past_len - (past_len - w_buf + _lane_iota((N_HEADS, w_buf)))
    o_win = with_new_key(_dot_nt(qp, win[:, 0:KV_W].astype(BF16)), (rel >= 0) & (rel <= WINDOW),
                         win[:, KV_W:2 * KV_W].astype(BF16), kvw_new[:, 0:KV_W], kvw_new[:, KV_W:2 * KV_W],
                         jnp.full((N_HEADS, 1), True))
    row = lax.broadcasted_iota(jnp.int32, win.shape, 0)
    win_out_ref[0] = jnp.where(row == w_buf - 1, kvw_new, pltpu.roll(win, w_buf - 1, 0))

    for p in range(n_pages):
        page_copy(p).wait()

    kc = _pool_blocks(past_ref[:, 0:KV_W], wck_ref[...]).astype(BF16)
    vc = _pool_blocks(past_ref[:, KV_W:2 * KV_W], wcv_ref[...]).astype(BF16)
    kc = jnp.concatenate([kc, jnp.zeros((LANES - n_cmp, KV_W), BF16)], axis=0)
    vc = jnp.concatenate([vc, jnp.zeros((LANES - n_cmp, KV_W), BF16)], axis=0)
    c_end = (_lane_iota((N_HEADS, LANES)) + 1) * CMP_BLOCK - 1
    p_cmp, o_cmp = _masked_softmax_pv(_dot_nt(qp, kc), c_end <= q_pos, vc)
    psum = jnp.concatenate(
        [jnp.broadcast_to(jnp.sum(p_cmp[kvh * GROUP:(kvh + 1) * GROUP], axis=0, keepdims=True), (GROUP, LANES))
         for kvh in range(N_KV_HEADS)], axis=0)

    sel = _select_blocks(psum, q_pos, n_blk)
    selexp = jnp.dot(sel.astype(BF16), exp_ref[...], preferred_element_type=F32)
    new_blk = past_len // SEL_BLOCK
    ok_new = sel[:, 2 * new_blk:2 * new_blk + 1] > 0.5
    ok_past = (selexp > 0.5) & (_lane_iota((N_HEADS, past_len)) <= q_pos)
    o_sel = with_new_key(_dot_nt(qp, past_ref[:, 2 * KV_W:3 * KV_W].astype(BF16)), ok_past,
                         past_ref[:, 3 * KV_W:4 * KV_W].astype(BF16),
                         kv_new[:, 2 * KV_W:3 * KV_W], kv_new[:, 3 * KV_W:4 * KV_W], ok_new)

    o_by_kvh = [[tuple(o[kvh * GROUP + g:kvh * GROUP + g + 1] for o in (o_cmp, o_sel, o_win))
                 for g in range(GROUP)] for kvh in range(N_KV_HEADS)]
    out_ref[0] = _place_heads(o_by_kvh, gate_ref[0])


def _attn_sample(page_table, q, kv_new, kvw_new, gate, wck, wcv, cache, cache_win):
    nb, n_pages = page_table.shape
    page = cache.shape[1]
    w_buf = cache_win.shape[1]
    row3 = lambda w: pl.BlockSpec((1, 1, w), lambda b, pt: (b, 0, 0))
    wspec = pl.BlockSpec((CMP_BLOCK, KV_W), lambda b, pt: (0, 0))
    win_spec = pl.BlockSpec((1, w_buf, 2 * KV_W), lambda b, pt: (b, 0, 0))
    grid_spec = pltpu.PrefetchScalarGridSpec(
        num_scalar_prefetch=1,
        grid=(nb,),
        in_specs=[row3(N_HEADS * HEAD_DIM), row3(N_PAGED_SLOTS * KV_W), row3(2 * KV_W), row3(LANES), wspec, wspec,
                  pl.BlockSpec(memory_space=pl.ANY), win_spec],
        out_specs=[row3(N_HEADS * HEAD_DIM), win_spec],
        scratch_shapes=[pltpu.VMEM((n_pages * page, N_PAGED_SLOTS * KV_W), F32),
                        pltpu.VMEM((LANES, n_pages * page), BF16),
                        pltpu.SemaphoreType.DMA((n_pages,))],
    )
    r3 = lambda a: a.reshape(nb, 1, a.shape[-1])
    return pl.pallas_call(
        _attn_sample_kernel,
        grid_spec=grid_spec,
        out_shape=[jax.ShapeDtypeStruct((nb, 1, N_HEADS * HEAD_DIM), F32),
                   jax.ShapeDtypeStruct(cache_win.shape, F32)],
        compiler_params=_cparams("arbitrary"),
        name="attn_sample",
    )(page_table, r3(q), r3(kv_new), r3(kvw_new), r3(gate), wck, wcv, cache, cache_win)


FF_CHUNK = 1024


def _rms(x, g):
    return x * lax.rsqrt(jnp.mean(x * x, axis=-1, keepdims=True) + RMS_EPS) * g


def _mlp_kernel(x_ref, conv_ref, attn_ref, wout_ref, gm_ref, wup_ref, wdn_ref, gf_ref, y_ref, h_ref):
    h_ref[...] = (x_ref[...]
                  + jnp.dot(conv_ref[...].astype(BF16), wout_ref[0:C_CONV, :], preferred_element_type=F32)
                  + jnp.dot(attn_ref[...].astype(BF16), wout_ref[C_CONV:, :], preferred_element_type=F32))
    hn = _rms(h_ref[...], gm_ref[...]).astype(BF16)
    acc = None
    for c in range(D_FF // FF_CHUNK):
        m = jnp.dot(hn, wup_ref[:, c * FF_CHUNK:(c + 1) * FF_CHUNK], preferred_element_type=F32)
        m = jnp.maximum(m, 0.0)
        t = jnp.dot((m * m).astype(BF16), wdn_ref[c * FF_CHUNK:(c + 1) * FF_CHUNK, :], preferred_element_type=F32)
        acc = t if acc is None else acc + t
    y_ref[...] = _rms(h_ref[...] + acc, gf_ref[...])


def _mlp(x, conv_y, attn_y, w_out, g_mlp, w_up, w_down, g_final, tm):
    n = x.shape[0]
    row = lambda w: pl.BlockSpec((tm, w), lambda i: (i, 0))
    once = lambda shape: pl.BlockSpec(shape, lambda i: (0,) * len(shape), pipeline_mode=pl.Buffered(1))
    return pl.pallas_call(
        _mlp_kernel,
        grid=(n // tm,),
        in_specs=[row(D_MODEL), row(C_CONV), row(N_HEADS * HEAD_DIM), once((D_MODEL, D_MODEL)), once((1, D_MODEL)),
                  once((D_MODEL, D_FF)), once((D_FF, D_MODEL)), once((1, D_MODEL))],
        out_specs=row(D_MODEL),
        out_shape=jax.ShapeDtypeStruct((n, D_MODEL), F32),
        scratch_shapes=[pltpu.VMEM((tm, D_MODEL), F32)],
        compiler_params=_cparams("parallel"),
        name="mlp",
    )(x, conv_y, attn_y, w_out, g_mlp, w_up, w_down, g_final)


def kernel(x_prompt, x_sample, cache_kv, cache_win, state_conv, page_table, g_attn_norm, w_in, w_dw, b_dw,
           conv_ln_g, conv_ln_b, w_cmp_k, w_cmp_v, w_out, g_mlp_norm, w_up, w_down, g_final):
    depth = w_in.shape[0]
    assert depth == 1, "single-layer trunk"
    nb, seq, _ = x_prompt.shape
    db, dseq, _ = x_sample.shape
    assert dseq == 1
    n_phys, page = cache_kv.shape[1], cache_kv.shape[2]
    w_buf = cache_win.shape[2]

    w_pad = jnp.pad(w_in[0], ((0, 0), (0, IN_COLS_PAD - IN_COLS))).astype(BF16)
    wck = jnp.repeat(w_cmp_k[0], HEAD_DIM, axis=1)
    wcv = jnp.repeat(w_cmp_v[0], HEAD_DIM, axis=1)
    w_out_b, w_up_b, w_down_b = w_out[0].astype(BF16), w_up[0].astype(BF16), w_down[0].astype(BF16)
    row = lambda a: a.reshape(1, -1)
    g_attn, g_mlp, g_fin = row(g_attn_norm[0]), row(g_mlp_norm[0]), row(g_final)
    conv_w = (w_dw[0], row(b_dw[0]), row(conv_ln_g[0]), row(conv_ln_b[0]))
    mlp_w = (w_out_b, g_mlp, w_up_b, w_down_b, g_fin)

    xp = x_prompt.reshape(nb * seq, D_MODEL)
    glu, q, kv, kvw, gate = _project(xp, g_attn, w_pad, 512)
    per_b = lambda a: a.reshape(nb, seq, a.shape[-1])
    glu3, kvw3 = per_b(glu), per_b(kvw)
    conv_y = _conv_prompt(glu3, *conv_w)
    attn_y = _attn_prompt(per_b(q), per_b(kv), kvw3, per_b(gate), wck, wcv)
    y_prompt = _mlp(xp, conv_y.reshape(nb * seq, C_CONV), attn_y.reshape(nb * seq, -1), *mlp_w, 512)
    new_kv_p = kv.reshape(1, nb, seq, N_PAGED_SLOTS, N_KV_HEADS, HEAD_DIM)
    n_win = min(WINDOW, seq)
    new_win_p = kvw3[:, seq - n_win:].reshape(1, nb, n_win, 2, N_KV_HEADS, HEAD_DIM)
    new_conv_p = glu3[:, seq - (CONV_WIDTH - 1):][None]

    xs = x_sample.reshape(db, D_MODEL)
    glu_s, q_s, kv_s, kvw_s, gate_s = _project(xs, g_attn, w_pad, db)
    conv_y_s, conv_st = _conv_sample(state_conv[0].reshape(db, (CONV_WIDTH - 1) * C_CONV), glu_s, *conv_w)
    attn_s, new_win_s = _attn_sample(page_table, q_s, kv_s, kvw_s, gate_s, wck, wcv,
                                     cache_kv[0].reshape(n_phys, page, N_PAGED_SLOTS * KV_W),
                                     cache_win[0].reshape(db, w_buf, 2 * KV_W))
    y_sample = _mlp(xs, conv_y_s, attn_s.reshape(db, -1), *mlp_w, db)

    return (y_prompt.reshape(nb, seq, D_MODEL), y_sample.reshape(db, 1, D_MODEL), new_kv_p, new_win_p, new_conv_p,
            kv_s.reshape(1, db, 1, N_PAGED_SLOTS, N_KV_HEADS, HEAD_DIM),
            new_win_s.reshape(1, db, w_buf, 2, N_KV_HEADS, HEAD_DIM),
            conv_st.reshape(1, db, CONV_WIDTH - 1, C_CONV))
```

```python
import functools

import jax
import jax.numpy as jnp
from jax import lax
from jax.experimental import pallas as pl
from jax.experimental.pallas import tpu as pltpu

F32 = jnp.float32
BF16 = jnp.bfloat16

D_MODEL = 1024
C_CONV = 512
CONV_WIDTH = 31
N_HEADS = 8
HEAD_DIM = 64
N_KV_HEADS = 2
GROUP = N_HEADS // N_KV_HEADS
KV_W = N_KV_HEADS * HEAD_DIM
N_PAGED_SLOTS = 4
D_FF = 4 * D_MODEL
CMP_BLOCK = 32
SEL_BLOCK = 64
TOP_N = 16
N_LOCAL = 2
WINDOW = 512
FORCE_BONUS = float(GROUP + 1)
NEG_INF = -1e30
RMS_EPS = 1e-6
LN_EPS = 1e-5
ATTN_SCALE = HEAD_DIM ** -0.5

COL_Q = 2 * C_CONV
COL_KV = COL_Q + N_HEADS * HEAD_DIM
COL_WIN = COL_KV + N_PAGED_SLOTS * KV_W
COL_GATE = COL_WIN + 2 * KV_W
IN_COLS = COL_GATE + 3 * N_HEADS
LANES = 128
IN_COLS_PAD = -(-IN_COLS // LANES) * LANES

VMEM_LIMIT = 56 * 1024 * 1024


def _cparams(*sem):
    return pltpu.CompilerParams(dimension_semantics=sem, vmem_limit_bytes=VMEM_LIMIT)


def _const_spec(shape):
    return pl.BlockSpec(shape, lambda *_: (0,) * len(shape))


def _proj_kernel(x_ref, g_ref, w_ref, glu_ref, q_ref, kv_ref, kvw_ref, gate_ref):
    x = x_ref[...]
    ms = jnp.mean(x * x, axis=-1, keepdims=True)
    xn = (x * lax.rsqrt(ms + RMS_EPS) * g_ref[...]).astype(BF16)
    z = jnp.dot(xn, w_ref[...], preferred_element_type=F32)
    glu_ref[...] = z[:, :C_CONV] * jax.nn.sigmoid(z[:, C_CONV:COL_Q])
    q_ref[...] = z[:, COL_Q:COL_KV]
    kv_ref[...] = z[:, COL_KV:COL_WIN]
    kvw_ref[...] = z[:, COL_WIN:COL_GATE]
    gate_ref[...] = jax.nn.sigmoid(z[:, COL_GATE:IN_COLS_PAD])


def _project(x, g, w_pad, tm):
    n = x.shape[0]
    row = lambda w: pl.BlockSpec((tm, w), lambda i: (i, 0))
    widths = (C_CONV, N_HEADS * HEAD_DIM, N_PAGED_SLOTS * KV_W, 2 * KV_W, LANES)
    return pl.pallas_call(
        _proj_kernel,
        grid=(n // tm,),
        in_specs=[row(D_MODEL), _const_spec((1, D_MODEL)), _const_spec((D_MODEL, IN_COLS_PAD))],
        out_specs=[row(w) for w in widths],
        out_shape=[jax.ShapeDtypeStruct((n, w), F32) for w in widths],
        compiler_params=_cparams("parallel"),
        name="projection",
    )(x, g, w_pad)


def _ln_swish(y, lng, lnb):
    mu = jnp.mean(y, axis=-1, keepdims=True)
    d = y - mu
    var = jnp.mean(d * d, axis=-1, keepdims=True)
    y = d * lax.rsqrt(var + LN_EPS) * lng + lnb
    return y * jax.nn.sigmoid(y)


CONV_ROWS = 64
CONV_PAD = 32


def _conv_prompt_kernel(glu_ref, w_ref, b_ref, lng_ref, lnb_ref, y_ref, ext_ref):
    seq = glu_ref.shape[1]
    ext_ref[0:CONV_PAD, :] = jnp.zeros((CONV_PAD, C_CONV), F32)
    ext_ref[CONV_PAD:, :] = glu_ref[0]
    shift = CONV_PAD - (CONV_WIDTH - 1)

    def body(c, carry):
        base = pl.multiple_of(c * CONV_ROWS, CONV_ROWS)
        win = ext_ref[pl.ds(base, CONV_ROWS + CONV_PAD), :]
        acc = jnp.zeros((CONV_ROWS, C_CONV), F32)
        for j in range(CONV_WIDTH):
            acc = acc + w_ref[j:j + 1, :] * win[j + shift:j + shift + CONV_ROWS, :]
        y_ref[0, pl.ds(base, CONV_ROWS), :] = _ln_swish(acc + b_ref[...], lng_ref[...], lnb_ref[...])
        return carry

    lax.fori_loop(0, seq // CONV_ROWS, body, 0)


def _conv_prompt(glu, w_dw, b_dw, lng, lnb):
    nb, seq, _ = glu.shape
    blk = pl.BlockSpec((1, seq, C_CONV), lambda b: (b, 0, 0))
    return pl.pallas_call(
        _conv_prompt_kernel,
        grid=(nb,),
        in_specs=[blk, _const_spec((CONV_WIDTH, C_CONV))] + [_const_spec((1, C_CONV))] * 3,
        out_specs=blk,
        out_shape=jax.ShapeDtypeStruct(glu.shape, F32),
        scratch_shapes=[pltpu.VMEM((seq + CONV_PAD, C_CONV), F32)],
        compiler_params=_cparams("parallel"),
        name="conv_prompt",
    )(glu, w_dw, b_dw, lng, lnb)


def _conv_sample_kernel(st_ref, glu_ref, w_ref, b_ref, lng_ref, lnb_ref, y_ref, st_out_ref):
    n_prev = CONV_WIDTH - 1
    glu = glu_ref[...]
    acc = w_ref[n_prev:n_prev + 1, :] * glu
    for j in range(n_prev):
        acc = acc + w_ref[j:j + 1, :] * st_ref[j]
    y_ref[...] = _ln_swish(acc + b_ref[...], lng_ref[...], lnb_ref[...])
    st_out_ref[0:n_prev - 1] = st_ref[1:n_prev]
    st_out_ref[n_prev - 1] = glu


def _conv_sample(state_t, glu, w_dw, b_dw, lng, lnb):
    nb = glu.shape[0]
    return pl.pallas_call(
        _conv_sample_kernel,
        grid=(1,),
        in_specs=[_const_spec(state_t.shape), _const_spec((nb, C_CONV)), _const_spec((CONV_WIDTH, C_CONV))]
        + [_const_spec((1, C_CONV))] * 3,
        out_specs=[_const_spec((nb, C_CONV)), _const_spec(state_t.shape)],
        out_shape=[jax.ShapeDtypeStruct((nb, C_CONV), F32), jax.ShapeDtypeStruct(state_t.shape, F32)],
        compiler_params=_cparams("arbitrary"),
        name="conv_sample",
    )(state_t, glu, w_dw, b_dw, lng, lnb)


def _lane_iota(shape):
    return lax.broadcasted_iota(jnp.int32, shape, len(shape) - 1)


def _head_rows(q, kvh):
    rows = q.shape[0]
    half = _lane_iota((rows, KV_W)) // HEAD_DIM
    out = []
    for g in range(GROUP):
        h = kvh * GROUP + g
        slab = q[:, KV_W * (h // 2):KV_W * (h // 2 + 1)]
        if h % 2 != kvh:
            slab = pltpu.roll(slab, HEAD_DIM, 1)
        out.append(jnp.where(half == kvh, slab * ATTN_SCALE, 0.0))
    return out


def _dot_nt(a, b):
    return lax.dot_general(a, b, (((1,), (1,)), ((), ())), preferred_element_type=F32)


def _masked_softmax_pv(s, ok, v):
    s = jnp.where(ok, s, NEG_INF)
    m = jnp.max(s, axis=-1, keepdims=True)
    p = jnp.where(ok, jnp.exp(s - m), 0.0)
    l = jnp.sum(p, axis=-1, keepdims=True)
    inv = jnp.where(l > 0.0, 1.0 / l, 0.0)
    return p * inv, jnp.dot(p.astype(BF16), v, preferred_element_type=F32) * inv


def _pool_blocks(x, w):
    t = x.shape[0]
    return jnp.sum(x.reshape(t // CMP_BLOCK, CMP_BLOCK, KV_W) * w[None], axis=1)


def _select_blocks(psum, q_pos, n_blk):
    shape = psum.shape
    lane = _lane_iota(shape)
    even = (lane & 1) == 0
    pair = psum + jnp.where(even, pltpu.roll(psum, LANES - 1, 1), pltpu.roll(psum, 1, 1))
    blk = lane // (SEL_BLOCK // CMP_BLOCK)
    back = q_pos // SEL_BLOCK - blk
    forced = (blk == 0) | ((back >= 0) & (back < N_LOCAL))
    score = jnp.where(back >= 0, pair + jnp.where(forced, FORCE_BONUS, 0.0), -jnp.inf)
    rank = jnp.zeros(shape, jnp.int32)
    for i in range(n_blk):
        si = score[:, 2 * i:2 * i + 1]
        beats = (si > score) | ((si == score) & (i < blk))
        rank = rank + beats.astype(jnp.int32)
    return jnp.where((rank < min(TOP_N, n_blk)) & even, 1.0, 0.0)


def _block_expander(n_keys):
    li = lax.broadcasted_iota(jnp.int32, (LANES, n_keys), 0)
    ti = lax.broadcasted_iota(jnp.int32, (LANES, n_keys), 1)
    return jnp.where(((li & 1) == 0) & ((li >> 1) == ti // SEL_BLOCK), 1.0, 0.0).astype(BF16)


def _place_heads(o_by_kvh, gates):
    rows = gates.shape[0]
    low = _lane_iota((rows, KV_W)) < HEAD_DIM
    slabs = []
    for s in range(N_HEADS // 2):
        pair = []
        for h in (2 * s, 2 * s + 1):
            kvh, g = divmod(h, GROUP)
            o = sum(o_by_kvh[kvh][g][k] * gates[:, 3 * h + k:3 * h + k + 1] for k in range(3))
            if kvh != h % 2:
                o = pltpu.roll(o, HEAD_DIM, 1)
            pair.append(o)
        slabs.append(jnp.where(low, pair[0], pair[1]))
    return jnp.concatenate(slabs, axis=1)


QB = 128
KT = 256
WIN_KEYS = WINDOW + QB


def _select_blocks_t(p_t, q_pos, n_blk):
    nq = q_pos.shape[1]
    rows = 2 * n_blk
    psum = sum(p_t[:, g * nq:(g + 1) * nq] for g in range(GROUP))
    row = lax.broadcasted_iota(jnp.int32, psum.shape, 0)
    pair = psum + jnp.where((row & 1) == 0, pltpu.roll(psum, LANES - 1, 0), pltpu.roll(psum, 1, 0))
    pair = pair[0:rows]
    row = row[0:rows]
    even = (row & 1) == 0
    blk = row // (SEL_BLOCK // CMP_BLOCK)
    back = q_pos // SEL_BLOCK - blk
    forced = (blk == 0) | ((back >= 0) & (back < N_LOCAL))
    score = jnp.where(back >= 0, pair + jnp.where(forced, FORCE_BONUS, 0.0), -jnp.inf)
    rank = jnp.zeros(score.shape, jnp.int32)
    for i in range(n_blk):
        si = score[2 * i:2 * i + 1, :]
        beats = (si > score) | ((si == score) & (i < blk))
        rank = rank + beats.astype(jnp.int32)
    return jnp.where((rank < min(TOP_N, n_blk)) & even, 1.0, 0.0)


def _attn_prompt_kernel(q_ref, kv_ref, kvw_ref, gate_ref, wck_ref, wcv_ref, out_ref,
                        kc_ref, vc_ref, ks_ref, vs_ref, kw_ref, vw_ref, exp_ref,
                        bias_ref, s_ref, mrun_ref, mb_ref, lrun_ref, acc_ref):
    seq = kv_ref.shape[1]
    n_cmp = seq // CMP_BLOCK
    n_blk = seq // SEL_BLOCK
    n_kt = seq // KT
    rows = GROUP * QB
    qi = pl.program_id(1)
    qs = qi * QB

    @pl.when((pl.program_id(0) == 0) & (qi == 0))
    def _():
        exp_ref[...] = _block_expander(seq)

    @pl.when(qi == 0)
    def _():
        kc_ref[...] = jnp.zeros(kc_ref.shape, BF16)
        vc_ref[...] = jnp.zeros(vc_ref.shape, BF16)
        kc_ref[0:n_cmp, :] = _pool_blocks(kv_ref[0, :, 0:KV_W], wck_ref[...]).astype(BF16)
        vc_ref[0:n_cmp, :] = _pool_blocks(kv_ref[0, :, KV_W:2 * KV_W], wcv_ref[...]).astype(BF16)
        ks_ref[...] = kv_ref[0, :, 2 * KV_W:3 * KV_W].astype(BF16)
        vs_ref[...] = kv_ref[0, :, 3 * KV_W:4 * KV_W].astype(BF16)
        kw_ref[...] = kvw_ref[0, :, 0:KV_W].astype(BF16)
        vw_ref[...] = kvw_ref[0, :, KV_W:2 * KV_W].astype(BF16)

    q = q_ref[0]
    q_pos = qs + lax.broadcasted_iota(jnp.int32, (QB, 1), 0)
    q_pos_t = qs + lax.broadcasted_iota(jnp.int32, (1, QB), 1)
    rep = lambda a: jnp.concatenate([a] * GROUP, axis=0)
    n_tiles = (qs + QB + KT - 1) // KT
    qp = [jnp.concatenate(_head_rows(q, kvh), axis=0).astype(BF16) for kvh in range(N_KV_HEADS)]

    o_cmp = []
    c_end = (_lane_iota((rows, LANES)) + 1) * CMP_BLOCK - 1
    c_end_t = (lax.broadcasted_iota(jnp.int32, (LANES, rows), 0) + 1) * CMP_BLOCK - 1
    for kvh in range(N_KV_HEADS):
        _, o = _masked_softmax_pv(_dot_nt(qp[kvh], kc_ref[...]), c_end <= rep(q_pos), vc_ref[...])
        o_cmp.append(o)
        s_t = _dot_nt(kc_ref[...], qp[kvh])
        ok_t = c_end_t <= jnp.concatenate([q_pos_t] * GROUP, axis=1)
        s_t = jnp.where(ok_t, s_t, NEG_INF)
        p_t = jnp.where(ok_t, jnp.exp(s_t - jnp.max(s_t, axis=0, keepdims=True)), 0.0)
        l_t = jnp.sum(p_t, axis=0, keepdims=True)
        p_t = p_t * jnp.where(l_t > 0.0, 1.0 / l_t, 0.0)
        sel_t = _select_blocks_t(p_t, q_pos_t, n_blk)
        sel = jnp.concatenate([sel_t, jnp.zeros((LANES - 2 * n_blk, QB), F32)], axis=0).T
        selexp = jnp.dot(sel.astype(BF16), exp_ref[...], preferred_element_type=F32)
        for t in range(n_kt):
            @pl.when(t < n_tiles)
            def _():
                ok = (selexp[:, t * KT:(t + 1) * KT] > 0.5) & (t * KT + _lane_iota((QB, KT)) <= q_pos)
                bias_ref[kvh, t] = jnp.where(ok, 0.0, NEG_INF)

    mrun_ref[...] = jnp.full(mrun_ref.shape, NEG_INF, F32)

    def max_pass(t, carry):
        k_t = ks_ref[pl.ds(pl.multiple_of(t * KT, KT), KT), :]
        for kvh in range(N_KV_HEADS):
            s = _dot_nt(qp[kvh], k_t)
            s = (s.reshape(GROUP, QB, KT) + bias_ref[kvh, t][None]).reshape(rows, KT)
            s_ref[kvh, t] = s
            mrun_ref[kvh] = jnp.maximum(mrun_ref[kvh], jnp.maximum(s[:, :LANES], s[:, LANES:]))
        return carry

    lax.fori_loop(0, n_tiles, max_pass, 0)
    for kvh in range(N_KV_HEADS):
        mb_ref[kvh] = jnp.broadcast_to(jnp.max(mrun_ref[kvh], axis=-1, keepdims=True), (rows, LANES))
    lrun_ref[...] = jnp.zeros(lrun_ref.shape, F32)
    acc_ref[...] = jnp.zeros(acc_ref.shape, F32)

    def sum_pass(t, carry):
        v_t = vs_ref[pl.ds(pl.multiple_of(t * KT, KT), KT), :]
        for kvh in range(N_KV_HEADS):
            s = s_ref[kvh, t]
            mb = mb_ref[kvh]
            p = jnp.exp(jnp.concatenate([s[:, :LANES] - mb, s[:, LANES:] - mb], axis=1))
            lrun_ref[kvh] += p[:, :LANES] + p[:, LANES:]
            acc_ref[kvh] += jnp.dot(p.astype(BF16), v_t, preferred_element_type=F32)
        return carry

    lax.fori_loop(0, n_tiles, sum_pass, 0)

    win_start = pl.multiple_of(jnp.maximum(qs - WINDOW, 0), QB)
    rel = q_pos - (win_start + _lane_iota((QB, WIN_KEYS)))
    win_ok = rep((rel >= 0) & (rel <= WINDOW))

    o_by_kvh = []
    for kvh in range(N_KV_HEADS):
        l = jnp.sum(lrun_ref[kvh], axis=-1, keepdims=True)
        o_sel = acc_ref[kvh] * jnp.where(l > 0.0, 1.0 / l, 0.0)
        s_w = _dot_nt(qp[kvh], kw_ref[pl.ds(win_start, WIN_KEYS), :])
        _, o_win = _masked_softmax_pv(s_w, win_ok, vw_ref[pl.ds(win_start, WIN_KEYS), :])
        o_by_kvh.append([tuple(o[g * QB:(g + 1) * QB] for o in (o_cmp[kvh], o_sel, o_win)) for g in range(GROUP)])

    out_ref[0] = _place_heads(o_by_kvh, gate_ref[0])


def _attn_prompt(q, kv, kvw, gate, wck, wcv):
    nb, seq, _ = q.shape
    per_q = lambda w: pl.BlockSpec((1, QB, w), lambda b, i: (b, i, 0))
    per_b = lambda w: pl.BlockSpec((1, seq, w), lambda b, i: (b, 0, 0))
    wspec = pl.BlockSpec((CMP_BLOCK, KV_W), lambda b, i: (0, 0))
    rows = GROUP * QB
    return pl.pallas_call(
        _attn_prompt_kernel,
        grid=(nb, seq // QB),
        in_specs=[per_q(N_HEADS * HEAD_DIM), per_b(N_PAGED_SLOTS * KV_W), per_b(2 * KV_W), per_q(LANES), wspec, wspec],
        out_specs=per_q(N_HEADS * HEAD_DIM),
        out_shape=jax.ShapeDtypeStruct(q.shape, F32),
        scratch_shapes=[pltpu.VMEM((LANES, KV_W), BF16), pltpu.VMEM((LANES, KV_W), BF16)]
        + [pltpu.VMEM((seq, KV_W), BF16)] * 4
        + [pltpu.VMEM((LANES, seq), BF16), pltpu.VMEM((N_KV_HEADS, seq // KT, QB, KT), F32),
           pltpu.VMEM((N_KV_HEADS, seq // KT, rows, KT), F32)]
        + [pltpu.VMEM((N_KV_HEADS, rows, LANES), F32)] * 4,
        compiler_params=_cparams("arbitrary", "arbitrary"),
        name="attn_prompt",
    )(q, kv, kvw, gate, wck, wcv)


def _pool_matrix(w_cmp, n_keys):
    t = jnp.arange(n_keys)
    onehot = (t // CMP_BLOCK)[:, None] == jnp.arange(LANES // N_KV_HEADS)[None, :]
    per_head = [jnp.where(onehot, w_cmp[t % CMP_BLOCK, h][:, None], 0.0) for h in range(N_KV_HEADS)]
    return jnp.concatenate(per_head, axis=1).astype(BF16)


def _pool_blocks_t(x_t, w_pool):
    res = jnp.dot(x_t, w_pool, preferred_element_type=F32)
    row = lax.broadcasted_iota(jnp.int32, res.shape, 0)
    own = jnp.where(row < HEAD_DIM, res, pltpu.roll(res, HEAD_DIM, 1))
    return jnp.where(_lane_iota(res.shape) < w_pool.shape[0] // CMP_BLOCK, own, 0.0)


def _attn_sample_kernel(pt_ref, q_ref, kvn_ref, kvwn_ref, gate_ref, wk_ref, wv_ref, kvwt_ref, cache_ref, win_ref,
                        out_ref, win_out_ref, past_ref, exp_ref, sem):
    b = pl.program_id(0)
    nb = pl.num_programs(0)
    n_pages = pt_ref.shape[1]
    page = cache_ref.shape[2]
    past_len = n_pages * page
    n_blk = past_len // SEL_BLOCK + 1
    w_buf = win_ref.shape[2]
    slot = b % 2

    def page_copy(bb, sl, p):
        return pltpu.make_async_copy(cache_ref.at[pt_ref[bb, p]], past_ref.at[sl, :, pl.ds(p * page, page)],
                                     sem.at[sl, p])

    @pl.when(b == 0)
    def _():
        for p in range(n_pages):
            page_copy(0, 0, p).start()
        exp_ref[...] = _block_expander(past_len)

    @pl.when(b + 1 < nb)
    def _():
        for p in range(n_pages):
            page_copy(b + 1, 1 - slot, p).start()

    q = q_ref[0]
    q_pos = jnp.full((N_HEADS, 1), past_len, jnp.int32)
    qp_f = jnp.concatenate([r for kvh in range(N_KV_HEADS) for r in _head_rows(q, kvh)], axis=0)
    qp = qp_f.astype(BF16)
    kv_new = kvn_ref[0]
    kvw_new = kvwn_ref[0]

    def with_new_key(s_past, ok_past, vt_past, k_new, v_new, ok_new):
        s_new = jnp.sum(qp_f * k_new, axis=-1, keepdims=True)
        s_past = jnp.where(ok_past, s_past, NEG_INF)
        s_new = jnp.where(ok_new, s_new, NEG_INF)
        m = jnp.maximum(jnp.max(s_past, axis=-1, keepdims=True), s_new)
        p_past = jnp.where(ok_past, jnp.exp(s_past - m), 0.0)
        p_new = jnp.where(ok_new, jnp.exp(s_new - m), 0.0)
        l = jnp.sum(p_past, axis=-1, keepdims=True) + p_new
        o = _dot_nt(p_past.astype(BF16), vt_past) + p_new * v_new
        return o * jnp.where(l > 0.0, 1.0 / l, 0.0)

    win = win_ref[0]
    rel = past_len - (past_len - w_buf + _lane_iota((N_HEADS, w_buf)))
    o_win = with_new_key(jnp.dot(qp, win[0:KV_W].astype(BF16), preferred_element_type=F32),
                         (rel >= 0) & (rel <= WINDOW), win[KV_W:2 * KV_W].astype(BF16),
                         kvw_new[:, 0:KV_W], kvw_new[:, KV_W:2 * KV_W], jnp.full((N_HEADS, 1), True))
    new_col = jnp.sum(jnp.where(_lane_iota(kvwt_ref.shape) == b, kvwt_ref[...], 0.0), axis=1, keepdims=True)
    win_out_ref[0] = jnp.where(_lane_iota(win.shape) == w_buf - 1, new_col, pltpu.roll(win, w_buf - 1, 1))

    for p in range(n_pages):
        page_copy(b, slot, p).wait()

    def slab(i):
        return past_ref[slot, i * KV_W:(i + 1) * KV_W, :].astype(BF16)

    kc_t = _pool_blocks_t(slab(0), wk_ref[...]).astype(BF16)
    vc_t = _pool_blocks_t(slab(1), wv_ref[...]).astype(BF16)
    c_end = (_lane_iota((N_HEADS, LANES)) + 1) * CMP_BLOCK - 1
    s_c = jnp.where(c_end <= q_pos, jnp.dot(qp, kc_t, preferred_element_type=F32), NEG_INF)
    p_c = jnp.where(c_end <= q_pos, jnp.exp(s_c - jnp.max(s_c, axis=-1, keepdims=True)), 0.0)
    l_c = jnp.sum(p_c, axis=-1, keepdims=True)
    inv_c = jnp.where(l_c > 0.0, 1.0 / l_c, 0.0)
    o_cmp = _dot_nt(p_c.astype(BF16), vc_t) * inv_c
    p_cmp = p_c * inv_c
    psum = jnp.concatenate(
        [jnp.broadcast_to(jnp.sum(p_cmp[kvh * GROUP:(kvh + 1) * GROUP], axis=0, keepdims=True), (GROUP, LANES))
         for kvh in range(N_KV_HEADS)], axis=0)

    sel = _select_blocks(psum, q_pos, n_blk)
    selexp = jnp.dot(sel.astype(BF16), exp_ref[...], preferred_element_type=F32)
    new_blk = past_len // SEL_BLOCK
    ok_new = sel[:, 2 * new_blk:2 * new_blk + 1] > 0.5
    ok_past = (selexp > 0.5) & (_lane_iota((N_HEADS, past_len)) <= q_pos)
    o_sel = with_new_key(jnp.dot(qp, slab(2), preferred_element_type=F32), ok_past, slab(3),
                         kv_new[:, 2 * KV_W:3 * KV_W], kv_new[:, 3 * KV_W:4 * KV_W], ok_new)

    o_by_kvh = [[tuple(o[kvh * GROUP + g:kvh * GROUP + g + 1] for o in (o_cmp, o_sel, o_win))
                 for g in range(GROUP)] for kvh in range(N_KV_HEADS)]
    out_ref[0] = _place_heads(o_by_kvh, gate_ref[0])


def _attn_sample(page_table, q, kv_new, kvw_new, gate, w_pool_k, w_pool_v, cache_t, win_t):
    nb, n_pages = page_table.shape
    n_feat, page = cache_t.shape[1], cache_t.shape[2]
    w_buf = win_t.shape[2]
    past_len = n_pages * page
    row3 = lambda w: pl.BlockSpec((1, 1, w), lambda b, pt: (b, 0, 0))
    full = lambda shape: pl.BlockSpec(shape, lambda b, pt: (0,) * len(shape))
    win_spec = pl.BlockSpec((1, 2 * KV_W, w_buf), lambda b, pt: (b, 0, 0))
    grid_spec = pltpu.PrefetchScalarGridSpec(
        num_scalar_prefetch=1,
        grid=(nb,),
        in_specs=[row3(N_HEADS * HEAD_DIM), row3(N_PAGED_SLOTS * KV_W), row3(2 * KV_W), row3(LANES),
                  full((past_len, LANES)), full((past_len, LANES)), full((2 * KV_W, nb)),
                  pl.BlockSpec(memory_space=pl.ANY), win_spec],
        out_specs=[row3(N_HEADS * HEAD_DIM), win_spec],
        scratch_shapes=[pltpu.VMEM((2, n_feat, past_len), F32),
                        pltpu.VMEM((LANES, past_len), BF16),
                        pltpu.SemaphoreType.DMA((2, n_pages))],
    )
    r3 = lambda a: a.reshape(nb, 1, a.shape[-1])
    return pl.pallas_call(
        _attn_sample_kernel,
        grid_spec=grid_spec,
        out_shape=[jax.ShapeDtypeStruct((nb, 1, N_HEADS * HEAD_DIM), F32),
                   jax.ShapeDtypeStruct(win_t.shape, F32)],
        compiler_params=_cparams("arbitrary"),
        name="attn_sample",
    )(page_table, r3(q), r3(kv_new), r3(kvw_new), r3(gate), w_pool_k, w_pool_v, kvw_new.T, cache_t, win_t)


FF_CHUNK = 1024


def _rms(x, g):
    return x * lax.rsqrt(jnp.mean(x * x, axis=-1, keepdims=True) + RMS_EPS) * g


def _mlp_kernel(x_ref, conv_ref, attn_ref, wout_ref, gm_ref, wup_ref, wdn_ref, gf_ref, y_ref, h_ref):
    h_ref[...] = (x_ref[...]
                  + jnp.dot(conv_ref[...].astype(BF16), wout_ref[0:C_CONV, :], preferred_element_type=F32)
                  + jnp.dot(attn_ref[...].astype(BF16), wout_ref[C_CONV:, :], preferred_element_type=F32))
    hn = _rms(h_ref[...], gm_ref[...]).astype(BF16)
    acc = None
    for c in range(D_FF // FF_CHUNK):
        m = jnp.dot(hn, wup_ref[:, c * FF_CHUNK:(c + 1) * FF_CHUNK], preferred_element_type=F32)
        m = jnp.maximum(m, 0.0)
        t = jnp.dot((m * m).astype(BF16), wdn_ref[c * FF_CHUNK:(c + 1) * FF_CHUNK, :], preferred_element_type=F32)
        acc = t if acc is None else acc + t
    y_ref[...] = _rms(h_ref[...] + acc, gf_ref[...])


def _mlp(x, conv_y, attn_y, w_out, g_mlp, w_up, w_down, g_final, tm):
    n = x.shape[0]
    row = lambda w: pl.BlockSpec((tm, w), lambda i: (i, 0))
    once = lambda shape: pl.BlockSpec(shape, lambda i: (0,) * len(shape), pipeline_mode=pl.Buffered(1))
    return pl.pallas_call(
        _mlp_kernel,
        grid=(n // tm,),
        in_specs=[row(D_MODEL), row(C_CONV), row(N_HEADS * HEAD_DIM), once((D_MODEL, D_MODEL)), once((1, D_MODEL)),
                  once((D_MODEL, D_FF)), once((D_FF, D_MODEL)), once((1, D_MODEL))],
        out_specs=row(D_MODEL),
        out_shape=jax.ShapeDtypeStruct((n, D_MODEL), F32),
        scratch_shapes=[pltpu.VMEM((tm, D_MODEL), F32)],
        compiler_params=_cparams("parallel"),
        name="mlp",
    )(x, conv_y, attn_y, w_out, g_mlp, w_up, w_down, g_final)


def kernel(x_prompt, x_sample, cache_kv, cache_win, state_conv, page_table, g_attn_norm, w_in, w_dw, b_dw,
           conv_ln_g, conv_ln_b, w_cmp_k, w_cmp_v, w_out, g_mlp_norm, w_up, w_down, g_final):
    depth = w_in.shape[0]
    assert depth == 1, "single-layer trunk"
    nb, seq, _ = x_prompt.shape
    db, dseq, _ = x_sample.shape
    assert dseq == 1
    n_phys, page = cache_kv.shape[1], cache_kv.shape[2]
    w_buf = cache_win.shape[2]

    w_pad = jnp.pad(w_in[0], ((0, 0), (0, IN_COLS_PAD - IN_COLS))).astype(BF16)
    wck = jnp.repeat(w_cmp_k[0], HEAD_DIM, axis=1)
    wcv = jnp.repeat(w_cmp_v[0], HEAD_DIM, axis=1)
    w_out_b, w_up_b, w_down_b = w_out[0].astype(BF16), w_up[0].astype(BF16), w_down[0].astype(BF16)
    row = lambda a: a.reshape(1, -1)
    g_attn, g_mlp, g_fin = row(g_attn_norm[0]), row(g_mlp_norm[0]), row(g_final)
    conv_w = (w_dw[0], row(b_dw[0]), row(conv_ln_g[0]), row(conv_ln_b[0]))
    mlp_w = (w_out_b, g_mlp, w_up_b, w_down_b, g_fin)

    xp = x_prompt.reshape(nb * seq, D_MODEL)
    glu, q, kv, kvw, gate = _project(xp, g_attn, w_pad, 512)
    per_b = lambda a: a.reshape(nb, seq, a.shape[-1])
    glu3, kvw3 = per_b(glu), per_b(kvw)
    conv_y = _conv_prompt(glu3, *conv_w)
    attn_y = _attn_prompt(per_b(q), per_b(kv), kvw3, per_b(gate), wck, wcv)
    y_prompt = _mlp(xp, conv_y.reshape(nb * seq, C_CONV), attn_y.reshape(nb * seq, -1), *mlp_w, 512)
    new_kv_p = kv.reshape(1, nb, seq, N_PAGED_SLOTS, N_KV_HEADS, HEAD_DIM)
    n_win = min(WINDOW, seq)
    new_win_p = kvw3[:, seq - n_win:].reshape(1, nb, n_win, 2, N_KV_HEADS, HEAD_DIM)
    new_conv_p = glu3[:, seq - (CONV_WIDTH - 1):][None]

    xs = x_sample.reshape(db, D_MODEL)
    glu_s, q_s, kv_s, kvw_s, gate_s = _project(xs, g_attn, w_pad, db)
    conv_y_s, conv_st = _conv_sample(jnp.transpose(state_conv[0], (1, 0, 2)), glu_s, *conv_w)
    cache_t = jnp.transpose(cache_kv[0], (0, 2, 3, 4, 1)).reshape(n_phys, N_PAGED_SLOTS * KV_W, page)
    win_t = jnp.transpose(cache_win[0], (0, 2, 3, 4, 1)).reshape(db, 2 * KV_W, w_buf)
    n_past = page_table.shape[1] * page
    attn_s, new_win_t = _attn_sample(page_table, q_s, kv_s, kvw_s, gate_s, _pool_matrix(w_cmp_k[0], n_past),
                                     _pool_matrix(w_cmp_v[0], n_past), cache_t, win_t)
    new_win_s = jnp.transpose(new_win_t.reshape(db, 2, N_KV_HEADS, HEAD_DIM, w_buf), (0, 4, 1, 2, 3))[None]
    y_sample = _mlp(xs, conv_y_s, attn_s.reshape(db, -1), *mlp_w, db)

    return (y_prompt.reshape(nb, seq, D_MODEL), y_sample.reshape(db, 1, D_MODEL), new_kv_p, new_win_p, new_conv_p,
            kv_s.reshape(1, db, 1, N_PAGED_SLOTS, N_KV_HEADS, HEAD_DIM), new_win_s,
            jnp.transpose(conv_st, (1, 0, 2))[None])
```

```python
import functools

import jax
import jax.numpy as jnp
from jax import lax
from jax.experimental import pallas as pl
from jax.experimental.pallas import tpu as pltpu

F32 = jnp.float32
BF16 = jnp.bfloat16

D_MODEL = 1024
C_CONV = 512
CONV_WIDTH = 31
N_HEADS = 8
HEAD_DIM = 64
N_KV_HEADS = 2
GROUP = N_HEADS // N_KV_HEADS
KV_W = N_KV_HEADS * HEAD_DIM
N_PAGED_SLOTS = 4
D_FF = 4 * D_MODEL
CMP_BLOCK = 32
SEL_BLOCK = 64
TOP_N = 16
N_LOCAL = 2
WINDOW = 512
FORCE_BONUS = float(GROUP + 1)
NEG_INF = -1e30
RMS_EPS = 1e-6
LN_EPS = 1e-5
ATTN_SCALE = HEAD_DIM ** -0.5
LOG2E = 1.4426950408889634

COL_Q = 2 * C_CONV
COL_KV = COL_Q + N_HEADS * HEAD_DIM
COL_WIN = COL_KV + N_PAGED_SLOTS * KV_W
COL_GATE = COL_WIN + 2 * KV_W
IN_COLS = COL_GATE + 3 * N_HEADS
LANES = 128
IN_COLS_PAD = -(-IN_COLS // LANES) * LANES

VMEM_LIMIT = 56 * 1024 * 1024


def _cparams(*sem):
    return pltpu.CompilerParams(dimension_semantics=sem, vmem_limit_bytes=VMEM_LIMIT)


def _const_spec(shape):
    return pl.BlockSpec(shape, lambda *_: (0,) * len(shape))


def _dot_nt(a, b):
    return lax.dot_general(a, b, (((1,), (1,)), ((), ())), preferred_element_type=F32)


N_KV_COLS = COL_GATE - COL_KV


def _proj_kernel(x_ref, g_ref, w_ref, wkv_ref, glu_ref, q_ref, gate_ref, kv_ref, kvw_ref, *, kv_t):
    x = x_ref[...]
    ms = jnp.mean(x * x, axis=-1, keepdims=True)
    xn = (x * lax.rsqrt(ms + RMS_EPS) * g_ref[...]).astype(BF16)
    z = jnp.dot(xn, w_ref[...], preferred_element_type=F32)
    glu_ref[...] = z[:, :C_CONV] * jax.nn.sigmoid(z[:, C_CONV:COL_Q])
    q_ref[...] = z[:, COL_Q:COL_KV]
    gate_ref[...] = jax.nn.sigmoid(z[:, COL_KV:])
    n_paged = N_PAGED_SLOTS * KV_W
    if kv_t:
        kv = _dot_nt(wkv_ref[...], xn)
        kv_ref[0] = kv[:n_paged]
        kvw_ref[0] = kv[n_paged:]
    else:
        kv = _dot_nt(xn, wkv_ref[...])
        kv_ref[...] = kv[:, :n_paged]
        kvw_ref[...] = kv[:, n_paged:]


def _project(x, g, w_main, w_kv, tm, seq=None):
    n = x.shape[0]
    row = lambda w: pl.BlockSpec((tm, w), lambda i: (i, 0))
    n_paged = N_PAGED_SLOTS * KV_W
    widths = (C_CONV, N_HEADS * HEAD_DIM, LANES)
    out_specs = [row(w) for w in widths]
    out_shape = [jax.ShapeDtypeStruct((n, w), F32) for w in widths]
    for w in (n_paged, 2 * KV_W):
        if seq is None:
            out_specs.append(row(w))
            out_shape.append(jax.ShapeDtypeStruct((n, w), F32))
        else:
            per_seq = seq // tm
            out_specs.append(pl.BlockSpec((1, w, tm), lambda i: (i // per_seq, 0, i % per_seq)))
            out_shape.append(jax.ShapeDtypeStruct((n // seq, w, seq), F32))
    return pl.pallas_call(
        functools.partial(_proj_kernel, kv_t=seq is not None),
        grid=(n // tm,),
        in_specs=[row(D_MODEL), _const_spec((1, D_MODEL)), _const_spec(w_main.shape), _const_spec(w_kv.shape)],
        out_specs=out_specs,
        out_shape=out_shape,
        compiler_params=_cparams("parallel"),
        name="projection",
    )(x, g, w_main, w_kv)


def _ln_swish(y, lng, lnb):
    mu = jnp.mean(y, axis=-1, keepdims=True)
    d = y - mu
    var = jnp.mean(d * d, axis=-1, keepdims=True)
    y = d * lax.rsqrt(var + LN_EPS) * lng + lnb
    return y * jax.nn.sigmoid(y)


CONV_ROWS = 64
CONV_PAD = 32


def _conv_prompt_kernel(glu_ref, w_ref, b_ref, lng_ref, lnb_ref, y_ref, ext_ref):
    seq = glu_ref.shape[1]
    ext_ref[0:CONV_PAD, :] = jnp.zeros((CONV_PAD, C_CONV), F32)
    ext_ref[CONV_PAD:, :] = glu_ref[0]
    shift = CONV_PAD - (CONV_WIDTH - 1)

    def body(c, carry):
        base = pl.multiple_of(c * CONV_ROWS, CONV_ROWS)
        win = ext_ref[pl.ds(base, CONV_ROWS + CONV_PAD), :]
        acc = jnp.zeros((CONV_ROWS, C_CONV), F32)
        for j in range(CONV_WIDTH):
            acc = acc + w_ref[j:j + 1, :] * win[j + shift:j + shift + CONV_ROWS, :]
        y_ref[0, pl.ds(base, CONV_ROWS), :] = _ln_swish(acc + b_ref[...], lng_ref[...], lnb_ref[...])
        return carry

    lax.fori_loop(0, seq // CONV_ROWS, body, 0)


def _conv_prompt(glu, w_dw, b_dw, lng, lnb):
    nb, seq, _ = glu.shape
    blk = pl.BlockSpec((1, seq, C_CONV), lambda b: (b, 0, 0))
    return pl.pallas_call(
        _conv_prompt_kernel,
        grid=(nb,),
        in_specs=[blk, _const_spec((CONV_WIDTH, C_CONV))] + [_const_spec((1, C_CONV))] * 3,
        out_specs=blk,
        out_shape=jax.ShapeDtypeStruct(glu.shape, F32),
        scratch_shapes=[pltpu.VMEM((seq + CONV_PAD, C_CONV), F32)],
        compiler_params=_cparams("parallel"),
        name="conv_prompt",
    )(glu, w_dw, b_dw, lng, lnb)


def _conv_sample_kernel(st_ref, glu_ref, w_ref, b_ref, lng_ref, lnb_ref, y_ref, st_out_ref):
    n_prev = CONV_WIDTH - 1
    glu = glu_ref[...]
    acc = w_ref[n_prev:n_prev + 1, :] * glu
    for j in range(n_prev):
        acc = acc + w_ref[j:j + 1, :] * st_ref[j]
    y_ref[...] = _ln_swish(acc + b_ref[...], lng_ref[...], lnb_ref[...])
    st_out_ref[0:n_prev - 1] = st_ref[1:n_prev]
    st_out_ref[n_prev - 1] = glu


def _conv_sample(state_t, glu, w_dw, b_dw, lng, lnb):
    nb = glu.shape[0]
    return pl.pallas_call(
        _conv_sample_kernel,
        grid=(1,),
        in_specs=[_const_spec(state_t.shape), _const_spec((nb, C_CONV)), _const_spec((CONV_WIDTH, C_CONV))]
        + [_const_spec((1, C_CONV))] * 3,
        out_specs=[_const_spec((nb, C_CONV)), _const_spec(state_t.shape)],
        out_shape=[jax.ShapeDtypeStruct((nb, C_CONV), F32), jax.ShapeDtypeStruct(state_t.shape, F32)],
        compiler_params=_cparams("arbitrary"),
        name="conv_sample",
    )(state_t, glu, w_dw, b_dw, lng, lnb)


def _lane_iota(shape):
    return lax.broadcasted_iota(jnp.int32, shape, len(shape) - 1)


def _head_rows(q, kvh):
    rows = q.shape[0]
    half = _lane_iota((rows, KV_W)) // HEAD_DIM
    out = []
    for g in range(GROUP):
        h = kvh * GROUP + g
        slab = q[:, KV_W * (h // 2):KV_W * (h // 2 + 1)]
        if h % 2 != kvh:
            slab = pltpu.roll(slab, HEAD_DIM, 1)
        out.append(jnp.where(half == kvh, slab * (ATTN_SCALE * LOG2E), 0.0))
    return out


def _masked_softmax_pv(s, ok, v):
    s = jnp.where(ok, s, NEG_INF)
    m = jnp.max(s, axis=-1, keepdims=True)
    p = jnp.where(ok, jnp.exp2(s - m), 0.0)
    l = jnp.sum(p, axis=-1, keepdims=True)
    inv = jnp.where(l > 0.0, 1.0 / l, 0.0)
    return p * inv, jnp.dot(p.astype(BF16), v, preferred_element_type=F32) * inv


def _select_blocks(psum, q_pos, n_blk):
    shape = psum.shape
    lane = _lane_iota(shape)
    even = (lane & 1) == 0
    pair = psum + jnp.where(even, pltpu.roll(psum, LANES - 1, 1), pltpu.roll(psum, 1, 1))
    blk = lane // (SEL_BLOCK // CMP_BLOCK)
    back = q_pos // SEL_BLOCK - blk
    forced = (blk == 0) | ((back >= 0) & (back < N_LOCAL))
    score = jnp.where(back >= 0, pair + jnp.where(forced, FORCE_BONUS, 0.0), -jnp.inf)
    rank = jnp.zeros(shape, jnp.int32)
    for i in range(n_blk):
        si = score[:, 2 * i:2 * i + 1]
        beats = (si > score) | ((si == score) & (i < blk))
        rank = rank + beats.astype(jnp.int32)
    return jnp.where((rank < min(TOP_N, n_blk)) & even, 1.0, 0.0)


def _block_expander(n_keys):
    li = lax.broadcasted_iota(jnp.int32, (LANES, n_keys), 0)
    ti = lax.broadcasted_iota(jnp.int32, (LANES, n_keys), 1)
    return jnp.where(((li & 1) == 0) & ((li >> 1) == ti // SEL_BLOCK), 1.0, 0.0).astype(BF16)


def _place_heads(o_by_kvh, gates):
    rows = gates.shape[0]
    low = _lane_iota((rows, KV_W)) < HEAD_DIM
    slabs = []
    for s in range(N_HEADS // 2):
        pair = []
        for h in (2 * s, 2 * s + 1):
            kvh, g = divmod(h, GROUP)
            o = sum(o_by_kvh[kvh][g][k] * gates[:, 3 * h + k:3 * h + k + 1] for k in range(3))
            if kvh != h % 2:
                o = pltpu.roll(o, HEAD_DIM, 1)
            pair.append(o)
        slabs.append(jnp.where(low, pair[0], pair[1]))
    return jnp.concatenate(slabs, axis=1)


QB = 128
KT = 256
WIN_KEYS = WINDOW + QB


def _select_blocks_t(p_t, q_pos, n_blk):
    nq = q_pos.shape[1]
    rows = 2 * n_blk
    psum = sum(p_t[:, g * nq:(g + 1) * nq] for g in range(GROUP))
    row = lax.broadcasted_iota(jnp.int32, psum.shape, 0)
    pair = psum + jnp.where((row & 1) == 0, pltpu.roll(psum, LANES - 1, 0), pltpu.roll(psum, 1, 0))
    pair = pair[0:rows]
    row = row[0:rows]
    even = (row & 1) == 0
    blk = row // (SEL_BLOCK // CMP_BLOCK)
    back = q_pos // SEL_BLOCK - blk
    forced = (blk == 0) | ((back >= 0) & (back < N_LOCAL))
    score = jnp.where(back >= 0, pair + jnp.where(forced, FORCE_BONUS, 0.0), -jnp.inf)
    rank = jnp.zeros(score.shape, jnp.int32)
    for i in range(n_blk):
        si = score[2 * i:2 * i + 1, :]
        beats = (si > score) | ((si == score) & (i < blk))
        rank = rank + beats.astype(jnp.int32)
    return jnp.where((rank < min(TOP_N, n_blk)) & even, 1.0, 0.0)


WIN_CHUNK = 128


def _attn_prompt_kernel(q_ref, kv_ref, kvw_ref, gate_ref, wk_ref, wv_ref, out_ref,
                        kc_ref, vc_ref, ks_ref, vs_ref, kw_ref, vw_ref, exp_ref,
                        bias_ref, s_ref, mrun_ref, mb_ref, lrun_ref, acc_ref):
    seq = kv_ref.shape[2]
    n_blk = seq // SEL_BLOCK
    n_kt = seq // KT
    n_wc = WIN_KEYS // WIN_CHUNK
    rows = GROUP * QB
    qi = pl.program_id(1)
    qs = qi * QB

    @pl.when((pl.program_id(0) == 0) & (qi == 0))
    def _():
        exp_ref[...] = _block_expander(seq)

    @pl.when(qi == 0)
    def _():
        kc_ref[...] = _pool_blocks_t(kv_ref[0, 0:KV_W, :].astype(BF16), wk_ref[...]).T.astype(BF16)
        vc_ref[...] = _pool_blocks_t(kv_ref[0, KV_W:2 * KV_W, :].astype(BF16), wv_ref[...]).T.astype(BF16)
        for t in range(n_kt):
            ks_ref[t] = kv_ref[0, 2 * KV_W:3 * KV_W, t * KT:(t + 1) * KT].astype(BF16)
            vs_ref[t] = kv_ref[0, 3 * KV_W:4 * KV_W, t * KT:(t + 1) * KT].astype(BF16)
        for c in range(seq // WIN_CHUNK):
            kw_ref[c] = kvw_ref[0, 0:KV_W, c * WIN_CHUNK:(c + 1) * WIN_CHUNK].astype(BF16)
            vw_ref[c] = kvw_ref[0, KV_W:2 * KV_W, c * WIN_CHUNK:(c + 1) * WIN_CHUNK].astype(BF16)

    q = q_ref[0]
    q_pos = qs + lax.broadcasted_iota(jnp.int32, (QB, 1), 0)
    q_pos_t = qs + lax.broadcasted_iota(jnp.int32, (1, QB), 1)
    rep = lambda a: jnp.concatenate([a] * GROUP, axis=0)
    add_bias = lambda s, b: (s.reshape(GROUP, QB, s.shape[-1]) + b[None]).reshape(s.shape)
    n_tiles = (qs + QB + KT - 1) // KT
    qp = [jnp.concatenate(_head_rows(q, kvh), axis=0).astype(BF16) for kvh in range(N_KV_HEADS)]

    o_cmp = []
    c_end = (_lane_iota((rows, LANES)) + 1) * CMP_BLOCK - 1
    c_end_t = (lax.broadcasted_iota(jnp.int32, (LANES, rows), 0) + 1) * CMP_BLOCK - 1
    for kvh in range(N_KV_HEADS):
        _, o = _masked_softmax_pv(_dot_nt(qp[kvh], kc_ref[...]), c_end <= rep(q_pos), vc_ref[...])
        o_cmp.append(o)
        s_t = _dot_nt(kc_ref[...], qp[kvh])
        ok_t = c_end_t <= jnp.concatenate([q_pos_t] * GROUP, axis=1)
        s_t = jnp.where(ok_t, s_t, NEG_INF)
        p_t = jnp.where(ok_t, jnp.exp2(s_t - jnp.max(s_t, axis=0, keepdims=True)), 0.0)
        l_t = jnp.sum(p_t, axis=0, keepdims=True)
        p_t = p_t * jnp.where(l_t > 0.0, 1.0 / l_t, 0.0)
        sel_t = _select_blocks_t(p_t, q_pos_t, n_blk)
        sel = jnp.concatenate([sel_t, jnp.zeros((LANES - 2 * n_blk, QB), F32)], axis=0).T
        selexp = jnp.dot(sel.astype(BF16), exp_ref[...], preferred_element_type=F32)
        for t in range(n_kt):
            @pl.when(t < n_tiles)
            def _():
                ok = (selexp[:, t * KT:(t + 1) * KT] > 0.5) & (t * KT + _lane_iota((QB, KT)) <= q_pos)
                bias_ref[kvh, t] = jnp.where(ok, 0.0, NEG_INF)

    mrun_ref[...] = jnp.full(mrun_ref.shape, NEG_INF, F32)

    def max_pass(t, carry):
        k_t = ks_ref[t]
        for kvh in range(N_KV_HEADS):
            s = add_bias(jnp.dot(qp[kvh], k_t, preferred_element_type=F32), bias_ref[kvh, t])
            s_ref[kvh, t] = s
            mrun_ref[kvh] = jnp.maximum(mrun_ref[kvh], jnp.maximum(s[:, :LANES], s[:, LANES:]))
        return carry

    lax.fori_loop(0, n_tiles, max_pass, 0)
    for kvh in range(N_KV_HEADS):
        mb_ref[kvh] = jnp.broadcast_to(jnp.max(mrun_ref[kvh], axis=-1, keepdims=True), (rows, LANES))
    lrun_ref[...] = jnp.zeros(lrun_ref.shape, F32)
    acc_ref[...] = jnp.zeros(acc_ref.shape, F32)

    def sum_pass(t, carry):
        v_t = vs_ref[t]
        for kvh in range(N_KV_HEADS):
            s = s_ref[kvh, t]
            mb = mb_ref[kvh]
            p = jnp.exp2(jnp.concatenate([s[:, :LANES] - mb, s[:, LANES:] - mb], axis=1))
            lrun_ref[kvh] += p[:, :LANES] + p[:, LANES:]
            acc_ref[kvh] += _dot_nt(p.astype(BF16), v_t)
        return carry

    lax.fori_loop(0, n_tiles, sum_pass, 0)

    win_c0 = jnp.maximum(qs - WINDOW, 0) // WIN_CHUNK
    rel = q_pos - (win_c0 * WIN_CHUNK + _lane_iota((QB, WIN_KEYS)))
    win_bias = jnp.where((rel >= 0) & (rel <= WINDOW), 0.0, NEG_INF)

    o_by_kvh = []
    for kvh in range(N_KV_HEADS):
        l = jnp.sum(lrun_ref[kvh], axis=-1, keepdims=True)
        o_sel = acc_ref[kvh] * jnp.where(l > 0.0, 1.0 / l, 0.0)
        s_w = jnp.concatenate([jnp.dot(qp[kvh], kw_ref[win_c0 + c], preferred_element_type=F32) for c in range(n_wc)],
                              axis=1)
        s_w = add_bias(s_w, win_bias)
        p_w = jnp.exp2(s_w - jnp.max(s_w, axis=-1, keepdims=True))
        o_win = sum(_dot_nt(p_w[:, c * WIN_CHUNK:(c + 1) * WIN_CHUNK].astype(BF16), vw_ref[win_c0 + c])
                    for c in range(n_wc)) * (1.0 / jnp.sum(p_w, axis=-1, keepdims=True))
        o_by_kvh.append([tuple(o[g * QB:(g + 1) * QB] for o in (o_cmp[kvh], o_sel, o_win)) for g in range(GROUP)])

    out_ref[0] = _place_heads(o_by_kvh, gate_ref[0])


def _attn_prompt(q, kv_t, kvw_t, gate, w_pool_k, w_pool_v):
    nb, seq, _ = q.shape
    per_q = lambda w: pl.BlockSpec((1, QB, w), lambda b, i: (b, i, 0))
    per_b = lambda w: pl.BlockSpec((1, w, seq), lambda b, i: (b, 0, 0))
    wspec = pl.BlockSpec((seq, LANES), lambda b, i: (0, 0))
    rows = GROUP * QB
    return pl.pallas_call(
        _attn_prompt_kernel,
        grid=(nb, seq // QB),
        in_specs=[per_q(N_HEADS * HEAD_DIM), per_b(N_PAGED_SLOTS * KV_W), per_b(2 * KV_W), per_q(LANES), wspec, wspec],
        out_specs=per_q(N_HEADS * HEAD_DIM),
        out_shape=jax.ShapeDtypeStruct(q.shape, F32),
        scratch_shapes=[pltpu.VMEM((LANES, KV_W), BF16), pltpu.VMEM((LANES, KV_W), BF16)]
        + [pltpu.VMEM((seq // KT, KV_W, KT), BF16)] * 2
        + [pltpu.VMEM((seq // WIN_CHUNK, KV_W, WIN_CHUNK), BF16)] * 2
        + [pltpu.VMEM((LANES, seq), BF16), pltpu.VMEM((N_KV_HEADS, seq // KT, QB, KT), F32),
           pltpu.VMEM((N_KV_HEADS, seq // KT, rows, KT), F32)]
        + [pltpu.VMEM((N_KV_HEADS, rows, LANES), F32)] * 4,
        compiler_params=_cparams("arbitrary", "arbitrary"),
        name="attn_prompt",
    )(q, kv_t, kvw_t, gate, w_pool_k, w_pool_v)


def _pool_matrix(w_cmp, n_keys):
    eye = jnp.eye(n_keys // CMP_BLOCK, LANES // N_KV_HEADS, dtype=F32)
    per_head = [(eye[:, None, :] * w_cmp[None, :, h, None]).reshape(n_keys, -1) for h in range(N_KV_HEADS)]
    return jnp.concatenate(per_head, axis=1).astype(BF16)


def _pool_blocks_t(x_t, w_pool):
    res = jnp.dot(x_t, w_pool, preferred_element_type=F32)
    row = lax.broadcasted_iota(jnp.int32, res.shape, 0)
    own = jnp.where(row < HEAD_DIM, res, pltpu.roll(res, HEAD_DIM, 1))
    return jnp.where(_lane_iota(res.shape) < w_pool.shape[0] // CMP_BLOCK, own, 0.0)


def _attn_sample_kernel(pt_ref, q_ref, kvn_ref, kvwn_ref, gate_ref, wk_ref, wv_ref, kvwt_ref, cache_ref, win_ref,
                        out_ref, win_out_ref, past_ref, exp_ref, sem):
    b = pl.program_id(0)
    nb = pl.num_programs(0)
    n_pages = pt_ref.shape[1]
    page = cache_ref.shape[2]
    past_len = n_pages * page
    n_blk = past_len // SEL_BLOCK + 1
    w_buf = win_ref.shape[2]
    slot = b % 2

    def page_copy(bb, sl, p):
        return pltpu.make_async_copy(cache_ref.at[pt_ref[bb, p]], past_ref.at[sl, :, pl.ds(p * page, page)],
                                     sem.at[sl, p])

    @pl.when(b == 0)
    def _():
        for p in range(n_pages):
            page_copy(0, 0, p).start()
        exp_ref[...] = _block_expander(past_len)

    @pl.when(b + 1 < nb)
    def _():
        for p in range(n_pages):
            page_copy(b + 1, 1 - slot, p).start()

    q = q_ref[0]
    q_pos = jnp.full((N_HEADS, 1), past_len, jnp.int32)
    qp_f = jnp.concatenate([r for kvh in range(N_KV_HEADS) for r in _head_rows(q, kvh)], axis=0)
    qp = qp_f.astype(BF16)
    kv_new = kvn_ref[0]
    kvw_new = kvwn_ref[0]

    def with_new_key(s_past, ok_past, vt_past, k_new, v_new, ok_new):
        s_new = jnp.sum(qp_f * k_new, axis=-1, keepdims=True)
        s_past = jnp.where(ok_past, s_past, NEG_INF)
        s_new = jnp.where(ok_new, s_new, NEG_INF)
        m = jnp.maximum(jnp.max(s_past, axis=-1, keepdims=True), s_new)
        p_past = jnp.where(ok_past, jnp.exp2(s_past - m), 0.0)
        p_new = jnp.where(ok_new, jnp.exp2(s_new - m), 0.0)
        l = jnp.sum(p_past, axis=-1, keepdims=True) + p_new
        o = _dot_nt(p_past.astype(BF16), vt_past) + p_new * v_new
        return o * jnp.where(l > 0.0, 1.0 / l, 0.0)

    win = win_ref[0]
    rel = past_len - (past_len - w_buf + _lane_iota((N_HEADS, w_buf)))
    o_win = with_new_key(jnp.dot(qp, win[0:KV_W].astype(BF16), preferred_element_type=F32),
                         (rel >= 0) & (rel <= WINDOW), win[KV_W:2 * KV_W].astype(BF16),
                         kvw_new[:, 0:KV_W], kvw_new[:, KV_W:2 * KV_W], jnp.full((N_HEADS, 1), True))
    new_col = jnp.sum(jnp.where(_lane_iota(kvwt_ref.shape) == b, kvwt_ref[...], 0.0), axis=1, keepdims=True)
    win_out_ref[0] = jnp.where(_lane_iota(win.shape) == w_buf - 1, new_col, pltpu.roll(win, w_buf - 1, 1))

    for p in range(n_pages):
        page_copy(b, slot, p).wait()

    def slab(i):
        return past_ref[slot, i * KV_W:(i + 1) * KV_W, :].astype(BF16)

    kc_t = _pool_blocks_t(slab(0), wk_ref[...]).astype(BF16)
    vc_t = _pool_blocks_t(slab(1), wv_ref[...]).astype(BF16)
    c_end = (_lane_iota((N_HEADS, LANES)) + 1) * CMP_BLOCK - 1
    s_c = jnp.where(c_end <= q_pos, jnp.dot(qp, kc_t, preferred_element_type=F32), NEG_INF)
    p_c = jnp.where(c_end <= q_pos, jnp.exp2(s_c - jnp.max(s_c, axis=-1, keepdims=True)), 0.0)
    l_c = jnp.sum(p_c, axis=-1, keepdims=True)
    inv_c = jnp.where(l_c > 0.0, 1.0 / l_c, 0.0)
    o_cmp = _dot_nt(p_c.astype(BF16), vc_t) * inv_c
    p_cmp = p_c * inv_c
    psum = jnp.concatenate(
        [jnp.broadcast_to(jnp.sum(p_cmp[kvh * GROUP:(kvh + 1) * GROUP], axis=0, keepdims=True), (GROUP, LANES))
         for kvh in range(N_KV_HEADS)], axis=0)

    sel = _select_blocks(psum, q_pos, n_blk)
    selexp = jnp.dot(sel.astype(BF16), exp_ref[...], preferred_element_type=F32)
    new_blk = past_len // SEL_BLOCK
    ok_new = sel[:, 2 * new_blk:2 * new_blk + 1] > 0.5
    ok_past = (selexp > 0.5) & (_lane_iota((N_HEADS, past_len)) <= q_pos)
    o_sel = with_new_key(jnp.dot(qp, slab(2), preferred_element_type=F32), ok_past, slab(3),
                         kv_new[:, 2 * KV_W:3 * KV_W], kv_new[:, 3 * KV_W:4 * KV_W], ok_new)

    o_by_kvh = [[tuple(o[kvh * GROUP + g:kvh * GROUP + g + 1] for o in (o_cmp, o_sel, o_win))
                 for g in range(GROUP)] for kvh in range(N_KV_HEADS)]
    out_ref[0] = _place_heads(o_by_kvh, gate_ref[0])


def _attn_sample(page_table, q, kv_new, kvw_new, gate, w_pool_k, w_pool_v, cache_t, win_t):
    nb, n_pages = page_table.shape
    n_feat, page = cache_t.shape[1], cache_t.shape[2]
    w_buf = win_t.shape[2]
    past_len = n_pages * page
    row3 = lambda w: pl.BlockSpec((1, 1, w), lambda b, pt: (b, 0, 0))
    full = lambda shape: pl.BlockSpec(shape, lambda b, pt: (0,) * len(shape))
    win_spec = pl.BlockSpec((1, 2 * KV_W, w_buf), lambda b, pt: (b, 0, 0))
    grid_spec = pltpu.PrefetchScalarGridSpec(
        num_scalar_prefetch=1,
        grid=(nb,),
        in_specs=[row3(N_HEADS * HEAD_DIM), row3(N_PAGED_SLOTS * KV_W), row3(2 * KV_W), row3(LANES),
                  full((past_len, LANES)), full((past_len, LANES)), full((2 * KV_W, nb)),
                  pl.BlockSpec(memory_space=pl.ANY), win_spec],
        out_specs=[row3(N_HEADS * HEAD_DIM), win_spec],
        scratch_shapes=[pltpu.VMEM((2, n_feat, past_len), F32),
                        pltpu.VMEM((LANES, past_len), BF16),
                        pltpu.SemaphoreType.DMA((2, n_pages))],
    )
    r3 = lambda a: a.reshape(nb, 1, a.shape[-1])
    return pl.pallas_call(
        _attn_sample_kernel,
        grid_spec=grid_spec,
        out_shape=[jax.ShapeDtypeStruct((nb, 1, N_HEADS * HEAD_DIM), F32),
                   jax.ShapeDtypeStruct(win_t.shape, F32)],
        compiler_params=_cparams("arbitrary"),
        name="attn_sample",
    )(page_table, r3(q), r3(kv_new), r3(kvw_new), r3(gate), w_pool_k, w_pool_v, kvw_new.T, cache_t, win_t)


FF_CHUNK = 1024


def _rms(x, g):
    return x * lax.rsqrt(jnp.mean(x * x, axis=-1, keepdims=True) + RMS_EPS) * g


def _mlp_kernel(x_ref, conv_ref, attn_ref, wout_ref, gm_ref, wup_ref, wdn_ref, gf_ref, y_ref, h_ref):
    h_ref[...] = (x_ref[...]
                  + jnp.dot(conv_ref[...].astype(BF16), wout_ref[0:C_CONV, :], preferred_element_type=F32)
                  + jnp.dot(attn_ref[...].astype(BF16), wout_ref[C_CONV:, :], preferred_element_type=F32))
    hn = _rms(h_ref[...], gm_ref[...]).astype(BF16)
    acc = None
    for c in range(D_FF // FF_CHUNK):
        m = jnp.dot(hn, wup_ref[:, c * FF_CHUNK:(c + 1) * FF_CHUNK], preferred_element_type=F32)
        m = jnp.maximum(m, 0.0)
        t = jnp.dot((m * m).astype(BF16), wdn_ref[c * FF_CHUNK:(c + 1) * FF_CHUNK, :], preferred_element_type=F32)
        acc = t if acc is None else acc + t
    y_ref[...] = _rms(h_ref[...] + acc, gf_ref[...])


def _mlp(x, conv_y, attn_y, w_out, g_mlp, w_up, w_down, g_final, tm):
    n = x.shape[0]
    row = lambda w: pl.BlockSpec((tm, w), lambda i: (i, 0))
    once = lambda shape: pl.BlockSpec(shape, lambda i: (0,) * len(shape), pipeline_mode=pl.Buffered(1))
    return pl.pallas_call(
        _mlp_kernel,
        grid=(n // tm,),
        in_specs=[row(D_MODEL), row(C_CONV), row(N_HEADS * HEAD_DIM), once((D_MODEL, D_MODEL)), once((1, D_MODEL)),
                  once((D_MODEL, D_FF)), once((D_FF, D_MODEL)), once((1, D_MODEL))],
        out_specs=row(D_MODEL),
        out_shape=jax.ShapeDtypeStruct((n, D_MODEL), F32),
        scratch_shapes=[pltpu.VMEM((tm, D_MODEL), F32)],
        compiler_params=_cparams("parallel"),
        name="mlp",
    )(x, conv_y, attn_y, w_out, g_mlp, w_up, w_down, g_final)


def kernel(x_prompt, x_sample, cache_kv, cache_win, state_conv, page_table, g_attn_norm, w_in, w_dw, b_dw,
           conv_ln_g, conv_ln_b, w_cmp_k, w_cmp_v, w_out, g_mlp_norm, w_up, w_down, g_final):
    depth = w_in.shape[0]
    assert depth == 1, "single-layer trunk"
    nb, seq, _ = x_prompt.shape
    db, dseq, _ = x_sample.shape
    assert dseq == 1
    n_phys, page = cache_kv.shape[1], cache_kv.shape[2]
    w_buf = cache_win.shape[2]

    w_main = jnp.concatenate([w_in[0][:, :COL_KV], w_in[0][:, COL_GATE:],
                              jnp.zeros((D_MODEL, IN_COLS_PAD - IN_COLS), F32)], axis=1).astype(BF16)
    w_kv = jnp.transpose(w_in[0][:, COL_KV:COL_GATE]).astype(BF16)
    w_out_b, w_up_b, w_down_b = w_out[0].astype(BF16), w_up[0].astype(BF16), w_down[0].astype(BF16)
    row = lambda a: a.reshape(1, -1)
    g_attn, g_mlp, g_fin = row(g_attn_norm[0]), row(g_mlp_norm[0]), row(g_final)
    conv_w = (w_dw[0], row(b_dw[0]), row(conv_ln_g[0]), row(conv_ln_b[0]))
    mlp_w = (w_out_b, g_mlp, w_up_b, w_down_b, g_fin)

    xp = x_prompt.reshape(nb * seq, D_MODEL)
    glu, q, gate, kv_t, kvw_t = _project(xp, g_attn, w_main, w_kv, 512, seq)
    per_b = lambda a: a.reshape(nb, seq, a.shape[-1])
    glu3 = per_b(glu)
    conv_y = _conv_prompt(glu3, *conv_w)
    attn_y = _attn_prompt(per_b(q), kv_t, kvw_t, per_b(gate), _pool_matrix(w_cmp_k[0], seq),
                          _pool_matrix(w_cmp_v[0], seq))
    y_prompt = _mlp(xp, conv_y.reshape(nb * seq, C_CONV), attn_y.reshape(nb * seq, -1), *mlp_w, 512)
    pos_major = lambda a, n_slots: jnp.transpose(
        a.reshape(a.shape[0], n_slots, N_KV_HEADS, HEAD_DIM, a.shape[-1]), (0, 4, 1, 2, 3))[None]
    new_kv_p = pos_major(kv_t, N_PAGED_SLOTS)
    new_win_p = pos_major(kvw_t[:, :, seq - min(WINDOW, seq):], 2)
    new_conv_p = glu3[:, seq - (CONV_WIDTH - 1):][None]

    xs = x_sample.reshape(db, D_MODEL)
    glu_s, q_s, gate_s, kv_s, kvw_s = _project(xs, g_attn, w_main, w_kv, db)
    conv_y_s, conv_st = _conv_sample(jnp.transpose(state_conv[0], (1, 0, 2)), glu_s, *conv_w)
    cache_t = jnp.transpose(cache_kv[0], (0, 2, 3, 4, 1)).reshape(n_phys, N_PAGED_SLOTS * KV_W, page)
    win_t = jnp.transpose(cache_win[0], (0, 2, 3, 4, 1)).reshape(db, 2 * KV_W, w_buf)
    n_past = page_table.shape[1] * page
    attn_s, new_win_t = _attn_sample(page_table, q_s, kv_s, kvw_s, gate_s, _pool_matrix(w_cmp_k[0], n_past),
                                     _pool_matrix(w_cmp_v[0], n_past), cache_t, win_t)
    new_win_s = pos_major(new_win_t, 2)
    y_sample = _mlp(xs, conv_y_s, attn_s.reshape(db, -1), *mlp_w, db)

    return (y_prompt.reshape(nb, seq, D_MODEL), y_sample.reshape(db, 1, D_MODEL), new_kv_p, new_win_p, new_conv_p,
            kv_s.reshape(1, db, 1, N_PAGED_SLOTS, N_KV_HEADS, HEAD_DIM), new_win_s,
            jnp.transpose(conv_st, (1, 0, 2))[None])
```

```python
import functools

import jax
import jax.numpy as jnp
from jax import lax
from jax.experimental import pallas as pl
from jax.experimental.pallas import tpu as pltpu

F32 = jnp.float32
BF16 = jnp.bfloat16

D_MODEL = 1024
C_CONV = 512
CONV_WIDTH = 31
N_HEADS = 8
HEAD_DIM = 64
N_KV_HEADS = 2
GROUP = N_HEADS // N_KV_HEADS
KV_W = N_KV_HEADS * HEAD_DIM
N_PAGED_SLOTS = 4
D_FF = 4 * D_MODEL
CMP_BLOCK = 32
SEL_BLOCK = 64
TOP_N = 16
N_LOCAL = 2
WINDOW = 512
FORCE_BONUS = float(GROUP + 1)
NEG_INF = -1e30
RMS_EPS = 1e-6
LN_EPS = 1e-5
ATTN_SCALE = HEAD_DIM ** -0.5
LOG2E = 1.4426950408889634

COL_Q = 2 * C_CONV
COL_KV = COL_Q + N_HEADS * HEAD_DIM
COL_WIN = COL_KV + N_PAGED_SLOTS * KV_W
COL_GATE = COL_WIN + 2 * KV_W
IN_COLS = COL_GATE + 3 * N_HEADS
LANES = 128
IN_COLS_PAD = -(-IN_COLS // LANES) * LANES

VMEM_LIMIT = 56 * 1024 * 1024


def _cparams(*sem):
    return pltpu.CompilerParams(dimension_semantics=sem, vmem_limit_bytes=VMEM_LIMIT)


def _const_spec(shape):
    return pl.BlockSpec(shape, lambda *_: (0,) * len(shape))


def _dot_nt(a, b):
    return lax.dot_general(a, b, (((1,), (1,)), ((), ())), preferred_element_type=F32)


N_KV_COLS = COL_GATE - COL_KV


def _proj_kernel(x_ref, g_ref, w_ref, wkv_ref, glu_ref, q_ref, gate_ref, kv_ref, kvw_ref, *, kv_t):
    x = x_ref[...]
    ms = jnp.mean(x * x, axis=-1, keepdims=True)
    xn = (x * lax.rsqrt(ms + RMS_EPS) * g_ref[...]).astype(BF16)
    z = jnp.dot(xn, w_ref[...], preferred_element_type=F32)
    glu_ref[...] = z[:, :C_CONV] * jax.nn.sigmoid(z[:, C_CONV:COL_Q])
    q_ref[...] = z[:, COL_Q:COL_KV]
    gate_ref[...] = jax.nn.sigmoid(z[:, COL_KV:])
    n_paged = N_PAGED_SLOTS * KV_W
    if kv_t:
        kv = _dot_nt(wkv_ref[...], xn)
        kv_ref[0] = kv[:n_paged]
        kvw_ref[0] = kv[n_paged:]
    else:
        kv = _dot_nt(xn, wkv_ref[...])
        kv_ref[...] = kv[:, :n_paged]
        kvw_ref[...] = kv[:, n_paged:]


def _project(x, g, w_main, w_kv, tm, seq=None):
    n = x.shape[0]
    row = lambda w: pl.BlockSpec((tm, w), lambda i: (i, 0))
    n_paged = N_PAGED_SLOTS * KV_W
    widths = (C_CONV, N_HEADS * HEAD_DIM, LANES)
    out_specs = [row(w) for w in widths]
    out_shape = [jax.ShapeDtypeStruct((n, w), F32) for w in widths]
    for w in (n_paged, 2 * KV_W):
        if seq is None:
            out_specs.append(row(w))
            out_shape.append(jax.ShapeDtypeStruct((n, w), F32))
        else:
            per_seq = seq // tm
            out_specs.append(pl.BlockSpec((1, w, tm), lambda i: (i // per_seq, 0, i % per_seq)))
            out_shape.append(jax.ShapeDtypeStruct((n // seq, w, seq), F32))
    return pl.pallas_call(
        functools.partial(_proj_kernel, kv_t=seq is not None),
        grid=(n // tm,),
        in_specs=[row(D_MODEL), _const_spec((1, D_MODEL)), _const_spec(w_main.shape), _const_spec(w_kv.shape)],
        out_specs=out_specs,
        out_shape=out_shape,
        compiler_params=_cparams("parallel"),
        name="projection",
    )(x, g, w_main, w_kv)


def _ln_swish(y, lng, lnb):
    mu = jnp.mean(y, axis=-1, keepdims=True)
    d = y - mu
    var = jnp.mean(d * d, axis=-1, keepdims=True)
    y = d * lax.rsqrt(var + LN_EPS) * lng + lnb
    return y * jax.nn.sigmoid(y)


CONV_ROWS = 64
CONV_PAD = 32


def _conv_prompt_kernel(glu_ref, w_ref, b_ref, lng_ref, lnb_ref, y_ref, ext_ref):
    seq = glu_ref.shape[1]
    ext_ref[0:CONV_PAD, :] = jnp.zeros((CONV_PAD, C_CONV), F32)
    ext_ref[CONV_PAD:, :] = glu_ref[0]
    shift = CONV_PAD - (CONV_WIDTH - 1)

    def body(c, carry):
        base = pl.multiple_of(c * CONV_ROWS, CONV_ROWS)
        win = ext_ref[pl.ds(base, CONV_ROWS + CONV_PAD), :]
        acc = jnp.zeros((CONV_ROWS, C_CONV), F32)
        for j in range(CONV_WIDTH):
            acc = acc + w_ref[j:j + 1, :] * win[j + shift:j + shift + CONV_ROWS, :]
        y_ref[0, pl.ds(base, CONV_ROWS), :] = _ln_swish(acc + b_ref[...], lng_ref[...], lnb_ref[...])
        return carry

    lax.fori_loop(0, seq // CONV_ROWS, body, 0)


def _conv_prompt(glu, w_dw, b_dw, lng, lnb):
    nb, seq, _ = glu.shape
    blk = pl.BlockSpec((1, seq, C_CONV), lambda b: (b, 0, 0))
    return pl.pallas_call(
        _conv_prompt_kernel,
        grid=(nb,),
        in_specs=[blk, _const_spec((CONV_WIDTH, C_CONV))] + [_const_spec((1, C_CONV))] * 3,
        out_specs=blk,
        out_shape=jax.ShapeDtypeStruct(glu.shape, F32),
        scratch_shapes=[pltpu.VMEM((seq + CONV_PAD, C_CONV), F32)],
        compiler_params=_cparams("parallel"),
        name="conv_prompt",
    )(glu, w_dw, b_dw, lng, lnb)


def _conv_sample_kernel(st_ref, glu_ref, w_ref, b_ref, lng_ref, lnb_ref, y_ref, st_out_ref):
    n_prev = CONV_WIDTH - 1
    glu = glu_ref[...]
    acc = w_ref[n_prev:n_prev + 1, :] * glu
    for j in range(n_prev):
        acc = acc + w_ref[j:j + 1, :] * st_ref[j]
    y_ref[...] = _ln_swish(acc + b_ref[...], lng_ref[...], lnb_ref[...])
    st_out_ref[0:n_prev - 1] = st_ref[1:n_prev]
    st_out_ref[n_prev - 1] = glu


def _conv_sample(state_t, glu, w_dw, b_dw, lng, lnb):
    nb = glu.shape[0]
    return pl.pallas_call(
        _conv_sample_kernel,
        grid=(1,),
        in_specs=[_const_spec(state_t.shape), _const_spec((nb, C_CONV)), _const_spec((CONV_WIDTH, C_CONV))]
        + [_const_spec((1, C_CONV))] * 3,
        out_specs=[_const_spec((nb, C_CONV)), _const_spec(state_t.shape)],
        out_shape=[jax.ShapeDtypeStruct((nb, C_CONV), F32), jax.ShapeDtypeStruct(state_t.shape, F32)],
        compiler_params=_cparams("arbitrary"),
        name="conv_sample",
    )(state_t, glu, w_dw, b_dw, lng, lnb)


def _lane_iota(shape):
    return lax.broadcasted_iota(jnp.int32, shape, len(shape) - 1)


def _head_rows(q, kvh):
    rows = q.shape[0]
    half = _lane_iota((rows, KV_W)) // HEAD_DIM
    out = []
    for g in range(GROUP):
        h = kvh * GROUP + g
        slab = q[:, KV_W * (h // 2):KV_W * (h // 2 + 1)]
        if h % 2 != kvh:
            slab = pltpu.roll(slab, HEAD_DIM, 1)
        out.append(jnp.where(half == kvh, slab * (ATTN_SCALE * LOG2E), 0.0))
    return out


def _masked_softmax_pv(s, ok, v):
    s = jnp.where(ok, s, NEG_INF)
    m = jnp.max(s, axis=-1, keepdims=True)
    p = jnp.where(ok, jnp.exp2(s - m), 0.0)
    l = jnp.sum(p, axis=-1, keepdims=True)
    return jnp.dot(p.astype(BF16), v, preferred_element_type=F32), jnp.where(l > 0.0, 1.0 / l, 0.0)


def _select_blocks(psum, q_pos, n_blk):
    shape = psum.shape
    lane = _lane_iota(shape)
    even = (lane & 1) == 0
    pair = psum + jnp.where(even, pltpu.roll(psum, LANES - 1, 1), pltpu.roll(psum, 1, 1))
    blk = lane // (SEL_BLOCK // CMP_BLOCK)
    back = q_pos // SEL_BLOCK - blk
    forced = (blk == 0) | ((back >= 0) & (back < N_LOCAL))
    score = jnp.where(back >= 0, pair + jnp.where(forced, FORCE_BONUS, 0.0), -jnp.inf)
    rank = jnp.zeros(shape, jnp.int32)
    for i in range(n_blk):
        si = score[:, 2 * i:2 * i + 1]
        beats = (si > score) | ((si == score) & (i < blk))
        rank = rank + beats.astype(jnp.int32)
    return jnp.where((rank < min(TOP_N, n_blk)) & even, 1.0, 0.0)


def _block_expander(n_keys, row_stride):
    li = lax.broadcasted_iota(jnp.int32, (LANES, n_keys), 0)
    ti = lax.broadcasted_iota(jnp.int32, (LANES, n_keys), 1)
    return jnp.where(li == row_stride * (ti // SEL_BLOCK), 1.0, 0.0).astype(BF16)


def _gate_cols(gates, heads):
    return [jnp.concatenate([gates[:, 3 * h + k:3 * h + k + 1] for h in heads], axis=0) for k in range(3)]


def _place_heads(o_heads):
    rows = o_heads[0].shape[0]
    low = _lane_iota((rows, KV_W)) < HEAD_DIM
    slabs = []
    for s in range(N_HEADS // 2):
        pair = []
        for h in (2 * s, 2 * s + 1):
            o = o_heads[h]
            if h // GROUP != h % 2:
                o = pltpu.roll(o, HEAD_DIM, 1)
            pair.append(o)
        slabs.append(jnp.where(low, pair[0], pair[1]))
    return jnp.concatenate(slabs, axis=1)


QB = 128
KT = 256
WIN_KEYS = WINDOW + QB


def _select_bias_t(p_blk, q_pos):
    n_blk = p_blk.shape[0]
    blk = lax.broadcasted_iota(jnp.int32, p_blk.shape, 0)
    back = q_pos // SEL_BLOCK - blk
    forced = (blk == 0) | ((back >= 0) & (back < N_LOCAL))
    score = jnp.where(back >= 0, p_blk + jnp.where(forced, FORCE_BONUS, 0.0), -jnp.inf)
    rank = jnp.zeros(score.shape, jnp.int32)
    for i in range(n_blk):
        si = score[i:i + 1, :]
        beats = (si > score) | ((si == score) & (i < blk))
        rank = rank + beats.astype(jnp.int32)
    return jnp.where((rank < min(TOP_N, n_blk)) & (back >= 0), 0.0, NEG_INF)


WIN_CHUNK = 128


def _attn_prompt_kernel(q_ref, kv_ref, kvw_ref, gate_ref, wk_ref, wv_ref, out_ref,
                        kc_ref, vc_ref, ks_ref, vs_ref, kw_ref, vw_ref, exp_ref,
                        bias_ref, s_ref, mrun_ref, mb_ref, lrun_ref, acc_ref, psum_ref):
    seq = kv_ref.shape[2]
    n_blk = seq // SEL_BLOCK
    n_kt = seq // KT
    n_wc = WIN_KEYS // WIN_CHUNK
    rows = GROUP * QB
    qi = pl.program_id(1)
    qs = qi * QB

    @pl.when((pl.program_id(0) == 0) & (qi == 0))
    def _():
        exp_ref[...] = _block_expander(seq, 1)

    @pl.when(qi == 0)
    def _():
        kc_ref[...] = _pool_blocks_t(kv_ref[0, 0:KV_W, :].astype(BF16), wk_ref[...]).T.astype(BF16)
        vc_ref[...] = _pool_blocks_t(kv_ref[0, KV_W:2 * KV_W, :].astype(BF16), wv_ref[...]).T.astype(BF16)
        for t in range(n_kt):
            ks_ref[t] = kv_ref[0, 2 * KV_W:3 * KV_W, t * KT:(t + 1) * KT].astype(BF16)
            vs_ref[t] = kv_ref[0, 3 * KV_W:4 * KV_W, t * KT:(t + 1) * KT].astype(BF16)
        for c in range(seq // WIN_CHUNK):
            kw_ref[c] = kvw_ref[0, 0:KV_W, c * WIN_CHUNK:(c + 1) * WIN_CHUNK].astype(BF16)
            vw_ref[c] = kvw_ref[0, KV_W:2 * KV_W, c * WIN_CHUNK:(c + 1) * WIN_CHUNK].astype(BF16)

    q = q_ref[0]
    q_pos = qs + lax.broadcasted_iota(jnp.int32, (QB, 1), 0)
    q_pos_t = qs + lax.broadcasted_iota(jnp.int32, (1, QB), 1)
    rep = lambda a: jnp.concatenate([a] * GROUP, axis=0)
    add_bias = lambda s, b: (s.reshape(GROUP, QB, s.shape[-1]) + b[None]).reshape(s.shape)
    n_tiles = (qs + QB + KT - 1) // KT
    qp = [jnp.concatenate(_head_rows(q, kvh), axis=0).astype(BF16) for kvh in range(N_KV_HEADS)]

    o_cmp = []
    c_end = (_lane_iota((rows, LANES)) + 1) * CMP_BLOCK - 1
    c_end_t = (lax.broadcasted_iota(jnp.int32, (LANES, rows), 0) + 1) * CMP_BLOCK - 1
    last = n_tiles - 1
    diag_bias = jnp.where(last * KT + _lane_iota((QB, KT)) <= q_pos, 0.0, NEG_INF)
    for kvh in range(N_KV_HEADS):
        o_cmp.append(_masked_softmax_pv(_dot_nt(qp[kvh], kc_ref[...]), c_end <= rep(q_pos), vc_ref[...]))
        s_t = _dot_nt(kc_ref[...], qp[kvh])
        ok_t = c_end_t <= jnp.concatenate([q_pos_t] * GROUP, axis=1)
        s_t = jnp.where(ok_t, s_t, NEG_INF)
        p_t = jnp.where(ok_t, jnp.exp2(s_t - jnp.max(s_t, axis=0, keepdims=True)), 0.0)
        l_t = jnp.sum(p_t, axis=0, keepdims=True)
        p_t = p_t * jnp.where(l_t > 0.0, 1.0 / l_t, 0.0)
        psum_ref[...] = sum(p_t[:, g * QB:(g + 1) * QB] for g in range(GROUP))
        per_sel = SEL_BLOCK // CMP_BLOCK
        p_blk = sum(psum_ref[pl.ds(j, n_blk, stride=per_sel), :] for j in range(per_sel))
        blk_bias = _select_bias_t(p_blk, q_pos_t)
        blk_bias = jnp.concatenate([blk_bias, jnp.zeros((LANES - n_blk, QB), F32)], axis=0).T
        key_bias = jnp.dot(blk_bias.astype(BF16), exp_ref[...], preferred_element_type=F32)
        for t in range(n_kt):
            @pl.when(t < n_tiles)
            def _():
                bias_ref[kvh, t] = key_bias[:, t * KT:(t + 1) * KT]
        bias_ref[kvh, last] += diag_bias

    mrun_ref[...] = jnp.full(mrun_ref.shape, NEG_INF, F32)

    def max_pass(t, carry):
        k_t = ks_ref[t]
        for kvh in range(N_KV_HEADS):
            s = add_bias(jnp.dot(qp[kvh], k_t, preferred_element_type=F32), bias_ref[kvh, t])
            s_ref[kvh, t] = s
            mrun_ref[kvh] = jnp.maximum(mrun_ref[kvh], jnp.maximum(s[:, :LANES], s[:, LANES:]))
        return carry

    lax.fori_loop(0, n_tiles, max_pass, 0)
    for kvh in range(N_KV_HEADS):
        mb_ref[kvh] = jnp.broadcast_to(jnp.max(mrun_ref[kvh], axis=-1, keepdims=True), (rows, LANES))
    lrun_ref[...] = jnp.zeros(lrun_ref.shape, F32)
    acc_ref[...] = jnp.zeros(acc_ref.shape, F32)

    def sum_pass(t, carry):
        v_t = vs_ref[t]
        for kvh in range(N_KV_HEADS):
            s = s_ref[kvh, t]
            mb = mb_ref[kvh]
            p = jnp.exp2(jnp.concatenate([s[:, :LANES] - mb, s[:, LANES:] - mb], axis=1))
            lrun_ref[kvh] += p[:, :LANES] + p[:, LANES:]
            acc_ref[kvh] += _dot_nt(p.astype(BF16), v_t)
        return carry

    lax.fori_loop(0, n_tiles, sum_pass, 0)

    win_c0 = jnp.maximum(qs - WINDOW, 0) // WIN_CHUNK
    rel = q_pos - (win_c0 * WIN_CHUNK + _lane_iota((QB, WIN_KEYS)))
    win_bias = jnp.where((rel >= 0) & (rel <= WINDOW), 0.0, NEG_INF)

    o_heads = []
    for kvh in range(N_KV_HEADS):
        g_cmp, g_sel, g_win = _gate_cols(gate_ref[0], [kvh * GROUP + g for g in range(GROUP)])
        l = jnp.sum(lrun_ref[kvh], axis=-1, keepdims=True)
        s_w = jnp.concatenate([jnp.dot(qp[kvh], kw_ref[win_c0 + c], preferred_element_type=F32) for c in range(n_wc)],
                              axis=1)
        s_w = add_bias(s_w, win_bias)
        p_w = jnp.exp2(s_w - jnp.max(s_w, axis=-1, keepdims=True))
        o_win = sum(_dot_nt(p_w[:, c * WIN_CHUNK:(c + 1) * WIN_CHUNK].astype(BF16), vw_ref[win_c0 + c])
                    for c in range(n_wc))
        o_c, inv_c = o_cmp[kvh]
        o = (o_c * (inv_c * g_cmp) + acc_ref[kvh] * (jnp.where(l > 0.0, 1.0 / l, 0.0) * g_sel)
             + o_win * (g_win / jnp.sum(p_w, axis=-1, keepdims=True)))
        o_heads += [o[g * QB:(g + 1) * QB] for g in range(GROUP)]

    out_ref[0] = _place_heads(o_heads)


def _attn_prompt(q, kv_t, kvw_t, gate, w_pool_k, w_pool_v):
    nb, seq, _ = q.shape
    per_q = lambda w: pl.BlockSpec((1, QB, w), lambda b, i: (b, i, 0))
    per_b = lambda w: pl.BlockSpec((1, w, seq), lambda b, i: (b, 0, 0))
    wspec = pl.BlockSpec((seq, LANES), lambda b, i: (0, 0))
    rows = GROUP * QB
    return pl.pallas_call(
        _attn_prompt_kernel,
        grid=(nb, seq // QB),
        in_specs=[per_q(N_HEADS * HEAD_DIM), per_b(N_PAGED_SLOTS * KV_W), per_b(2 * KV_W), per_q(LANES), wspec, wspec],
        out_specs=per_q(N_HEADS * HEAD_DIM),
        out_shape=jax.ShapeDtypeStruct(q.shape, F32),
        scratch_shapes=[pltpu.VMEM((LANES, KV_W), BF16), pltpu.VMEM((LANES, KV_W), BF16)]
        + [pltpu.VMEM((seq // KT, KV_W, KT), BF16)] * 2
        + [pltpu.VMEM((seq // WIN_CHUNK, KV_W, WIN_CHUNK), BF16)] * 2
        + [pltpu.VMEM((LANES, seq), BF16), pltpu.VMEM((N_KV_HEADS, seq // KT, QB, KT), F32),
           pltpu.VMEM((N_KV_HEADS, seq // KT, rows, KT), F32)]
        + [pltpu.VMEM((N_KV_HEADS, rows, LANES), F32)] * 4
        + [pltpu.VMEM((LANES, QB), F32)],
        compiler_params=_cparams("arbitrary", "arbitrary"),
        name="attn_prompt",
    )(q, kv_t, kvw_t, gate, w_pool_k, w_pool_v)


def _pool_matrix(w_cmp, n_keys):
    eye = jnp.eye(n_keys // CMP_BLOCK, LANES // N_KV_HEADS, dtype=F32)
    per_head = [(eye[:, None, :] * w_cmp[None, :, h, None]).reshape(n_keys, -1) for h in range(N_KV_HEADS)]
    return jnp.concatenate(per_head, axis=1).astype(BF16)


def _pool_blocks_t(x_t, w_pool):
    res = jnp.dot(x_t, w_pool, preferred_element_type=F32)
    row = lax.broadcasted_iota(jnp.int32, res.shape, 0)
    own = jnp.where(row < HEAD_DIM, res, pltpu.roll(res, HEAD_DIM, 1))
    return jnp.where(_lane_iota(res.shape) < w_pool.shape[0] // CMP_BLOCK, own, 0.0)


def _attn_sample_kernel(pt_ref, q_ref, kvn_ref, kvwn_ref, gate_ref, wk_ref, wv_ref, kvwt_ref, cache_ref, win_ref,
                        out_ref, win_out_ref, past_ref, exp_ref, sem):
    b = pl.program_id(0)
    nb = pl.num_programs(0)
    n_pages = pt_ref.shape[1]
    page = cache_ref.shape[2]
    past_len = n_pages * page
    n_blk = past_len // SEL_BLOCK + 1
    w_buf = win_ref.shape[2]
    slot = b % 2

    def page_copy(bb, sl, p):
        return pltpu.make_async_copy(cache_ref.at[pt_ref[bb, p]], past_ref.at[sl, :, pl.ds(p * page, page)],
                                     sem.at[sl, p])

    @pl.when(b == 0)
    def _():
        for p in range(n_pages):
            page_copy(0, 0, p).start()
        exp_ref[...] = _block_expander(past_len, SEL_BLOCK // CMP_BLOCK)

    @pl.when(b + 1 < nb)
    def _():
        for p in range(n_pages):
            page_copy(b + 1, 1 - slot, p).start()

    q = q_ref[0]
    q_pos = jnp.full((N_HEADS, 1), past_len, jnp.int32)
    qp_f = jnp.concatenate([r for kvh in range(N_KV_HEADS) for r in _head_rows(q, kvh)], axis=0)
    qp = qp_f.astype(BF16)
    kv_new = kvn_ref[0]
    kvw_new = kvwn_ref[0]

    def with_new_key(s_past, ok_past, vt_past, k_new, v_new, ok_new):
        s_new = jnp.sum(qp_f * k_new, axis=-1, keepdims=True)
        s_past = jnp.where(ok_past, s_past, NEG_INF)
        s_new = jnp.where(ok_new, s_new, NEG_INF)
        m = jnp.maximum(jnp.max(s_past, axis=-1, keepdims=True), s_new)
        p_past = jnp.where(ok_past, jnp.exp2(s_past - m), 0.0)
        p_new = jnp.where(ok_new, jnp.exp2(s_new - m), 0.0)
        l = jnp.sum(p_past, axis=-1, keepdims=True) + p_new
        o = _dot_nt(p_past.astype(BF16), vt_past) + p_new * v_new
        return o * jnp.where(l > 0.0, 1.0 / l, 0.0)

    win = win_ref[0]
    rel = past_len - (past_len - w_buf + _lane_iota((N_HEADS, w_buf)))
    o_win = with_new_key(jnp.dot(qp, win[0:KV_W].astype(BF16), preferred_element_type=F32),
                         (rel >= 0) & (rel <= WINDOW), win[KV_W:2 * KV_W].astype(BF16),
                         kvw_new[:, 0:KV_W], kvw_new[:, KV_W:2 * KV_W], jnp.full((N_HEADS, 1), True))
    new_col = jnp.sum(jnp.where(_lane_iota(kvwt_ref.shape) == b, kvwt_ref[...], 0.0), axis=1, keepdims=True)
    win_out_ref[0] = jnp.where(_lane_iota(win.shape) == w_buf - 1, new_col, pltpu.roll(win, w_buf - 1, 1))

    for p in range(n_pages):
        page_copy(b, slot, p).wait()

    def slab(i):
        return past_ref[slot, i * KV_W:(i + 1) * KV_W, :].astype(BF16)

    kc_t = _pool_blocks_t(slab(0), wk_ref[...]).astype(BF16)
    vc_t = _pool_blocks_t(slab(1), wv_ref[...]).astype(BF16)
    c_end = (_lane_iota((N_HEADS, LANES)) + 1) * CMP_BLOCK - 1
    s_c = jnp.where(c_end <= q_pos, jnp.dot(qp, kc_t, preferred_element_type=F32), NEG_INF)
    p_c = jnp.where(c_end <= q_pos, jnp.exp2(s_c - jnp.max(s_c, axis=-1, keepdims=True)), 0.0)
    l_c = jnp.sum(p_c, axis=-1, keepdims=True)
    inv_c = jnp.where(l_c > 0.0, 1.0 / l_c, 0.0)
    o_cmp = _dot_nt(p_c.astype(BF16), vc_t) * inv_c
    p_cmp = p_c * inv_c
    psum = jnp.concatenate(
        [jnp.broadcast_to(jnp.sum(p_cmp[kvh * GROUP:(kvh + 1) * GROUP], axis=0, keepdims=True), (GROUP, LANES))
         for kvh in range(N_KV_HEADS)], axis=0)

    sel = _select_blocks(psum, q_pos, n_blk)
    selexp = jnp.dot(sel.astype(BF16), exp_ref[...], preferred_element_type=F32)
    new_blk = past_len // SEL_BLOCK
    ok_new = sel[:, 2 * new_blk:2 * new_blk + 1] > 0.5
    ok_past = (selexp > 0.5) & (_lane_iota((N_HEADS, past_len)) <= q_pos)
    o_sel = with_new_key(jnp.dot(qp, slab(2), preferred_element_type=F32), ok_past, slab(3),
                         kv_new[:, 2 * KV_W:3 * KV_W], kv_new[:, 3 * KV_W:4 * KV_W], ok_new)

    g_cmp, g_sel, g_win = _gate_cols(gate_ref[0], range(N_HEADS))
    o = o_cmp * g_cmp + o_sel * g_sel + o_win * g_win
    out_ref[0] = _place_heads([o[h:h + 1] for h in range(N_HEADS)])


def _attn_sample(page_table, q, kv_new, kvw_new, gate, w_pool_k, w_pool_v, cache_t, win_t):
    nb, n_pages = page_table.shape
    n_feat, page = cache_t.shape[1], cache_t.shape[2]
    w_buf = win_t.shape[2]
    past_len = n_pages * page
    row3 = lambda w: pl.BlockSpec((1, 1, w), lambda b, pt: (b, 0, 0))
    full = lambda shape: pl.BlockSpec(shape, lambda b, pt: (0,) * len(shape))
    win_spec = pl.BlockSpec((1, 2 * KV_W, w_buf), lambda b, pt: (b, 0, 0))
    grid_spec = pltpu.PrefetchScalarGridSpec(
        num_scalar_prefetch=1,
        grid=(nb,),
        in_specs=[row3(N_HEADS * HEAD_DIM), row3(N_PAGED_SLOTS * KV_W), row3(2 * KV_W), row3(LANES),
                  full((past_len, LANES)), full((past_len, LANES)), full((2 * KV_W, nb)),
                  pl.BlockSpec(memory_space=pl.ANY), win_spec],
        out_specs=[row3(N_HEADS * HEAD_DIM), win_spec],
        scratch_shapes=[pltpu.VMEM((2, n_feat, past_len), F32),
                        pltpu.VMEM((LANES, past_len), BF16),
                        pltpu.SemaphoreType.DMA((2, n_pages))],
    )
    r3 = lambda a: a.reshape(nb, 1, a.shape[-1])
    return pl.pallas_call(
        _attn_sample_kernel,
        grid_spec=grid_spec,
        out_shape=[jax.ShapeDtypeStruct((nb, 1, N_HEADS * HEAD_DIM), F32),
                   jax.ShapeDtypeStruct(win_t.shape, F32)],
        compiler_params=_cparams("arbitrary"),
        name="attn_sample",
    )(page_table, r3(q), r3(kv_new), r3(kvw_new), r3(gate), w_pool_k, w_pool_v, kvw_new.T, cache_t, win_t)


FF_CHUNK = 1024


def _rms(x, g):
    return x * lax.rsqrt(jnp.mean(x * x, axis=-1, keepdims=True) + RMS_EPS) * g


def _mlp_kernel(x_ref, conv_ref, attn_ref, wout_ref, gm_ref, wup_ref, wdn_ref, gf_ref, y_ref, h_ref):
    h_ref[...] = (x_ref[...]
                  + jnp.dot(conv_ref[...].astype(BF16), wout_ref[0:C_CONV, :], preferred_element_type=F32)
                  + jnp.dot(attn_ref[...].astype(BF16), wout_ref[C_CONV:, :], preferred_element_type=F32))
    hn = _rms(h_ref[...], gm_ref[...]).astype(BF16)
    acc = None
    for c in range(D_FF // FF_CHUNK):
        m = jnp.dot(hn, wup_ref[:, c * FF_CHUNK:(c + 1) * FF_CHUNK], preferred_element_type=F32)
        m = jnp.maximum(m, 0.0)
        t = jnp.dot((m * m).astype(BF16), wdn_ref[c * FF_CHUNK:(c + 1) * FF_CHUNK, :], preferred_element_type=F32)
        acc = t if acc is None else acc + t
    y_ref[...] = _rms(h_ref[...] + acc, gf_ref[...])


def _mlp(x, conv_y, attn_y, w_out, g_mlp, w_up, w_down, g_final, tm):
    n = x.shape[0]
    row = lambda w: pl.BlockSpec((tm, w), lambda i: (i, 0))
    once = lambda shape: pl.BlockSpec(shape, lambda i: (0,) * len(shape), pipeline_mode=pl.Buffered(1))
    return pl.pallas_call(
        _mlp_kernel,
        grid=(n // tm,),
        in_specs=[row(D_MODEL), row(C_CONV), row(N_HEADS * HEAD_DIM), once((D_MODEL, D_MODEL)), once((1, D_MODEL)),
                  once((D_MODEL, D_FF)), once((D_FF, D_MODEL)), once((1, D_MODEL))],
        out_specs=row(D_MODEL),
        out_shape=jax.ShapeDtypeStruct((n, D_MODEL), F32),
        scratch_shapes=[pltpu.VMEM((tm, D_MODEL), F32)],
        compiler_params=_cparams("parallel"),
        name="mlp",
    )(x, conv_y, attn_y, w_out, g_mlp, w_up, w_down, g_final)


def kernel(x_prompt, x_sample, cache_kv, cache_win, state_conv, page_table, g_attn_norm, w_in, w_dw, b_dw,
           conv_ln_g, conv_ln_b, w_cmp_k, w_cmp_v, w_out, g_mlp_norm, w_up, w_down, g_final):
    depth = w_in.shape[0]
    assert depth == 1, "single-layer trunk"
    nb, seq, _ = x_prompt.shape
    db, dseq, _ = x_sample.shape
    assert dseq == 1
    n_phys, page = cache_kv.shape[1], cache_kv.shape[2]
    w_buf = cache_win.shape[2]

    w_main = jnp.concatenate([w_in[0][:, :COL_KV], w_in[0][:, COL_GATE:],
                              jnp.zeros((D_MODEL, IN_COLS_PAD - IN_COLS), F32)], axis=1).astype(BF16)
    w_kv = jnp.transpose(w_in[0][:, COL_KV:COL_GATE]).astype(BF16)
    w_out_b, w_up_b, w_down_b = w_out[0].astype(BF16), w_up[0].astype(BF16), w_down[0].astype(BF16)
    row = lambda a: a.reshape(1, -1)
    g_attn, g_mlp, g_fin = row(g_attn_norm[0]), row(g_mlp_norm[0]), row(g_final)
    conv_w = (w_dw[0], row(b_dw[0]), row(conv_ln_g[0]), row(conv_ln_b[0]))
    mlp_w = (w_out_b, g_mlp, w_up_b, w_down_b, g_fin)

    xp = x_prompt.reshape(nb * seq, D_MODEL)
    glu, q, gate, kv_t, kvw_t = _project(xp, g_attn, w_main, w_kv, 512, seq)
    per_b = lambda a: a.reshape(nb, seq, a.shape[-1])
    glu3 = per_b(glu)
    conv_y = _conv_prompt(glu3, *conv_w)
    attn_y = _attn_prompt(per_b(q), kv_t, kvw_t, per_b(gate), _pool_matrix(w_cmp_k[0], seq),
                          _pool_matrix(w_cmp_v[0], seq))
    y_prompt = _mlp(xp, conv_y.reshape(nb * seq, C_CONV), attn_y.reshape(nb * seq, -1), *mlp_w, 512)
    pos_major = lambda a, n_slots: jnp.transpose(
        a.reshape(a.shape[0], n_slots, N_KV_HEADS, HEAD_DIM, a.shape[-1]), (0, 4, 1, 2, 3))[None]
    new_kv_p = pos_major(kv_t, N_PAGED_SLOTS)
    new_win_p = pos_major(kvw_t[:, :, seq - min(WINDOW, seq):], 2)
    new_conv_p = glu3[:, seq - (CONV_WIDTH - 1):][None]

    xs = x_sample.reshape(db, D_MODEL)
    glu_s, q_s, gate_s, kv_s, kvw_s = _project(xs, g_attn, w_main, w_kv, db)
    conv_y_s, conv_st = _conv_sample(jnp.transpose(state_conv[0], (1, 0, 2)), glu_s, *conv_w)
    cache_t = jnp.transpose(cache_kv[0], (0, 2, 3, 4, 1)).reshape(n_phys, N_PAGED_SLOTS * KV_W, page)
    win_t = jnp.transpose(cache_win[0], (0, 2, 3, 4, 1)).reshape(db, 2 * KV_W, w_buf)
    n_past = page_table.shape[1] * page
    attn_s, new_win_t = _attn_sample(page_table, q_s, kv_s, kvw_s, gate_s, _pool_matrix(w_cmp_k[0], n_past),
                                     _pool_matrix(w_cmp_v[0], n_past), cache_t, win_t)
    new_win_s = pos_major(new_win_t, 2)
    y_sample = _mlp(xs, conv_y_s, attn_s.reshape(db, -1), *mlp_w, db)

    return (y_prompt.reshape(nb, seq, D_MODEL), y_sample.reshape(db, 1, D_MODEL), new_kv_p, new_win_p, new_conv_p,
            kv_s.reshape(1, db, 1, N_PAGED_SLOTS, N_KV_HEADS, HEAD_DIM), new_win_s,
            jnp.transpose(conv_st, (1, 0, 2))[None])
```

```python
import functools

import jax
import jax.numpy as jnp
from jax import lax
from jax.experimental import pallas as pl
from jax.experimental.pallas import tpu as pltpu

F32 = jnp.float32
BF16 = jnp.bfloat16

D_MODEL = 1024
C_CONV = 512
CONV_WIDTH = 31
N_HEADS = 8
HEAD_DIM = 64
N_KV_HEADS = 2
GROUP = N_HEADS // N_KV_HEADS
KV_W = N_KV_HEADS * HEAD_DIM
N_PAGED_SLOTS = 4
D_FF = 4 * D_MODEL
CMP_BLOCK = 32
SEL_BLOCK = 64
TOP_N = 16
N_LOCAL = 2
WINDOW = 512
FORCE_BONUS = float(GROUP + 1)
NEG_INF = -1e30
RMS_EPS = 1e-6
LN_EPS = 1e-5
ATTN_SCALE = HEAD_DIM ** -0.5
LOG2E = 1.4426950408889634

COL_Q = 2 * C_CONV
COL_KV = COL_Q + N_HEADS * HEAD_DIM
COL_WIN = COL_KV + N_PAGED_SLOTS * KV_W
COL_GATE = COL_WIN + 2 * KV_W
IN_COLS = COL_GATE + 3 * N_HEADS
LANES = 128
SUBLANES = 8
IN_COLS_PAD = -(-IN_COLS // LANES) * LANES

VMEM_LIMIT = 56 * 1024 * 1024


def _cparams(*sem):
    return pltpu.CompilerParams(dimension_semantics=sem, vmem_limit_bytes=VMEM_LIMIT)


def _const_spec(shape):
    return pl.BlockSpec(shape, lambda *_: (0,) * len(shape))


def _dot_nt(a, b):
    return lax.dot_general(a, b, (((1,), (1,)), ((), ())), preferred_element_type=F32)


N_KV_COLS = COL_GATE - COL_KV


def _proj_kernel(x_ref, g_ref, w_ref, wkv_ref, glu_ref, q_ref, gate_ref, kv_ref, kvw_ref, *, kv_t):
    x = x_ref[...]
    ms = jnp.mean(x * x, axis=-1, keepdims=True)
    xn = (x * lax.rsqrt(ms + RMS_EPS) * g_ref[...]).astype(BF16)
    z = jnp.dot(xn, w_ref[...], preferred_element_type=F32)
    glu_ref[...] = z[:, :C_CONV] * jax.nn.sigmoid(z[:, C_CONV:COL_Q])
    q_ref[...] = z[:, COL_Q:COL_KV]
    gate_ref[...] = jax.nn.sigmoid(z[:, COL_KV:])
    n_paged = N_PAGED_SLOTS * KV_W
    if kv_t:
        kv = _dot_nt(wkv_ref[...], xn)
        kv_ref[0] = kv[:n_paged]
        kvw_ref[0] = kv[n_paged:]
    else:
        kv = _dot_nt(xn, wkv_ref[...])
        kv_ref[...] = kv[:, :n_paged]
        kvw_ref[...] = kv[:, n_paged:]


def _project(x, g, w_main, w_kv, tm, seq=None):
    n = x.shape[0]
    row = lambda w: pl.BlockSpec((tm, w), lambda i: (i, 0))
    n_paged = N_PAGED_SLOTS * KV_W
    widths = (C_CONV, N_HEADS * HEAD_DIM, LANES)
    out_specs = [row(w) for w in widths]
    out_shape = [jax.ShapeDtypeStruct((n, w), F32) for w in widths]
    for w in (n_paged, 2 * KV_W):
        if seq is None:
            out_specs.append(row(w))
            out_shape.append(jax.ShapeDtypeStruct((n, w), F32))
        else:
            per_seq = seq // tm
            out_specs.append(pl.BlockSpec((1, w, tm), lambda i: (i // per_seq, 0, i % per_seq)))
            out_shape.append(jax.ShapeDtypeStruct((n // seq, w, seq), F32))
    return pl.pallas_call(
        functools.partial(_proj_kernel, kv_t=seq is not None),
        grid=(n // tm,),
        in_specs=[row(D_MODEL), _const_spec((1, D_MODEL)), _const_spec(w_main.shape), _const_spec(w_kv.shape)],
        out_specs=out_specs,
        out_shape=out_shape,
        compiler_params=_cparams("parallel"),
        name="projection",
    )(x, g, w_main, w_kv)


def _ln_swish(y, lng, lnb):
    mu = jnp.mean(y, axis=-1, keepdims=True)
    d = y - mu
    var = jnp.mean(d * d, axis=-1, keepdims=True)
    y = d * lax.rsqrt(var + LN_EPS) * lng + lnb
    return y * jax.nn.sigmoid(y)


CONV_ROWS = 64
CONV_PAD = 32


def _conv_prompt_kernel(glu_ref, w_ref, b_ref, lng_ref, lnb_ref, y_ref, ext_ref):
    seq = glu_ref.shape[1]
    ext_ref[0:CONV_PAD, :] = jnp.zeros((CONV_PAD, C_CONV), F32)
    ext_ref[CONV_PAD:, :] = glu_ref[0]
    shift = CONV_PAD - (CONV_WIDTH - 1)

    def body(c, carry):
        base = pl.multiple_of(c * CONV_ROWS, CONV_ROWS)
        cols = []
        for ct in range(C_CONV // LANES):
            lanes = slice(ct * LANES, (ct + 1) * LANES)
            win = ext_ref[pl.ds(base, CONV_ROWS + CONV_PAD), lanes]
            acc = jnp.zeros((CONV_ROWS, LANES), F32)
            for r in range(SUBLANES):
                shifted = win if r == 0 else pltpu.roll(win, CONV_ROWS + CONV_PAD - r, 0)
                for j in range(CONV_WIDTH):
                    if (j + shift) % SUBLANES == r:
                        a = j + shift - r
                        acc = acc + w_ref[j:j + 1, lanes] * shifted[a:a + CONV_ROWS]
            cols.append(acc)
        y = jnp.concatenate(cols, axis=1) + b_ref[...]
        y_ref[0, pl.ds(base, CONV_ROWS), :] = _ln_swish(y, lng_ref[...], lnb_ref[...])
        return carry

    lax.fori_loop(0, seq // CONV_ROWS, body, 0)


def _conv_prompt(glu, w_dw, b_dw, lng, lnb):
    nb, seq, _ = glu.shape
    blk = pl.BlockSpec((1, seq, C_CONV), lambda b: (b, 0, 0))
    return pl.pallas_call(
        _conv_prompt_kernel,
        grid=(nb,),
        in_specs=[blk, _const_spec((CONV_WIDTH, C_CONV))] + [_const_spec((1, C_CONV))] * 3,
        out_specs=blk,
        out_shape=jax.ShapeDtypeStruct(glu.shape, F32),
        scratch_shapes=[pltpu.VMEM((seq + CONV_PAD, C_CONV), F32)],
        compiler_params=_cparams("parallel"),
        name="conv_prompt",
    )(glu, w_dw, b_dw, lng, lnb)


def _conv_sample_kernel(st_ref, glu_ref, w_ref, b_ref, lng_ref, lnb_ref, y_ref, st_out_ref):
    n_prev = CONV_WIDTH - 1
    glu = glu_ref[...]
    acc = w_ref[n_prev:n_prev + 1, :] * glu
    for j in range(n_prev):
        acc = acc + w_ref[j:j + 1, :] * st_ref[j]
    y_ref[...] = _ln_swish(acc + b_ref[...], lng_ref[...], lnb_ref[...])
    st_out_ref[0:n_prev - 1] = st_ref[1:n_prev]
    st_out_ref[n_prev - 1] = glu


def _conv_sample(state_t, glu, w_dw, b_dw, lng, lnb):
    nb = glu.shape[0]
    return pl.pallas_call(
        _conv_sample_kernel,
        grid=(1,),
        in_specs=[_const_spec(state_t.shape), _const_spec((nb, C_CONV)), _const_spec((CONV_WIDTH, C_CONV))]
        + [_const_spec((1, C_CONV))] * 3,
        out_specs=[_const_spec((nb, C_CONV)), _const_spec(state_t.shape)],
        out_shape=[jax.ShapeDtypeStruct((nb, C_CONV), F32), jax.ShapeDtypeStruct(state_t.shape, F32)],
        compiler_params=_cparams("arbitrary"),
        name="conv_sample",
    )(state_t, glu, w_dw, b_dw, lng, lnb)


def _lane_iota(shape):
    return lax.broadcasted_iota(jnp.int32, shape, len(shape) - 1)


def _head_rows(q, kvh):
    rows = q.shape[0]
    half = _lane_iota((rows, KV_W)) // HEAD_DIM
    out = []
    for g in range(GROUP):
        h = kvh * GROUP + g
        slab = q[:, KV_W * (h // 2):KV_W * (h // 2 + 1)]
        if h % 2 != kvh:
            slab = pltpu.roll(slab, HEAD_DIM, 1)
        out.append(jnp.where(half == kvh, slab * (ATTN_SCALE * LOG2E), 0.0))
    return out


def _masked_softmax_pv(s, ok, v):
    s = jnp.where(ok, s, NEG_INF)
    m = jnp.max(s, axis=-1, keepdims=True)
    p = jnp.where(ok, jnp.exp2(s - m), 0.0)
    l = jnp.sum(p, axis=-1, keepdims=True)
    return jnp.dot(p.astype(BF16), v, preferred_element_type=F32), jnp.where(l > 0.0, 1.0 / l, 0.0)


def _select_blocks(psum, q_pos, n_blk):
    shape = psum.shape
    lane = _lane_iota(shape)
    even = (lane & 1) == 0
    pair = psum + jnp.where(even, pltpu.roll(psum, LANES - 1, 1), pltpu.roll(psum, 1, 1))
    blk = lane // (SEL_BLOCK // CMP_BLOCK)
    back = q_pos // SEL_BLOCK - blk
    forced = (blk == 0) | ((back >= 0) & (back < N_LOCAL))
    score = jnp.where(back >= 0, pair + jnp.where(forced, FORCE_BONUS, 0.0), -jnp.inf)
    rank = jnp.zeros(shape, jnp.int32)
    for i in range(n_blk):
        si = score[:, 2 * i:2 * i + 1]
        beats = (si > score) | ((si == score) & (i < blk))
        rank = rank + beats.astype(jnp.int32)
    return jnp.where((rank < min(TOP_N, n_blk)) & even, 1.0, 0.0)


def _block_expander(n_keys, row_stride):
    li = lax.broadcasted_iota(jnp.int32, (LANES, n_keys), 0)
    ti = lax.broadcasted_iota(jnp.int32, (LANES, n_keys), 1)
    return jnp.where(li == row_stride * (ti // SEL_BLOCK), 1.0, 0.0).astype(BF16)


def _gate_cols(gates, heads):
    return [jnp.concatenate([gates[:, 3 * h + k:3 * h + k + 1] for h in heads], axis=0) for k in range(3)]


def _place_heads(o_heads):
    rows = o_heads[0].shape[0]
    low = _lane_iota((rows, KV_W)) < HEAD_DIM
    slabs = []
    for s in range(N_HEADS // 2):
        pair = []
        for h in (2 * s, 2 * s + 1):
            o = o_heads[h]
            if h // GROUP != h % 2:
                o = pltpu.roll(o, HEAD_DIM, 1)
            pair.append(o)
        slabs.append(jnp.where(low, pair[0], pair[1]))
    return jnp.concatenate(slabs, axis=1)


QB = 128
KT = 256
WIN_KEYS = WINDOW + QB


def _select_bias_t(p_blk, q_pos):
    n_blk = p_blk.shape[0]
    blk = lax.broadcasted_iota(jnp.int32, p_blk.shape, 0)
    back = q_pos // SEL_BLOCK - blk
    forced = (blk == 0) | ((back >= 0) & (back < N_LOCAL))
    score = jnp.where(back >= 0, p_blk + jnp.where(forced, FORCE_BONUS, 0.0), -jnp.inf)
    rank = jnp.zeros(score.shape, jnp.int32)
    for i in range(n_blk):
        si = score[i:i + 1, :]
        beats = (si > score) | ((si == score) & (i < blk))
        rank = rank + beats.astype(jnp.int32)
    return jnp.where((rank < min(TOP_N, n_blk)) & (back >= 0), 0.0, NEG_INF)


WIN_CHUNK = 128


def _attn_prompt_kernel(q_ref, kv_ref, kvw_ref, gate_ref, wk_ref, wv_ref, out_ref,
                        kc_ref, vc_ref, ks_ref, vs_ref, kw_ref, vw_ref, exp_ref,
                        bias_ref, s_ref, mrun_ref, mb_ref, lrun_ref, acc_ref, psum_ref):
    seq = kv_ref.shape[2]
    n_blk = seq // SEL_BLOCK
    n_kt = seq // KT
    n_wc = WIN_KEYS // WIN_CHUNK
    rows = GROUP * QB
    qi = pl.program_id(1)
    qs = qi * QB

    @pl.when((pl.program_id(0) == 0) & (qi == 0))
    def _():
        exp_ref[...] = _block_expander(seq, 1)

    @pl.when(qi == 0)
    def _():
        kc_ref[...] = _pool_blocks_t(kv_ref[0, 0:KV_W, :].astype(BF16), wk_ref[...]).T.astype(BF16)
        vc_ref[...] = _pool_blocks_t(kv_ref[0, KV_W:2 * KV_W, :].astype(BF16), wv_ref[...]).T.astype(BF16)
        for t in range(n_kt):
            ks_ref[t] = kv_ref[0, 2 * KV_W:3 * KV_W, t * KT:(t + 1) * KT].astype(BF16)
            vs_ref[t] = kv_ref[0, 3 * KV_W:4 * KV_W, t * KT:(t + 1) * KT].astype(BF16)
        for c in range(seq // WIN_CHUNK):
            kw_ref[c] = kvw_ref[0, 0:KV_W, c * WIN_CHUNK:(c + 1) * WIN_CHUNK].astype(BF16)
            vw_ref[c] = kvw_ref[0, KV_W:2 * KV_W, c * WIN_CHUNK:(c + 1) * WIN_CHUNK].astype(BF16)

    q = q_ref[0]
    q_pos = qs + lax.broadcasted_iota(jnp.int32, (QB, 1), 0)
    q_pos_t = qs + lax.broadcasted_iota(jnp.int32, (1, QB), 1)
    rep = lambda a: jnp.concatenate([a] * GROUP, axis=0)
    add_bias = lambda s, b: (s.reshape(GROUP, QB, s.shape[-1]) + b[None]).reshape(s.shape)
    n_tiles = (qs + QB + KT - 1) // KT
    qp = [jnp.concatenate(_head_rows(q, kvh), axis=0).astype(BF16) for kvh in range(N_KV_HEADS)]

    o_cmp = []
    c_end = (_lane_iota((rows, LANES)) + 1) * CMP_BLOCK - 1
    c_end_t = (lax.broadcasted_iota(jnp.int32, (LANES, rows), 0) + 1) * CMP_BLOCK - 1
    last = n_tiles - 1
    diag_bias = jnp.where(last * KT + _lane_iota((QB, KT)) <= q_pos, 0.0, NEG_INF)
    for kvh in range(N_KV_HEADS):
        o_cmp.append(_masked_softmax_pv(_dot_nt(qp[kvh], kc_ref[...]), c_end <= rep(q_pos), vc_ref[...]))
        s_t = _dot_nt(kc_ref[...], qp[kvh])
        ok_t = c_end_t <= jnp.concatenate([q_pos_t] * GROUP, axis=1)
        s_t = jnp.where(ok_t, s_t, NEG_INF)
        p_t = jnp.where(ok_t, jnp.exp2(s_t - jnp.max(s_t, axis=0, keepdims=True)), 0.0)
        l_t = jnp.sum(p_t, axis=0, keepdims=True)
        p_t = p_t * jnp.where(l_t > 0.0, 1.0 / l_t, 0.0)
        psum_ref[...] = sum(p_t[:, g * QB:(g + 1) * QB] for g in range(GROUP))
        per_sel = SEL_BLOCK // CMP_BLOCK
        p_blk = sum(psum_ref[pl.ds(j, n_blk, stride=per_sel), :] for j in range(per_sel))
        blk_bias = _select_bias_t(p_blk, q_pos_t)
        blk_bias = jnp.concatenate([blk_bias, jnp.zeros((LANES - n_blk, QB), F32)], axis=0).T
        key_bias = jnp.dot(blk_bias.astype(BF16), exp_ref[...], preferred_element_type=F32)
        for t in range(n_kt):
            @pl.when(t < n_tiles)
            def _():
                bias_ref[kvh, t] = key_bias[:, t * KT:(t + 1) * KT]
        bias_ref[kvh, last] += diag_bias

    mrun_ref[...] = jnp.full(mrun_ref.shape, NEG_INF, F32)

    def max_pass(t, carry):
        k_t = ks_ref[t]
        for kvh in range(N_KV_HEADS):
            s = add_bias(jnp.dot(qp[kvh], k_t, preferred_element_type=F32), bias_ref[kvh, t])
            s_ref[kvh, t] = s
            mrun_ref[kvh] = jnp.maximum(mrun_ref[kvh], jnp.maximum(s[:, :LANES], s[:, LANES:]))
        return carry

    lax.fori_loop(0, n_tiles, max_pass, 0)
    for kvh in range(N_KV_HEADS):
        mb_ref[kvh] = jnp.broadcast_to(jnp.max(mrun_ref[kvh], axis=-1, keepdims=True), (rows, LANES))
    lrun_ref[...] = jnp.zeros(lrun_ref.shape, F32)
    acc_ref[...] = jnp.zeros(acc_ref.shape, F32)

    def sum_pass(t, carry):
        v_t = vs_ref[t]
        for kvh in range(N_KV_HEADS):
            s = s_ref[kvh, t]
            mb = mb_ref[kvh]
            p = jnp.exp2(jnp.concatenate([s[:, :LANES] - mb, s[:, LANES:] - mb], axis=1))
            lrun_ref[kvh] += p[:, :LANES] + p[:, LANES:]
            acc_ref[kvh] += _dot_nt(p.astype(BF16), v_t)
        return carry

    lax.fori_loop(0, n_tiles, sum_pass, 0)

    win_c0 = jnp.maximum(qs - WINDOW, 0) // WIN_CHUNK
    rel = q_pos - (win_c0 * WIN_CHUNK + _lane_iota((QB, WIN_KEYS)))
    win_bias = jnp.where((rel >= 0) & (rel <= WINDOW), 0.0, NEG_INF)

    o_heads = []
    for kvh in range(N_KV_HEADS):
        g_cmp, g_sel, g_win = _gate_cols(gate_ref[0], [kvh * GROUP + g for g in range(GROUP)])
        l = jnp.sum(lrun_ref[kvh], axis=-1, keepdims=True)
        s_w = jnp.concatenate([jnp.dot(qp[kvh], kw_ref[win_c0 + c], preferred_element_type=F32) for c in range(n_wc)],
                              axis=1)
        s_w = add_bias(s_w, win_bias)
        p_w = jnp.exp2(s_w - jnp.max(s_w, axis=-1, keepdims=True))
        o_win = sum(_dot_nt(p_w[:, c * WIN_CHUNK:(c + 1) * WIN_CHUNK].astype(BF16), vw_ref[win_c0 + c])
                    for c in range(n_wc))
        o_c, inv_c = o_cmp[kvh]
        o = (o_c * (inv_c * g_cmp) + acc_ref[kvh] * (jnp.where(l > 0.0, 1.0 / l, 0.0) * g_sel)
             + o_win * (g_win / jnp.sum(p_w, axis=-1, keepdims=True)))
        o_heads += [o[g * QB:(g + 1) * QB] for g in range(GROUP)]

    out_ref[0] = _place_heads(o_heads)


def _attn_prompt(q, kv_t, kvw_t, gate, w_pool_k, w_pool_v):
    nb, seq, _ = q.shape
    per_q = lambda w: pl.BlockSpec((1, QB, w), lambda b, i: (b, i, 0))
    per_b = lambda w: pl.BlockSpec((1, w, seq), lambda b, i: (b, 0, 0))
    wspec = pl.BlockSpec((seq, LANES), lambda b, i: (0, 0))
    rows = GROUP * QB
    return pl.pallas_call(
        _attn_prompt_kernel,
        grid=(nb, seq // QB),
        in_specs=[per_q(N_HEADS * HEAD_DIM), per_b(N_PAGED_SLOTS * KV_W), per_b(2 * KV_W), per_q(LANES), wspec, wspec],
        out_specs=per_q(N_HEADS * HEAD_DIM),
        out_shape=jax.ShapeDtypeStruct(q.shape, F32),
        scratch_shapes=[pltpu.VMEM((LANES, KV_W), BF16), pltpu.VMEM((LANES, KV_W), BF16)]
        + [pltpu.VMEM((seq // KT, KV_W, KT), BF16)] * 2
        + [pltpu.VMEM((seq // WIN_CHUNK, KV_W, WIN_CHUNK), BF16)] * 2
        + [pltpu.VMEM((LANES, seq), BF16), pltpu.VMEM((N_KV_HEADS, seq // KT, QB, KT), F32),
           pltpu.VMEM((N_KV_HEADS, seq // KT, rows, KT), F32)]
        + [pltpu.VMEM((N_KV_HEADS, rows, LANES), F32)] * 4
        + [pltpu.VMEM((LANES, QB), F32)],
        compiler_params=_cparams("arbitrary", "arbitrary"),
        name="attn_prompt",
    )(q, kv_t, kvw_t, gate, w_pool_k, w_pool_v)


def _pool_matrix(w_cmp, n_keys):
    eye = jnp.eye(n_keys // CMP_BLOCK, LANES // N_KV_HEADS, dtype=F32)
    per_head = [(eye[:, None, :] * w_cmp[None, :, h, None]).reshape(n_keys, -1) for h in range(N_KV_HEADS)]
    return jnp.concatenate(per_head, axis=1).astype(BF16)


def _pool_blocks_t(x_t, w_pool):
    res = jnp.dot(x_t, w_pool, preferred_element_type=F32)
    row = lax.broadcasted_iota(jnp.int32, res.shape, 0)
    own = jnp.where(row < HEAD_DIM, res, pltpu.roll(res, HEAD_DIM, 1))
    return jnp.where(_lane_iota(res.shape) < w_pool.shape[0] // CMP_BLOCK, own, 0.0)


DEC_GROUP = 4


def _attn_sample_kernel(pt_ref, q_ref, kvn_ref, kvwn_ref, gate_ref, wk_ref, wv_ref, kvwt_ref, cache_ref, win_ref,
                        out_ref, win_out_ref, past_ref, exp_ref, sem):
    step = pl.program_id(0)
    n_steps = pl.num_programs(0)
    n_pages = pt_ref.shape[1]
    page = cache_ref.shape[2]
    past_len = n_pages * page
    n_blk = past_len // SEL_BLOCK + 1
    w_buf = win_ref.shape[2]
    slot = step % 2

    def page_copy(st, sl, g, p):
        return pltpu.make_async_copy(cache_ref.at[pt_ref[st * DEC_GROUP + g, p]],
                                     past_ref.at[sl, g, :, pl.ds(p * page, page)], sem.at[sl, g, p])

    def start_pages(st, sl):
        for g in range(DEC_GROUP):
            for p in range(n_pages):
                page_copy(st, sl, g, p).start()

    @pl.when(step == 0)
    def _():
        start_pages(0, 0)
        exp_ref[...] = _block_expander(past_len, SEL_BLOCK // CMP_BLOCK)

    @pl.when(step + 1 < n_steps)
    def _():
        start_pages(step + 1, 1 - slot)

    q_pos = jnp.full((N_HEADS, 1), past_len, jnp.int32)

    def new_key_softmax(qp_f, s_past, ok_past, vt_past, k_new, v_new, ok_new):
        s_new = jnp.sum(qp_f * k_new, axis=-1, keepdims=True)
        s_past = jnp.where(ok_past, s_past, NEG_INF)
        s_new = jnp.where(ok_new, s_new, NEG_INF)
        m = jnp.maximum(jnp.max(s_past, axis=-1, keepdims=True), s_new)
        p_past = jnp.where(ok_past, jnp.exp2(s_past - m), 0.0)
        p_new = jnp.where(ok_new, jnp.exp2(s_new - m), 0.0)
        l = jnp.sum(p_past, axis=-1, keepdims=True) + p_new
        o = _dot_nt(p_past.astype(BF16), vt_past) + p_new * v_new
        return o * jnp.where(l > 0.0, 1.0 / l, 0.0)

    qp_f, o_win = [], []
    for g in range(DEC_GROUP):
        b = step * DEC_GROUP + g
        qf = jnp.concatenate([r for kvh in range(N_KV_HEADS) for r in _head_rows(q_ref[g], kvh)], axis=0)
        qp_f.append(qf)
        kvw_new = kvwn_ref[g]
        win = win_ref[g]
        rel = past_len - (past_len - w_buf + _lane_iota((N_HEADS, w_buf)))
        o_win.append(new_key_softmax(qf, jnp.dot(qf.astype(BF16), win[0:KV_W].astype(BF16),
                                                 preferred_element_type=F32),
                                     (rel >= 0) & (rel <= WINDOW), win[KV_W:2 * KV_W].astype(BF16),
                                     kvw_new[:, 0:KV_W], kvw_new[:, KV_W:2 * KV_W], jnp.full((N_HEADS, 1), True)))
        new_col = jnp.sum(jnp.where(_lane_iota(kvwt_ref.shape) == b, kvwt_ref[...], 0.0), axis=1, keepdims=True)
        win_out_ref[g] = jnp.where(_lane_iota(win.shape) == w_buf - 1, new_col, pltpu.roll(win, w_buf - 1, 1))

    for g in range(DEC_GROUP):
        for p in range(n_pages):
            page_copy(step, slot, g, p).wait()

    for g in range(DEC_GROUP):
        def slab(i):
            return past_ref[slot, g, i * KV_W:(i + 1) * KV_W, :].astype(BF16)

        qp = qp_f[g].astype(BF16)
        kv_new = kvn_ref[g]

        kc_t = _pool_blocks_t(slab(0), wk_ref[...]).astype(BF16)
        vc_t = _pool_blocks_t(slab(1), wv_ref[...]).astype(BF16)
        c_end = (_lane_iota((N_HEADS, LANES)) + 1) * CMP_BLOCK - 1
        s_c = jnp.where(c_end <= q_pos, jnp.dot(qp, kc_t, preferred_element_type=F32), NEG_INF)
        p_c = jnp.where(c_end <= q_pos, jnp.exp2(s_c - jnp.max(s_c, axis=-1, keepdims=True)), 0.0)
        l_c = jnp.sum(p_c, axis=-1, keepdims=True)
        inv_c = jnp.where(l_c > 0.0, 1.0 / l_c, 0.0)
        o_cmp = _dot_nt(p_c.astype(BF16), vc_t) * inv_c
        p_cmp = p_c * inv_c
        psum = jnp.concatenate(
            [jnp.broadcast_to(jnp.sum(p_cmp[kvh * GROUP:(kvh + 1) * GROUP], axis=0, keepdims=True), (GROUP, LANES))
             for kvh in range(N_KV_HEADS)], axis=0)

        sel = _select_blocks(psum, q_pos, n_blk)
        selexp = jnp.dot(sel.astype(BF16), exp_ref[...], preferred_element_type=F32)
        new_blk = past_len // SEL_BLOCK
        ok_new = sel[:, 2 * new_blk:2 * new_blk + 1] > 0.5
        ok_past = (selexp > 0.5) & (_lane_iota((N_HEADS, past_len)) <= q_pos)
        o_sel = new_key_softmax(qp_f[g], jnp.dot(qp, slab(2), preferred_element_type=F32), ok_past, slab(3),
                                kv_new[:, 2 * KV_W:3 * KV_W], kv_new[:, 3 * KV_W:4 * KV_W], ok_new)

        g_cmp, g_sel, g_win = _gate_cols(gate_ref[g], range(N_HEADS))
        o = o_cmp * g_cmp + o_sel * g_sel + o_win[g] * g_win
        out_ref[g] = _place_heads([o[h:h + 1] for h in range(N_HEADS)])


def _attn_sample(page_table, q, kv_new, kvw_new, gate, w_pool_k, w_pool_v, cache_t, win_t):
    nb, n_pages = page_table.shape
    n_feat, page = cache_t.shape[1], cache_t.shape[2]
    w_buf = win_t.shape[2]
    past_len = n_pages * page
    assert nb % DEC_GROUP == 0
    row3 = lambda w: pl.BlockSpec((DEC_GROUP, 1, w), lambda b, pt: (b, 0, 0))
    full = lambda shape: pl.BlockSpec(shape, lambda b, pt: (0,) * len(shape))
    win_spec = pl.BlockSpec((DEC_GROUP, 2 * KV_W, w_buf), lambda b, pt: (b, 0, 0))
    grid_spec = pltpu.PrefetchScalarGridSpec(
        num_scalar_prefetch=1,
        grid=(nb // DEC_GROUP,),
        in_specs=[row3(N_HEADS * HEAD_DIM), row3(N_PAGED_SLOTS * KV_W), row3(2 * KV_W), row3(LANES),
                  full((past_len, LANES)), full((past_len, LANES)), full((2 * KV_W, nb)),
                  pl.BlockSpec(memory_space=pl.ANY), win_spec],
        out_specs=[row3(N_HEADS * HEAD_DIM), win_spec],
        scratch_shapes=[pltpu.VMEM((2, DEC_GROUP, n_feat, past_len), F32),
                        pltpu.VMEM((LANES, past_len), BF16),
                        pltpu.SemaphoreType.DMA((2, DEC_GROUP, n_pages))],
    )
    r3 = lambda a: a.reshape(nb, 1, a.shape[-1])
    return pl.pallas_call(
        _attn_sample_kernel,
        grid_spec=grid_spec,
        out_shape=[jax.ShapeDtypeStruct((nb, 1, N_HEADS * HEAD_DIM), F32),
                   jax.ShapeDtypeStruct(win_t.shape, F32)],
        compiler_params=_cparams("arbitrary"),
        name="attn_sample",
    )(page_table, r3(q), r3(kv_new), r3(kvw_new), r3(gate), w_pool_k, w_pool_v, kvw_new.T, cache_t, win_t)


FF_CHUNK = 1024


def _rms(x, g):
    return x * lax.rsqrt(jnp.mean(x * x, axis=-1, keepdims=True) + RMS_EPS) * g


def _mlp_kernel(x_ref, conv_ref, attn_ref, wout_ref, gm_ref, wup_ref, wdn_ref, gf_ref, y_ref, h_ref):
    h_ref[...] = (x_ref[...]
                  + jnp.dot(conv_ref[...].astype(BF16), wout_ref[0:C_CONV, :], preferred_element_type=F32)
                  + jnp.dot(attn_ref[...].astype(BF16), wout_ref[C_CONV:, :], preferred_element_type=F32))
    hn = _rms(h_ref[...], gm_ref[...]).astype(BF16)
    acc = None
    for c in range(D_FF // FF_CHUNK):
        m = jnp.dot(hn, wup_ref[:, c * FF_CHUNK:(c + 1) * FF_CHUNK], preferred_element_type=F32)
        m = jnp.maximum(m, 0.0)
        t = jnp.dot((m * m).astype(BF16), wdn_ref[c * FF_CHUNK:(c + 1) * FF_CHUNK, :], preferred_element_type=F32)
        acc = t if acc is None else acc + t
    y_ref[...] = _rms(h_ref[...] + acc, gf_ref[...])


def _mlp(x, conv_y, attn_y, w_out, g_mlp, w_up, w_down, g_final, tm):
    n = x.shape[0]
    row = lambda w: pl.BlockSpec((tm, w), lambda i: (i, 0))
    once = lambda shape: pl.BlockSpec(shape, lambda i: (0,) * len(shape), pipeline_mode=pl.Buffered(1))
    return pl.pallas_call(
        _mlp_kernel,
        grid=(n // tm,),
        in_specs=[row(D_MODEL), row(C_CONV), row(N_HEADS * HEAD_DIM), once((D_MODEL, D_MODEL)), once((1, D_MODEL)),
                  once((D_MODEL, D_FF)), once((D_FF, D_MODEL)), once((1, D_MODEL))],
        out_specs=row(D_MODEL),
        out_shape=jax.ShapeDtypeStruct((n, D_MODEL), F32),
        scratch_shapes=[pltpu.VMEM((tm, D_MODEL), F32)],
        compiler_params=_cparams("parallel"),
        name="mlp",
    )(x, conv_y, attn_y, w_out, g_mlp, w_up, w_down, g_final)


def kernel(x_prompt, x_sample, cache_kv, cache_win, state_conv, page_table, g_attn_norm, w_in, w_dw, b_dw,
           conv_ln_g, conv_ln_b, w_cmp_k, w_cmp_v, w_out, g_mlp_norm, w_up, w_down, g_final):
    depth = w_in.shape[0]
    assert depth == 1, "single-layer trunk"
    nb, seq, _ = x_prompt.shape
    db, dseq, _ = x_sample.shape
    assert dseq == 1
    n_phys, page = cache_kv.shape[1], cache_kv.shape[2]
    w_buf = cache_win.shape[2]

    w_main = jnp.concatenate([w_in[0][:, :COL_KV], w_in[0][:, COL_GATE:],
                              jnp.zeros((D_MODEL, IN_COLS_PAD - IN_COLS), F32)], axis=1).astype(BF16)
    w_kv = jnp.transpose(w_in[0][:, COL_KV:COL_GATE]).astype(BF16)
    w_out_b, w_up_b, w_down_b = w_out[0].astype(BF16), w_up[0].astype(BF16), w_down[0].astype(BF16)
    row = lambda a: a.reshape(1, -1)
    g_attn, g_mlp, g_fin = row(g_attn_norm[0]), row(g_mlp_norm[0]), row(g_final)
    conv_w = (w_dw[0], row(b_dw[0]), row(conv_ln_g[0]), row(conv_ln_b[0]))
    mlp_w = (w_out_b, g_mlp, w_up_b, w_down_b, g_fin)

    xp = x_prompt.reshape(nb * seq, D_MODEL)
    glu, q, gate, kv_t, kvw_t = _project(xp, g_attn, w_main, w_kv, 512, seq)
    per_b = lambda a: a.reshape(nb, seq, a.shape[-1])
    glu3 = per_b(glu)
    conv_y = _conv_prompt(glu3, *conv_w)
    attn_y = _attn_prompt(per_b(q), kv_t, kvw_t, per_b(gate), _pool_matrix(w_cmp_k[0], seq),
                          _pool_matrix(w_cmp_v[0], seq))
    y_prompt = _mlp(xp, conv_y.reshape(nb * seq, C_CONV), attn_y.reshape(nb * seq, -1), *mlp_w, 512)
    pos_major = lambda a, n_slots: jnp.transpose(
        a.reshape(a.shape[0], n_slots, N_KV_HEADS, HEAD_DIM, a.shape[-1]), (0, 4, 1, 2, 3))[None]
    new_kv_p = pos_major(kv_t, N_PAGED_SLOTS)
    new_win_p = pos_major(kvw_t[:, :, seq - min(WINDOW, seq):], 2)
    new_conv_p = glu3[:, seq - (CONV_WIDTH - 1):][None]

    xs = x_sample.reshape(db, D_MODEL)
    glu_s, q_s, gate_s, kv_s, kvw_s = _project(xs, g_attn, w_main, w_kv, db)
    conv_y_s, conv_st = _conv_sample(jnp.transpose(state_conv[0], (1, 0, 2)), glu_s, *conv_w)
    cache_t = jnp.transpose(cache_kv[0], (0, 2, 3, 4, 1)).reshape(n_phys, N_PAGED_SLOTS * KV_W, page)
    win_t = jnp.transpose(cache_win[0], (0, 2, 3, 4, 1)).reshape(db, 2 * KV_W, w_buf)
    n_past = page_table.shape[1] * page
    attn_s, new_win_t = _attn_sample(page_table, q_s, kv_s, kvw_s, gate_s, _pool_matrix(w_cmp_k[0], n_past),
                                     _pool_matrix(w_cmp_v[0], n_past), cache_t, win_t)
    new_win_s = pos_major(new_win_t, 2)
    y_sample = _mlp(xs, conv_y_s, attn_s.reshape(db, -1), *mlp_w, db)

    return (y_prompt.reshape(nb, seq, D_MODEL), y_sample.reshape(db, 1, D_MODEL), new_kv_p, new_win_p, new_conv_p,
            kv_s.reshape(1, db, 1, N_PAGED_SLOTS, N_KV_HEADS, HEAD_DIM), new_win_s,
            jnp.transpose(conv_st, (1, 0, 2))[None])
```

```python
import functools

import jax
import jax.numpy as jnp
from jax import lax
from jax.experimental import pallas as pl
from jax.experimental.pallas import tpu as pltpu

F32 = jnp.float32
BF16 = jnp.bfloat16

D_MODEL = 1024
C_CONV = 512
CONV_WIDTH = 31
N_HEADS = 8
HEAD_DIM = 64
N_KV_HEADS = 2
GROUP = N_HEADS // N_KV_HEADS
KV_W = N_KV_HEADS * HEAD_DIM
N_PAGED_SLOTS = 4
D_FF = 4 * D_MODEL
CMP_BLOCK = 32
SEL_BLOCK = 64
TOP_N = 16
N_LOCAL = 2
WINDOW = 512
FORCE_BONUS = float(GROUP + 1)
NEG_INF = -1e30
RMS_EPS = 1e-6
LN_EPS = 1e-5
ATTN_SCALE = HEAD_DIM ** -0.5
LOG2E = 1.4426950408889634

COL_Q = 2 * C_CONV
COL_KV = COL_Q + N_HEADS * HEAD_DIM
COL_WIN = COL_KV + N_PAGED_SLOTS * KV_W
COL_GATE = COL_WIN + 2 * KV_W
IN_COLS = COL_GATE + 3 * N_HEADS
LANES = 128
SUBLANES = 8
IN_COLS_PAD = -(-IN_COLS // LANES) * LANES

VMEM_LIMIT = 56 * 1024 * 1024


def _cparams(*sem):
    return pltpu.CompilerParams(dimension_semantics=sem, vmem_limit_bytes=VMEM_LIMIT)


def _const_spec(shape):
    return pl.BlockSpec(shape, lambda *_: (0,) * len(shape))


def _dot_nt(a, b):
    return lax.dot_general(a, b, (((1,), (1,)), ((), ())), preferred_element_type=F32)


N_KV_COLS = COL_GATE - COL_KV


def _proj_kernel(x_ref, g_ref, w_ref, wkv_ref, glu_ref, q_ref, gate_ref, kv_ref, kvw_ref, *, kv_t):
    x = x_ref[...]
    ms = jnp.mean(x * x, axis=-1, keepdims=True)
    xn = (x * lax.rsqrt(ms + RMS_EPS) * g_ref[...]).astype(BF16)
    z = jnp.dot(xn, w_ref[...], preferred_element_type=F32)
    glu_ref[...] = z[:, :C_CONV] * jax.nn.sigmoid(z[:, C_CONV:COL_Q])
    q_ref[...] = z[:, COL_Q:COL_KV]
    gate_ref[...] = jax.nn.sigmoid(z[:, COL_KV:])
    n_paged = N_PAGED_SLOTS * KV_W
    if kv_t:
        kv = _dot_nt(wkv_ref[...], xn)
        kv_ref[0] = kv[:n_paged]
        kvw_ref[0] = kv[n_paged:]
    else:
        kv = _dot_nt(xn, wkv_ref[...])
        kv_ref[...] = kv[:, :n_paged]
        kvw_ref[...] = kv[:, n_paged:]


def _project(x, g, w_main, w_kv, tm, seq=None):
    n = x.shape[0]
    row = lambda w: pl.BlockSpec((tm, w), lambda i: (i, 0))
    n_paged = N_PAGED_SLOTS * KV_W
    widths = (C_CONV, N_HEADS * HEAD_DIM, LANES)
    out_specs = [row(w) for w in widths]
    out_shape = [jax.ShapeDtypeStruct((n, w), F32) for w in widths]
    for w in (n_paged, 2 * KV_W):
        if seq is None:
            out_specs.append(row(w))
            out_shape.append(jax.ShapeDtypeStruct((n, w), F32))
        else:
            per_seq = seq // tm
            out_specs.append(pl.BlockSpec((1, w, tm), lambda i: (i // per_seq, 0, i % per_seq)))
            out_shape.append(jax.ShapeDtypeStruct((n // seq, w, seq), F32))
    return pl.pallas_call(
        functools.partial(_proj_kernel, kv_t=seq is not None),
        grid=(n // tm,),
        in_specs=[row(D_MODEL), _const_spec((1, D_MODEL)), _const_spec(w_main.shape), _const_spec(w_kv.shape)],
        out_specs=out_specs,
        out_shape=out_shape,
        compiler_params=_cparams("parallel"),
        name="projection",
    )(x, g, w_main, w_kv)


def _ln_swish(y, lng, lnb):
    mu = jnp.mean(y, axis=-1, keepdims=True)
    d = y - mu
    var = jnp.mean(d * d, axis=-1, keepdims=True)
    y = d * lax.rsqrt(var + LN_EPS) * lng + lnb
    return y * jax.nn.sigmoid(y)


CONV_ROWS = 64
CONV_PAD = 32


def _conv_prompt_kernel(glu_ref, w_ref, b_ref, lng_ref, lnb_ref, y_ref, ext_ref):
    seq = glu_ref.shape[1]
    ext_ref[0:CONV_PAD, :] = jnp.zeros((CONV_PAD, C_CONV), F32)
    ext_ref[CONV_PAD:, :] = glu_ref[0]
    shift = CONV_PAD - (CONV_WIDTH - 1)

    def body(c, carry):
        base = pl.multiple_of(c * CONV_ROWS, CONV_ROWS)
        cols = []
        for ct in range(C_CONV // LANES):
            lanes = slice(ct * LANES, (ct + 1) * LANES)
            win = ext_ref[pl.ds(base, CONV_ROWS + CONV_PAD), lanes]
            acc = jnp.zeros((CONV_ROWS, LANES), F32)
            for r in range(SUBLANES):
                shifted = win if r == 0 else pltpu.roll(win, CONV_ROWS + CONV_PAD - r, 0)
                for j in range(CONV_WIDTH):
                    if (j + shift) % SUBLANES == r:
                        a = j + shift - r
                        acc = acc + w_ref[j:j + 1, lanes] * shifted[a:a + CONV_ROWS]
            cols.append(acc)
        y = jnp.concatenate(cols, axis=1) + b_ref[...]
        y_ref[0, pl.ds(base, CONV_ROWS), :] = _ln_swish(y, lng_ref[...], lnb_ref[...])
        return carry

    lax.fori_loop(0, seq // CONV_ROWS, body, 0)


def _conv_prompt(glu, w_dw, b_dw, lng, lnb):
    nb, seq, _ = glu.shape
    blk = pl.BlockSpec((1, seq, C_CONV), lambda b: (b, 0, 0))
    return pl.pallas_call(
        _conv_prompt_kernel,
        grid=(nb,),
        in_specs=[blk, _const_spec((CONV_WIDTH, C_CONV))] + [_const_spec((1, C_CONV))] * 3,
        out_specs=blk,
        out_shape=jax.ShapeDtypeStruct(glu.shape, F32),
        scratch_shapes=[pltpu.VMEM((seq + CONV_PAD, C_CONV), F32)],
        compiler_params=_cparams("parallel"),
        name="conv_prompt",
    )(glu, w_dw, b_dw, lng, lnb)


def _conv_sample_kernel(st_ref, glu_ref, w_ref, b_ref, lng_ref, lnb_ref, y_ref, st_out_ref):
    n_prev = CONV_WIDTH - 1
    glu = glu_ref[...]
    acc = w_ref[n_prev:n_prev + 1, :] * glu
    for j in range(n_prev):
        acc = acc + w_ref[j:j + 1, :] * st_ref[j]
    y_ref[...] = _ln_swish(acc + b_ref[...], lng_ref[...], lnb_ref[...])
    st_out_ref[0:n_prev - 1] = st_ref[1:n_prev]
    st_out_ref[n_prev - 1] = glu


def _conv_sample(state_t, glu, w_dw, b_dw, lng, lnb):
    nb = glu.shape[0]
    return pl.pallas_call(
        _conv_sample_kernel,
        grid=(1,),
        in_specs=[_const_spec(state_t.shape), _const_spec((nb, C_CONV)), _const_spec((CONV_WIDTH, C_CONV))]
        + [_const_spec((1, C_CONV))] * 3,
        out_specs=[_const_spec((nb, C_CONV)), _const_spec(state_t.shape)],
        out_shape=[jax.ShapeDtypeStruct((nb, C_CONV), F32), jax.ShapeDtypeStruct(state_t.shape, F32)],
        compiler_params=_cparams("arbitrary"),
        name="conv_sample",
    )(state_t, glu, w_dw, b_dw, lng, lnb)


def _lane_iota(shape):
    return lax.broadcasted_iota(jnp.int32, shape, len(shape) - 1)


def _head_rows(q, kvh):
    rows = q.shape[0]
    half = _lane_iota((rows, KV_W)) // HEAD_DIM
    out = []
    for g in range(GROUP):
        h = kvh * GROUP + g
        slab = q[:, KV_W * (h // 2):KV_W * (h // 2 + 1)]
        if h % 2 != kvh:
            slab = pltpu.roll(slab, HEAD_DIM, 1)
        out.append(jnp.where(half == kvh, slab * (ATTN_SCALE * LOG2E), 0.0))
    return out


def _masked_softmax_pv(s, ok, v):
    s = jnp.where(ok, s, NEG_INF)
    m = jnp.max(s, axis=-1, keepdims=True)
    p = jnp.where(ok, jnp.exp2(s - m), 0.0)
    l = jnp.sum(p, axis=-1, keepdims=True)
    return jnp.dot(p.astype(BF16), v, preferred_element_type=F32), jnp.where(l > 0.0, 1.0 / l, 0.0)


def _select_blocks(psum, q_pos, n_blk):
    shape = psum.shape
    lane = _lane_iota(shape)
    even = (lane & 1) == 0
    pair = psum + jnp.where(even, pltpu.roll(psum, LANES - 1, 1), pltpu.roll(psum, 1, 1))
    blk = lane // (SEL_BLOCK // CMP_BLOCK)
    back = q_pos // SEL_BLOCK - blk
    forced = (blk == 0) | ((back >= 0) & (back < N_LOCAL))
    score = jnp.where(back >= 0, pair + jnp.where(forced, FORCE_BONUS, 0.0), -jnp.inf)
    rank = jnp.zeros(shape, jnp.int32)
    for i in range(n_blk):
        si = score[:, 2 * i:2 * i + 1]
        beats = (si > score) | ((si == score) & (i < blk))
        rank = rank + beats.astype(jnp.int32)
    return jnp.where((rank < min(TOP_N, n_blk)) & even, 1.0, 0.0)


def _block_expander(n_keys, row_stride):
    li = lax.broadcasted_iota(jnp.int32, (LANES, n_keys), 0)
    ti = lax.broadcasted_iota(jnp.int32, (LANES, n_keys), 1)
    return jnp.where(li == row_stride * (ti // SEL_BLOCK), 1.0, 0.0).astype(BF16)


def _gate_cols(gates, heads):
    return [jnp.concatenate([gates[:, 3 * h + k:3 * h + k + 1] for h in heads], axis=0) for k in range(3)]


def _place_heads(o_heads):
    rows = o_heads[0].shape[0]
    low = _lane_iota((rows, KV_W)) < HEAD_DIM
    slabs = []
    for s in range(N_HEADS // 2):
        pair = []
        for h in (2 * s, 2 * s + 1):
            o = o_heads[h]
            if h // GROUP != h % 2:
                o = pltpu.roll(o, HEAD_DIM, 1)
            pair.append(o)
        slabs.append(jnp.where(low, pair[0], pair[1]))
    return jnp.concatenate(slabs, axis=1)


QB = 256
KT = 256
WIN_KEYS = WINDOW + QB


def _select_bias_t(p_blk, q_pos):
    n_blk = p_blk.shape[0]
    blk = lax.broadcasted_iota(jnp.int32, p_blk.shape, 0)
    back = q_pos // SEL_BLOCK - blk
    forced = (blk == 0) | ((back >= 0) & (back < N_LOCAL))
    score = jnp.where(back >= 0, p_blk + jnp.where(forced, FORCE_BONUS, 0.0), -jnp.inf)
    rank = jnp.zeros(score.shape, jnp.int32)
    for i in range(n_blk):
        si = score[i:i + 1, :]
        beats = (si > score) | ((si == score) & (i < blk))
        rank = rank + beats.astype(jnp.int32)
    return jnp.where((rank < min(TOP_N, n_blk)) & (back >= 0), 0.0, NEG_INF)


WIN_CHUNK = 128


def _attn_prompt_kernel(q_ref, kv_ref, kvw_ref, gate_ref, wk_ref, wv_ref, out_ref,
                        kc_ref, vc_ref, ks_ref, vs_ref, kw_ref, vw_ref, exp_ref,
                        bias_ref, s_ref, mrun_ref, mb_ref, lrun_ref, acc_ref, psum_ref):
    seq = kv_ref.shape[2]
    n_blk = seq // SEL_BLOCK
    n_kt = seq // KT
    n_wc = WIN_KEYS // WIN_CHUNK
    rows = GROUP * QB
    qi = pl.program_id(1)
    qs = qi * QB

    @pl.when((pl.program_id(0) == 0) & (qi == 0))
    def _():
        exp_ref[...] = _block_expander(seq, 1)

    @pl.when(qi == 0)
    def _():
        kc_ref[...] = _pool_blocks_t(kv_ref[0, 0:KV_W, :].astype(BF16), wk_ref[...]).T.astype(BF16)
        vc_ref[...] = _pool_blocks_t(kv_ref[0, KV_W:2 * KV_W, :].astype(BF16), wv_ref[...]).T.astype(BF16)
        for t in range(n_kt):
            ks_ref[t] = kv_ref[0, 2 * KV_W:3 * KV_W, t * KT:(t + 1) * KT].astype(BF16)
            vs_ref[t] = kv_ref[0, 3 * KV_W:4 * KV_W, t * KT:(t + 1) * KT].astype(BF16)
        for c in range(seq // WIN_CHUNK):
            kw_ref[c] = kvw_ref[0, 0:KV_W, c * WIN_CHUNK:(c + 1) * WIN_CHUNK].astype(BF16)
            vw_ref[c] = kvw_ref[0, KV_W:2 * KV_W, c * WIN_CHUNK:(c + 1) * WIN_CHUNK].astype(BF16)

    q = q_ref[0]
    q_pos = qs + lax.broadcasted_iota(jnp.int32, (QB, 1), 0)
    q_pos_t = qs + lax.broadcasted_iota(jnp.int32, (1, QB), 1)
    rep = lambda a: jnp.concatenate([a] * GROUP, axis=0)
    add_bias = lambda s, b: (s.reshape(GROUP, QB, s.shape[-1]) + b[None]).reshape(s.shape)
    n_tiles = (qs + QB + KT - 1) // KT
    qp = [jnp.concatenate(_head_rows(q, kvh), axis=0).astype(BF16) for kvh in range(N_KV_HEADS)]

    o_cmp = []
    c_end = (_lane_iota((rows, LANES)) + 1) * CMP_BLOCK - 1
    c_end_t = (lax.broadcasted_iota(jnp.int32, (LANES, rows), 0) + 1) * CMP_BLOCK - 1
    last = n_tiles - 1
    diag_bias = jnp.where(last * KT + _lane_iota((QB, KT)) <= q_pos, 0.0, NEG_INF)
    for kvh in range(N_KV_HEADS):
        o_cmp.append(_masked_softmax_pv(_dot_nt(qp[kvh], kc_ref[...]), c_end <= rep(q_pos), vc_ref[...]))
        s_t = _dot_nt(kc_ref[...], qp[kvh])
        ok_t = c_end_t <= jnp.concatenate([q_pos_t] * GROUP, axis=1)
        s_t = jnp.where(ok_t, s_t, NEG_INF)
        p_t = jnp.where(ok_t, jnp.exp2(s_t - jnp.max(s_t, axis=0, keepdims=True)), 0.0)
        l_t = jnp.sum(p_t, axis=0, keepdims=True)
        p_t = p_t * jnp.where(l_t > 0.0, 1.0 / l_t, 0.0)
        psum = sum(p_t[:, g * QB:(g + 1) * QB] for g in range(GROUP))
        per_sel = SEL_BLOCK // CMP_BLOCK
        p_blk = []
        for c in range(QB // LANES):
            psum_ref[c] = psum[:, c * LANES:(c + 1) * LANES]
            p_blk.append(sum(psum_ref[c, pl.ds(j, n_blk, stride=per_sel), :] for j in range(per_sel)))
        p_blk = jnp.concatenate(p_blk, axis=1)
        blk_bias = _select_bias_t(p_blk, q_pos_t)
        blk_bias = jnp.concatenate([blk_bias, jnp.zeros((LANES - n_blk, QB), F32)], axis=0).T
        key_bias = jnp.dot(blk_bias.astype(BF16), exp_ref[...], preferred_element_type=F32)
        for t in range(n_kt):
            @pl.when(t < n_tiles)
            def _():
                bias_ref[kvh, t] = key_bias[:, t * KT:(t + 1) * KT]
        bias_ref[kvh, last] += diag_bias

    mrun_ref[...] = jnp.full(mrun_ref.shape, NEG_INF, F32)

    def max_pass(t, carry):
        k_t = ks_ref[t]
        for kvh in range(N_KV_HEADS):
            s = add_bias(jnp.dot(qp[kvh], k_t, preferred_element_type=F32), bias_ref[kvh, t])
            s_ref[kvh, t] = s
            mrun_ref[kvh] = jnp.maximum(mrun_ref[kvh], jnp.maximum(s[:, :LANES], s[:, LANES:]))
        return carry

    lax.fori_loop(0, n_tiles, max_pass, 0)
    for kvh in range(N_KV_HEADS):
        mb_ref[kvh] = jnp.broadcast_to(jnp.max(mrun_ref[kvh], axis=-1, keepdims=True), (rows, LANES))
    lrun_ref[...] = jnp.zeros(lrun_ref.shape, F32)
    acc_ref[...] = jnp.zeros(acc_ref.shape, F32)

    def sum_pass(t, carry):
        v_t = vs_ref[t]
        for kvh in range(N_KV_HEADS):
            s = s_ref[kvh, t]
            mb = mb_ref[kvh]
            p = jnp.exp2(jnp.concatenate([s[:, :LANES] - mb, s[:, LANES:] - mb], axis=1))
            lrun_ref[kvh] += p[:, :LANES] + p[:, LANES:]
            acc_ref[kvh] += _dot_nt(p.astype(BF16), v_t)
        return carry

    lax.fori_loop(0, n_tiles, sum_pass, 0)

    win_c0 = jnp.maximum(qs - WINDOW, 0) // WIN_CHUNK
    rel = q_pos - (win_c0 * WIN_CHUNK + _lane_iota((QB, WIN_KEYS)))
    win_bias = jnp.where((rel >= 0) & (rel <= WINDOW), 0.0, NEG_INF)

    o_heads = []
    for kvh in range(N_KV_HEADS):
        g_cmp, g_sel, g_win = _gate_cols(gate_ref[0], [kvh * GROUP + g for g in range(GROUP)])
        l = jnp.sum(lrun_ref[kvh], axis=-1, keepdims=True)
        s_w = jnp.concatenate([jnp.dot(qp[kvh], kw_ref[win_c0 + c], preferred_element_type=F32) for c in range(n_wc)],
                              axis=1)
        s_w = add_bias(s_w, win_bias)
        p_w = jnp.exp2(s_w - jnp.max(s_w, axis=-1, keepdims=True))
        o_win = sum(_dot_nt(p_w[:, c * WIN_CHUNK:(c + 1) * WIN_CHUNK].astype(BF16), vw_ref[win_c0 + c])
                    for c in range(n_wc))
        o_c, inv_c = o_cmp[kvh]
        o = (o_c * (inv_c * g_cmp) + acc_ref[kvh] * (jnp.where(l > 0.0, 1.0 / l, 0.0) * g_sel)
             + o_win * (g_win / jnp.sum(p_w, axis=-1, keepdims=True)))
        o_heads += [o[g * QB:(g + 1) * QB] for g in range(GROUP)]

    out_ref[0] = _place_heads(o_heads)


def _attn_prompt(q, kv_t, kvw_t, gate, w_pool_k, w_pool_v):
    nb, seq, _ = q.shape
    per_q = lambda w: pl.BlockSpec((1, QB, w), lambda b, i: (b, i, 0))
    per_b = lambda w: pl.BlockSpec((1, w, seq), lambda b, i: (b, 0, 0))
    wspec = pl.BlockSpec((seq, LANES), lambda b, i: (0, 0))
    rows = GROUP * QB
    return pl.pallas_call(
        _attn_prompt_kernel,
        grid=(nb, seq // QB),
        in_specs=[per_q(N_HEADS * HEAD_DIM), per_b(N_PAGED_SLOTS * KV_W), per_b(2 * KV_W), per_q(LANES), wspec, wspec],
        out_specs=per_q(N_HEADS * HEAD_DIM),
        out_shape=jax.ShapeDtypeStruct(q.shape, F32),
        scratch_shapes=[pltpu.VMEM((LANES, KV_W), BF16), pltpu.VMEM((LANES, KV_W), BF16)]
        + [pltpu.VMEM((seq // KT, KV_W, KT), BF16)] * 2
        + [pltpu.VMEM((seq // WIN_CHUNK, KV_W, WIN_CHUNK), BF16)] * 2
        + [pltpu.VMEM((LANES, seq), BF16), pltpu.VMEM((N_KV_HEADS, seq // KT, QB, KT), F32),
           pltpu.VMEM((N_KV_HEADS, seq // KT, rows, KT), F32)]
        + [pltpu.VMEM((N_KV_HEADS, rows, LANES), F32)] * 4
        + [pltpu.VMEM((QB // LANES, LANES, LANES), F32)],
        compiler_params=_cparams("arbitrary", "arbitrary"),
        name="attn_prompt",
    )(q, kv_t, kvw_t, gate, w_pool_k, w_pool_v)


def _pool_matrix(w_cmp, n_keys):
    eye = jnp.eye(n_keys // CMP_BLOCK, LANES // N_KV_HEADS, dtype=F32)
    per_head = [(eye[:, None, :] * w_cmp[None, :, h, None]).reshape(n_keys, -1) for h in range(N_KV_HEADS)]
    return jnp.concatenate(per_head, axis=1).astype(BF16)


def _pool_blocks_t(x_t, w_pool):
    res = jnp.dot(x_t, w_pool, preferred_element_type=F32)
    row = lax.broadcasted_iota(jnp.int32, res.shape, 0)
    own = jnp.where(row < HEAD_DIM, res, pltpu.roll(res, HEAD_DIM, 1))
    return jnp.where(_lane_iota(res.shape) < w_pool.shape[0] // CMP_BLOCK, own, 0.0)


DEC_GROUP = 4


def _attn_sample_kernel(pt_ref, q_ref, kvn_ref, kvwn_ref, gate_ref, wk_ref, wv_ref, kvwt_ref, cache_ref, win_ref,
                        out_ref, win_out_ref, past_ref, exp_ref, sem):
    step = pl.program_id(0)
    n_steps = pl.num_programs(0)
    n_pages = pt_ref.shape[1]
    page = cache_ref.shape[2]
    past_len = n_pages * page
    n_blk = past_len // SEL_BLOCK + 1
    w_buf = win_ref.shape[2]
    slot = step % 2

    def page_copy(st, sl, g, p):
        return pltpu.make_async_copy(cache_ref.at[pt_ref[st * DEC_GROUP + g, p]],
                                     past_ref.at[sl, g, :, pl.ds(p * page, page)], sem.at[sl, g, p])

    def start_pages(st, sl):
        for g in range(DEC_GROUP):
            for p in range(n_pages):
                page_copy(st, sl, g, p).start()

    @pl.when(step == 0)
    def _():
        start_pages(0, 0)
        exp_ref[...] = _block_expander(past_len, SEL_BLOCK // CMP_BLOCK)

    @pl.when(step + 1 < n_steps)
    def _():
        start_pages(step + 1, 1 - slot)

    q_pos = jnp.full((N_HEADS, 1), past_len, jnp.int32)

    def new_key_softmax(qp_f, s_past, ok_past, vt_past, k_new, v_new, ok_new):
        s_new = jnp.sum(qp_f * k_new, axis=-1, keepdims=True)
        s_past = jnp.where(ok_past, s_past, NEG_INF)
        s_new = jnp.where(ok_new, s_new, NEG_INF)
        m = jnp.maximum(jnp.max(s_past, axis=-1, keepdims=True), s_new)
        p_past = jnp.where(ok_past, jnp.exp2(s_past - m), 0.0)
        p_new = jnp.where(ok_new, jnp.exp2(s_new - m), 0.0)
        l = jnp.sum(p_past, axis=-1, keepdims=True) + p_new
        o = _dot_nt(p_past.astype(BF16), vt_past) + p_new * v_new
        return o * jnp.where(l > 0.0, 1.0 / l, 0.0)

    qp_f, o_win = [], []
    for g in range(DEC_GROUP):
        b = step * DEC_GROUP + g
        qf = jnp.concatenate([r for kvh in range(N_KV_HEADS) for r in _head_rows(q_ref[g], kvh)], axis=0)
        qp_f.append(qf)
        kvw_new = kvwn_ref[g]
        win = win_ref[g]
        rel = past_len - (past_len - w_buf + _lane_iota((N_HEADS, w_buf)))
        o_win.append(new_key_softmax(qf, jnp.dot(qf.astype(BF16), win[0:KV_W].astype(BF16),
                                                 preferred_element_type=F32),
                                     (rel >= 0) & (rel <= WINDOW), win[KV_W:2 * KV_W].astype(BF16),
                                     kvw_new[:, 0:KV_W], kvw_new[:, KV_W:2 * KV_W], jnp.full((N_HEADS, 1), True)))
        new_col = jnp.sum(jnp.where(_lane_iota(kvwt_ref.shape) == b, kvwt_ref[...], 0.0), axis=1, keepdims=True)
        win_out_ref[g] = jnp.where(_lane_iota(win.shape) == w_buf - 1, new_col, pltpu.roll(win, w_buf - 1, 1))

    for g in range(DEC_GROUP):
        for p in range(n_pages):
            page_copy(step, slot, g, p).wait()

    for g in range(DEC_GROUP):
        def slab(i):
            return past_ref[slot, g, i * KV_W:(i + 1) * KV_W, :].astype(BF16)

        qp = qp_f[g].astype(BF16)
        kv_new = kvn_ref[g]

        kc_t = _pool_blocks_t(slab(0), wk_ref[...]).astype(BF16)
        vc_t = _pool_blocks_t(slab(1), wv_ref[...]).astype(BF16)
        c_end = (_lane_iota((N_HEADS, LANES)) + 1) * CMP_BLOCK - 1
        s_c = jnp.where(c_end <= q_pos, jnp.dot(qp, kc_t, preferred_element_type=F32), NEG_INF)
        p_c = jnp.where(c_end <= q_pos, jnp.exp2(s_c - jnp.max(s_c, axis=-1, keepdims=True)), 0.0)
        l_c = jnp.sum(p_c, axis=-1, keepdims=True)
        inv_c = jnp.where(l_c > 0.0, 1.0 / l_c, 0.0)
        o_cmp = _dot_nt(p_c.astype(BF16), vc_t) * inv_c
        p_cmp = p_c * inv_c
        psum = jnp.concatenate(
            [jnp.broadcast_to(jnp.sum(p_cmp[kvh * GROUP:(kvh + 1) * GROUP], axis=0, keepdims=True), (GROUP, LANES))
             for kvh in range(N_KV_HEADS)], axis=0)

        sel = _select_blocks(psum, q_pos, n_blk)
        selexp = jnp.dot(sel.astype(BF16), exp_ref[...], preferred_element_type=F32)
        new_blk = past_len // SEL_BLOCK
        ok_new = sel[:, 2 * new_blk:2 * new_blk + 1] > 0.5
        ok_past = (selexp > 0.5) & (_lane_iota((N_HEADS, past_len)) <= q_pos)
        o_sel = new_key_softmax(qp_f[g], jnp.dot(qp, slab(2), preferred_element_type=F32), ok_past, slab(3),
                                kv_new[:, 2 * KV_W:3 * KV_W], kv_new[:, 3 * KV_W:4 * KV_W], ok_new)

        g_cmp, g_sel, g_win = _gate_cols(gate_ref[g], range(N_HEADS))
        o = o_cmp * g_cmp + o_sel * g_sel + o_win[g] * g_win
        out_ref[g] = _place_heads([o[h:h + 1] for h in range(N_HEADS)])


def _attn_sample(page_table, q, kv_new, kvw_new, gate, w_pool_k, w_pool_v, cache_t, win_t):
    nb, n_pages = page_table.shape
    n_feat, page = cache_t.shape[1], cache_t.shape[2]
    w_buf = win_t.shape[2]
    past_len = n_pages * page
    assert nb % DEC_GROUP == 0
    row3 = lambda w: pl.BlockSpec((DEC_GROUP, 1, w), lambda b, pt: (b, 0, 0))
    full = lambda shape: pl.BlockSpec(shape, lambda b, pt: (0,) * len(shape))
    win_spec = pl.BlockSpec((DEC_GROUP, 2 * KV_W, w_buf), lambda b, pt: (b, 0, 0))
    grid_spec = pltpu.PrefetchScalarGridSpec(
        num_scalar_prefetch=1,
        grid=(nb // DEC_GROUP,),
        in_specs=[row3(N_HEADS * HEAD_DIM), row3(N_PAGED_SLOTS * KV_W), row3(2 * KV_W), row3(LANES),
                  full((past_len, LANES)), full((past_len, LANES)), full((2 * KV_W, nb)),
                  pl.BlockSpec(memory_space=pl.ANY), win_spec],
        out_specs=[row3(N_HEADS * HEAD_DIM), win_spec],
        scratch_shapes=[pltpu.VMEM((2, DEC_GROUP, n_feat, past_len), F32),
                        pltpu.VMEM((LANES, past_len), BF16),
                        pltpu.SemaphoreType.DMA((2, DEC_GROUP, n_pages))],
    )
    r3 = lambda a: a.reshape(nb, 1, a.shape[-1])
    return pl.pallas_call(
        _attn_sample_kernel,
        grid_spec=grid_spec,
        out_shape=[jax.ShapeDtypeStruct((nb, 1, N_HEADS * HEAD_DIM), F32),
                   jax.ShapeDtypeStruct(win_t.shape, F32)],
        compiler_params=_cparams("arbitrary"),
        name="attn_sample",
    )(page_table, r3(q), r3(kv_new), r3(kvw_new), r3(gate), w_pool_k, w_pool_v, kvw_new.T, cache_t, win_t)


FF_CHUNK = 1024


def _rms(x, g):
    return x * lax.rsqrt(jnp.mean(x * x, axis=-1, keepdims=True) + RMS_EPS) * g


def _mlp_kernel(x_ref, conv_ref, attn_ref, wout_ref, gm_ref, wup_ref, wdn_ref, gf_ref, y_ref, h_ref):
    h_ref[...] = (x_ref[...]
                  + jnp.dot(conv_ref[...].astype(BF16), wout_ref[0:C_CONV, :], preferred_element_type=F32)
                  + jnp.dot(attn_ref[...].astype(BF16), wout_ref[C_CONV:, :], preferred_element_type=F32))
    hn = _rms(h_ref[...], gm_ref[...]).astype(BF16)
    acc = None
    for c in range(D_FF // FF_CHUNK):
        m = jnp.dot(hn, wup_ref[:, c * FF_CHUNK:(c + 1) * FF_CHUNK], preferred_element_type=F32)
        m = jnp.maximum(m, 0.0)
        t = jnp.dot((m * m).astype(BF16), wdn_ref[c * FF_CHUNK:(c + 1) * FF_CHUNK, :], preferred_element_type=F32)
        acc = t if acc is None else acc + t
    y_ref[...] = _rms(h_ref[...] + acc, gf_ref[...])


def _mlp(x, conv_y, attn_y, w_out, g_mlp, w_up, w_down, g_final, tm):
    n = x.shape[0]
    row = lambda w: pl.BlockSpec((tm, w), lambda i: (i, 0))
    once = lambda shape: pl.BlockSpec(shape, lambda i: (0,) * len(shape), pipeline_mode=pl.Buffered(1))
    return pl.pallas_call(
        _mlp_kernel,
        grid=(n // tm,),
        in_specs=[row(D_MODEL), row(C_CONV), row(N_HEADS * HEAD_DIM), once((D_MODEL, D_MODEL)), once((1, D_MODEL)),
                  once((D_MODEL, D_FF)), once((D_FF, D_MODEL)), once((1, D_MODEL))],
        out_specs=row(D_MODEL),
        out_shape=jax.ShapeDtypeStruct((n, D_MODEL), F32),
        scratch_shapes=[pltpu.VMEM((tm, D_MODEL), F32)],
        compiler_params=_cparams("parallel"),
        name="mlp",
    )(x, conv_y, attn_y, w_out, g_mlp, w_up, w_down, g_final)


def kernel(x_prompt, x_sample, cache_kv, cache_win, state_conv, page_table, g_attn_norm, w_in, w_dw, b_dw,
           conv_ln_g, conv_ln_b, w_cmp_k, w_cmp_v, w_out, g_mlp_norm, w_up, w_down, g_final):
    depth = w_in.shape[0]
    assert depth == 1, "single-layer trunk"
    nb, seq, _ = x_prompt.shape
    db, dseq, _ = x_sample.shape
    assert dseq == 1
    n_phys, page = cache_kv.shape[1], cache_kv.shape[2]
    w_buf = cache_win.shape[2]

    w_main = jnp.concatenate([w_in[0][:, :COL_KV], w_in[0][:, COL_GATE:],
                              jnp.zeros((D_MODEL, IN_COLS_PAD - IN_COLS), F32)], axis=1).astype(BF16)
    w_kv = jnp.transpose(w_in[0][:, COL_KV:COL_GATE]).astype(BF16)
    w_out_b, w_up_b, w_down_b = w_out[0].astype(BF16), w_up[0].astype(BF16), w_down[0].astype(BF16)
    row = lambda a: a.reshape(1, -1)
    g_attn, g_mlp, g_fin = row(g_attn_norm[0]), row(g_mlp_norm[0]), row(g_final)
    conv_w = (w_dw[0], row(b_dw[0]), row(conv_ln_g[0]), row(conv_ln_b[0]))
    mlp_w = (w_out_b, g_mlp, w_up_b, w_down_b, g_fin)

    xp = x_prompt.reshape(nb * seq, D_MODEL)
    glu, q, gate, kv_t, kvw_t = _project(xp, g_attn, w_main, w_kv, 512, seq)
    per_b = lambda a: a.reshape(nb, seq, a.shape[-1])
    glu3 = per_b(glu)
    conv_y = _conv_prompt(glu3, *conv_w)
    attn_y = _attn_prompt(per_b(q), kv_t, kvw_t, per_b(gate), _pool_matrix(w_cmp_k[0], seq),
                          _pool_matrix(w_cmp_v[0], seq))
    y_prompt = _mlp(xp, conv_y.reshape(nb * seq, C_CONV), attn_y.reshape(nb * seq, -1), *mlp_w, 512)
    pos_major = lambda a, n_slots: jnp.transpose(
        a.reshape(a.shape[0], n_slots, N_KV_HEADS, HEAD_DIM, a.shape[-1]), (0, 4, 1, 2, 3))[None]
    new_kv_p = pos_major(kv_t, N_PAGED_SLOTS)
    new_win_p = pos_major(kvw_t[:, :, seq - min(WINDOW, seq):], 2)
    new_conv_p = glu3[:, seq - (CONV_WIDTH - 1):][None]

    xs = x_sample.reshape(db, D_MODEL)
    glu_s, q_s, gate_s, kv_s, kvw_s = _project(xs, g_attn, w_main, w_kv, db)
    conv_y_s, conv_st = _conv_sample(jnp.transpose(state_conv[0], (1, 0, 2)), glu_s, *conv_w)
    cache_t = jnp.transpose(cache_kv[0], (0, 2, 3, 4, 1)).reshape(n_phys, N_PAGED_SLOTS * KV_W, page)
    win_t = jnp.transpose(cache_win[0], (0, 2, 3, 4, 1)).reshape(db, 2 * KV_W, w_buf)
    n_past = page_table.shape[1] * page
    attn_s, new_win_t = _attn_sample(page_table, q_s, kv_s, kvw_s, gate_s, _pool_matrix(w_cmp_k[0], n_past),
                                     _pool_matrix(w_cmp_v[0], n_past), cache_t, win_t)
    new_win_s = pos_major(new_win_t, 2)
    y_sample = _mlp(xs, conv_y_s, attn_s.reshape(db, -1), *mlp_w, db)

    return (y_prompt.reshape(nb, seq, D_MODEL), y_sample.reshape(db, 1, D_MODEL), new_kv_p, new_win_p, new_conv_p,
            kv_s.reshape(1, db, 1, N_PAGED_SLOTS, N_KV_HEADS, HEAD_DIM), new_win_s,
            jnp.transpose(conv_st, (1, 0, 2))[None])
```

```python
import functools

import jax
import jax.numpy as jnp
from jax import lax
from jax.experimental import pallas as pl
from jax.experimental.pallas import tpu as pltpu

F32 = jnp.float32
BF16 = jnp.bfloat16

D_MODEL = 1024
C_CONV = 512
CONV_WIDTH = 31
N_HEADS = 8
HEAD_DIM = 64
N_KV_HEADS = 2
GROUP = N_HEADS // N_KV_HEADS
KV_W = N_KV_HEADS * HEAD_DIM
N_PAGED_SLOTS = 4
D_FF = 4 * D_MODEL
CMP_BLOCK = 32
SEL_BLOCK = 64
TOP_N = 16
N_LOCAL = 2
WINDOW = 512
FORCE_BONUS = float(GROUP + 1)
NEG_INF = -1e30
RMS_EPS = 1e-6
LN_EPS = 1e-5
ATTN_SCALE = HEAD_DIM ** -0.5
LOG2E = 1.4426950408889634

COL_Q = 2 * C_CONV
COL_KV = COL_Q + N_HEADS * HEAD_DIM
COL_WIN = COL_KV + N_PAGED_SLOTS * KV_W
COL_GATE = COL_WIN + 2 * KV_W
IN_COLS = COL_GATE + 3 * N_HEADS
LANES = 128
SUBLANES = 8
IN_COLS_PAD = -(-IN_COLS // LANES) * LANES

VMEM_LIMIT = 56 * 1024 * 1024


def _cparams(*sem):
    return pltpu.CompilerParams(dimension_semantics=sem, vmem_limit_bytes=VMEM_LIMIT)


def _const_spec(shape):
    return pl.BlockSpec(shape, lambda *_: (0,) * len(shape))


def _dot_nt(a, b):
    return lax.dot_general(a, b, (((1,), (1,)), ((), ())), preferred_element_type=F32)


N_KV_COLS = COL_GATE - COL_KV


def _proj_kernel(x_ref, g_ref, w_ref, wkv_ref, glu_ref, q_ref, gate_ref, kv_ref, kvw_ref, *, kv_t):
    x = x_ref[...]
    ms = jnp.mean(x * x, axis=-1, keepdims=True)
    xn = (x * lax.rsqrt(ms + RMS_EPS) * g_ref[...]).astype(BF16)
    z = jnp.dot(xn, w_ref[...], preferred_element_type=F32)
    glu_ref[...] = z[:, :C_CONV] * jax.nn.sigmoid(z[:, C_CONV:COL_Q])
    q_ref[...] = z[:, COL_Q:COL_KV]
    gate_ref[...] = jax.nn.sigmoid(z[:, COL_KV:])
    n_paged = N_PAGED_SLOTS * KV_W
    if kv_t:
        kv = _dot_nt(wkv_ref[...], xn)
        kv_ref[0] = kv[:n_paged]
        kvw_ref[0] = kv[n_paged:]
    else:
        kv = _dot_nt(xn, wkv_ref[...])
        kv_ref[...] = kv[:, :n_paged]
        kvw_ref[...] = kv[:, n_paged:]


def _project(x, g, w_main, w_kv, tm, seq=None):
    n = x.shape[0]
    row = lambda w: pl.BlockSpec((tm, w), lambda i: (i, 0))
    n_paged = N_PAGED_SLOTS * KV_W
    widths = (C_CONV, N_HEADS * HEAD_DIM, LANES)
    out_specs = [row(w) for w in widths]
    out_shape = [jax.ShapeDtypeStruct((n, w), F32) for w in widths]
    for w in (n_paged, 2 * KV_W):
        if seq is None:
            out_specs.append(row(w))
            out_shape.append(jax.ShapeDtypeStruct((n, w), F32))
        else:
            per_seq = seq // tm
            out_specs.append(pl.BlockSpec((1, w, tm), lambda i: (i // per_seq, 0, i % per_seq)))
            out_shape.append(jax.ShapeDtypeStruct((n // seq, w, seq), F32))
    return pl.pallas_call(
        functools.partial(_proj_kernel, kv_t=seq is not None),
        grid=(n // tm,),
        in_specs=[row(D_MODEL), _const_spec((1, D_MODEL)), _const_spec(w_main.shape), _const_spec(w_kv.shape)],
        out_specs=out_specs,
        out_shape=out_shape,
        compiler_params=_cparams("parallel"),
        name="projection",
    )(x, g, w_main, w_kv)


def _ln_swish(y, lng, lnb):
    mu = jnp.mean(y, axis=-1, keepdims=True)
    d = y - mu
    var = jnp.mean(d * d, axis=-1, keepdims=True)
    y = d * lax.rsqrt(var + LN_EPS) * lng + lnb
    return y * jax.nn.sigmoid(y)


CONV_ROWS = 64
CONV_PAD = 32
NORM_ROWS = 256


def _conv_prompt_kernel(glu_ref, w_ref, b_ref, lng_ref, lnb_ref, y_ref, ext_ref, stage_ref):
    seq = glu_ref.shape[1]
    n_ct = C_CONV // LANES
    for ct in range(n_ct):
        ext_ref[ct, 0:CONV_PAD, :] = jnp.zeros((CONV_PAD, LANES), F32)
        ext_ref[ct, CONV_PAD:, :] = glu_ref[0, :, ct * LANES:(ct + 1) * LANES]
    shift = CONV_PAD - (CONV_WIDTH - 1)

    def body(c, carry):
        base = pl.multiple_of(c * CONV_ROWS, CONV_ROWS)

        def tile(ct, carry_ct):
            win = ext_ref[ct, pl.ds(base, CONV_ROWS + CONV_PAD), :]
            acc = jnp.zeros((CONV_ROWS, LANES), F32)
            for r in range(SUBLANES):
                shifted = win if r == 0 else pltpu.roll(win, CONV_ROWS + CONV_PAD - r, 0)
                for j in range(CONV_WIDTH):
                    if (j + shift) % SUBLANES == r:
                        a = j + shift - r
                        acc = acc + w_ref[ct, j:j + 1, :] * shifted[a:a + CONV_ROWS]
            stage_ref[ct, pl.ds(base, CONV_ROWS), :] = acc
            return carry_ct

        lax.fori_loop(0, n_ct, tile, 0)
        return carry

    lax.fori_loop(0, seq // CONV_ROWS, body, 0)

    def norm(c, carry):
        base = pl.multiple_of(c * NORM_ROWS, NORM_ROWS)
        y = jnp.concatenate([stage_ref[ct, pl.ds(base, NORM_ROWS), :] for ct in range(n_ct)], axis=1) + b_ref[...]
        y_ref[0, pl.ds(base, NORM_ROWS), :] = _ln_swish(y, lng_ref[...], lnb_ref[...])
        return carry

    lax.fori_loop(0, seq // NORM_ROWS, norm, 0)


def _conv_prompt(glu, w_dw, b_dw, lng, lnb):
    nb, seq, _ = glu.shape
    n_ct = C_CONV // LANES
    w_tiles = jnp.transpose(w_dw.reshape(CONV_WIDTH, n_ct, LANES), (1, 0, 2))
    blk = pl.BlockSpec((1, seq, C_CONV), lambda b: (b, 0, 0))
    return pl.pallas_call(
        _conv_prompt_kernel,
        grid=(nb,),
        in_specs=[blk, _const_spec((n_ct, CONV_WIDTH, LANES))] + [_const_spec((1, C_CONV))] * 3,
        out_specs=blk,
        out_shape=jax.ShapeDtypeStruct(glu.shape, F32),
        scratch_shapes=[pltpu.VMEM((n_ct, seq + CONV_PAD, LANES), F32), pltpu.VMEM((n_ct, seq, LANES), F32)],
        compiler_params=_cparams("parallel"),
        name="conv_prompt",
    )(glu, w_tiles, b_dw, lng, lnb)


def _conv_sample_kernel(st_ref, glu_ref, w_ref, b_ref, lng_ref, lnb_ref, y_ref, st_out_ref):
    n_prev = CONV_WIDTH - 1
    glu = glu_ref[...]
    acc = w_ref[n_prev:n_prev + 1, :] * glu
    for j in range(n_prev):
        acc = acc + w_ref[j:j + 1, :] * st_ref[j]
    y_ref[...] = _ln_swish(acc + b_ref[...], lng_ref[...], lnb_ref[...])
    st_out_ref[0:n_prev - 1] = st_ref[1:n_prev]
    st_out_ref[n_prev - 1] = glu


def _conv_sample(state_t, glu, w_dw, b_dw, lng, lnb):
    nb = glu.shape[0]
    return pl.pallas_call(
        _conv_sample_kernel,
        grid=(1,),
        in_specs=[_const_spec(state_t.shape), _const_spec((nb, C_CONV)), _const_spec((CONV_WIDTH, C_CONV))]
        + [_const_spec((1, C_CONV))] * 3,
        out_specs=[_const_spec((nb, C_CONV)), _const_spec(state_t.shape)],
        out_shape=[jax.ShapeDtypeStruct((nb, C_CONV), F32), jax.ShapeDtypeStruct(state_t.shape, F32)],
        compiler_params=_cparams("arbitrary"),
        name="conv_sample",
    )(state_t, glu, w_dw, b_dw, lng, lnb)


def _lane_iota(shape):
    return lax.broadcasted_iota(jnp.int32, shape, len(shape) - 1)


def _head_rows(q, kvh):
    rows = q.shape[0]
    half = _lane_iota((rows, KV_W)) // HEAD_DIM
    out = []
    for g in range(GROUP):
        h = kvh * GROUP + g
        slab = q[:, KV_W * (h // 2):KV_W * (h // 2 + 1)]
        if h % 2 != kvh:
            slab = pltpu.roll(slab, HEAD_DIM, 1)
        out.append(jnp.where(half == kvh, slab * (ATTN_SCALE * LOG2E), 0.0))
    return out


def _select_blocks(psum, q_pos, n_blk):
    shape = psum.shape
    lane = _lane_iota(shape)
    even = (lane & 1) == 0
    pair = psum + jnp.where(even, pltpu.roll(psum, LANES - 1, 1), pltpu.roll(psum, 1, 1))
    blk = lane // (SEL_BLOCK // CMP_BLOCK)
    back = q_pos // SEL_BLOCK - blk
    forced = (blk == 0) | ((back >= 0) & (back < N_LOCAL))
    score = jnp.where(back >= 0, pair + jnp.where(forced, FORCE_BONUS, 0.0), -jnp.inf)
    rank = jnp.zeros(shape, jnp.int32)
    for i in range(n_blk):
        si = score[:, 2 * i:2 * i + 1]
        beats = (si > score) | ((si == score) & (i < blk))
        rank = rank + beats.astype(jnp.int32)
    return jnp.where((rank < min(TOP_N, n_blk)) & even, 1.0, 0.0)


def _block_expander(n_keys, row_stride):
    li = lax.broadcasted_iota(jnp.int32, (LANES, n_keys), 0)
    ti = lax.broadcasted_iota(jnp.int32, (LANES, n_keys), 1)
    return jnp.where(li == row_stride * (ti // SEL_BLOCK), 1.0, 0.0).astype(BF16)


def _gate_cols(gates, heads):
    return [jnp.concatenate([gates[:, 3 * h + k:3 * h + k + 1] for h in heads], axis=0) for k in range(3)]


def _place_heads(o_heads):
    rows = o_heads[0].shape[0]
    low = _lane_iota((rows, KV_W)) < HEAD_DIM
    slabs = []
    for s in range(N_HEADS // 2):
        pair = []
        for h in (2 * s, 2 * s + 1):
            o = o_heads[h]
            if h // GROUP != h % 2:
                o = pltpu.roll(o, HEAD_DIM, 1)
            pair.append(o)
        slabs.append(jnp.where(low, pair[0], pair[1]))
    return jnp.concatenate(slabs, axis=1)


QB = 256
KT = 256
WIN_KEYS = WINDOW + QB


def _select_bias_t(p_blk, q_pos):
    n_blk = p_blk.shape[0]
    blk = lax.broadcasted_iota(jnp.int32, p_blk.shape, 0)
    back = q_pos // SEL_BLOCK - blk
    forced = (blk == 0) | ((back >= 0) & (back < N_LOCAL))
    score = jnp.where(back >= 0, p_blk + jnp.where(forced, FORCE_BONUS, 0.0), -jnp.inf)
    rank = jnp.zeros(score.shape, jnp.int32)
    for i in range(n_blk):
        si = score[i:i + 1, :]
        beats = (si > score) | ((si == score) & (i < blk))
        rank = rank + beats.astype(jnp.int32)
    return jnp.where((rank < min(TOP_N, n_blk)) & (back >= 0), 0.0, NEG_INF)


WIN_CHUNK = 128


def _attn_prompt_kernel(q_ref, kv_ref, kvw_ref, gate_ref, wk_ref, wv_ref, out_ref,
                        kc_ref, vc_ref, ks_ref, vs_ref, kw_ref, vw_ref, exp_ref,
                        bias_ref, s_ref, mrun_ref, mb_ref, lrun_ref, acc_ref, psum_ref, oct_ref):
    seq = kv_ref.shape[2]
    n_blk = seq // SEL_BLOCK
    n_kt = seq // KT
    n_wc = WIN_KEYS // WIN_CHUNK
    rows = GROUP * QB
    qi = pl.program_id(1)
    qs = qi * QB

    @pl.when((pl.program_id(0) == 0) & (qi == 0))
    def _():
        exp_ref[...] = _block_expander(seq, 1)

    @pl.when(qi == 0)
    def _():
        kc_ref[...] = _pool_blocks_t(kv_ref[0, 0:KV_W, :].astype(BF16), wk_ref[...]).T.astype(BF16)
        vc_ref[...] = _pool_blocks_t(kv_ref[0, KV_W:2 * KV_W, :].astype(BF16), wv_ref[...]).astype(BF16)
        for t in range(n_kt):
            ks_ref[t] = kv_ref[0, 2 * KV_W:3 * KV_W, t * KT:(t + 1) * KT].astype(BF16)
            vs_ref[t] = kv_ref[0, 3 * KV_W:4 * KV_W, t * KT:(t + 1) * KT].astype(BF16)
        for c in range(seq // WIN_CHUNK):
            kw_ref[c] = kvw_ref[0, 0:KV_W, c * WIN_CHUNK:(c + 1) * WIN_CHUNK].astype(BF16)
            vw_ref[c] = kvw_ref[0, KV_W:2 * KV_W, c * WIN_CHUNK:(c + 1) * WIN_CHUNK].astype(BF16)

    q = q_ref[0]
    q_pos = qs + lax.broadcasted_iota(jnp.int32, (QB, 1), 0)
    q_pos_t = qs + lax.broadcasted_iota(jnp.int32, (1, QB), 1)
    add_bias = lambda s, b: (s.reshape(GROUP, QB, s.shape[-1]) + b[None]).reshape(s.shape)
    n_tiles = (qs + QB + KT - 1) // KT
    qp = [jnp.concatenate(_head_rows(q, kvh), axis=0).astype(BF16) for kvh in range(N_KV_HEADS)]

    o_cmp = []
    c_end_t = (lax.broadcasted_iota(jnp.int32, (LANES, rows), 0) + 1) * CMP_BLOCK - 1
    gate_t = gate_ref[0].T
    last = n_tiles - 1
    diag_bias = jnp.where(last * KT + _lane_iota((QB, KT)) <= q_pos, 0.0, NEG_INF)
    for kvh in range(N_KV_HEADS):
        s_t = _dot_nt(kc_ref[...], qp[kvh])
        ok_t = c_end_t <= jnp.concatenate([q_pos_t] * GROUP, axis=1)
        s_t = jnp.where(ok_t, s_t, NEG_INF)
        p_t = jnp.where(ok_t, jnp.exp2(s_t - jnp.max(s_t, axis=0, keepdims=True)), 0.0)
        l_t = jnp.sum(p_t, axis=0, keepdims=True)
        p_t = p_t * jnp.where(l_t > 0.0, 1.0 / l_t, 0.0)
        g_row = jnp.concatenate([gate_t[3 * (kvh * GROUP + g):3 * (kvh * GROUP + g) + 1] for g in range(GROUP)],
                                axis=1)
        oct_ref[...] = jnp.dot(vc_ref[...], (p_t * g_row).astype(BF16), preferred_element_type=F32)
        o_cmp.append(oct_ref[...].T)
        psum = sum(p_t[:, g * QB:(g + 1) * QB] for g in range(GROUP))
        per_sel = SEL_BLOCK // CMP_BLOCK
        p_blk = []
        for c in range(QB // LANES):
            psum_ref[c] = psum[:, c * LANES:(c + 1) * LANES]
            p_blk.append(sum(psum_ref[c, pl.ds(j, n_blk, stride=per_sel), :] for j in range(per_sel)))
        p_blk = jnp.concatenate(p_blk, axis=1)
        blk_bias = _select_bias_t(p_blk, q_pos_t)
        blk_bias = jnp.concatenate([blk_bias, jnp.zeros((LANES - n_blk, QB), F32)], axis=0).T
        key_bias = jnp.dot(blk_bias.astype(BF16), exp_ref[...], preferred_element_type=F32)
        for t in range(n_kt):
            @pl.when(t < n_tiles)
            def _():
                bias_ref[kvh, t] = key_bias[:, t * KT:(t + 1) * KT]
        bias_ref[kvh, last] += diag_bias

    mrun_ref[...] = jnp.full(mrun_ref.shape, NEG_INF, F32)

    def max_pass(t, carry):
        k_t = ks_ref[t]
        for kvh in range(N_KV_HEADS):
            s = add_bias(jnp.dot(qp[kvh], k_t, preferred_element_type=F32), bias_ref[kvh, t])
            s_ref[kvh, t] = s
            mrun_ref[kvh] = jnp.maximum(mrun_ref[kvh], jnp.maximum(s[:, :LANES], s[:, LANES:]))
        return carry

    lax.fori_loop(0, n_tiles, max_pass, 0)
    for kvh in range(N_KV_HEADS):
        mb_ref[kvh] = jnp.broadcast_to(jnp.max(mrun_ref[kvh], axis=-1, keepdims=True), (rows, LANES))
    lrun_ref[...] = jnp.zeros(lrun_ref.shape, F32)
    acc_ref[...] = jnp.zeros(acc_ref.shape, F32)

    def sum_pass(t, carry):
        v_t = vs_ref[t]
        for kvh in range(N_KV_HEADS):
            s = s_ref[kvh, t]
            mb = mb_ref[kvh]
            p = jnp.exp2(jnp.concatenate([s[:, :LANES] - mb, s[:, LANES:] - mb], axis=1))
            lrun_ref[kvh] += p[:, :LANES] + p[:, LANES:]
            acc_ref[kvh] += _dot_nt(p.astype(BF16), v_t)
        return carry

    lax.fori_loop(0, n_tiles, sum_pass, 0)

    win_c0 = jnp.maximum(qs - WINDOW, 0) // WIN_CHUNK
    rel = q_pos - (win_c0 * WIN_CHUNK + _lane_iota((QB, WIN_KEYS)))
    win_bias = jnp.where((rel >= 0) & (rel <= WINDOW), 0.0, NEG_INF)

    o_heads = []
    for kvh in range(N_KV_HEADS):
        _, g_sel, g_win = _gate_cols(gate_ref[0], [kvh * GROUP + g for g in range(GROUP)])
        l = jnp.sum(lrun_ref[kvh], axis=-1, keepdims=True)
        s_w = jnp.concatenate([jnp.dot(qp[kvh], kw_ref[win_c0 + c], preferred_element_type=F32) for c in range(n_wc)],
                              axis=1)
        s_w = add_bias(s_w, win_bias)
        p_w = jnp.exp2(s_w - jnp.max(s_w, axis=-1, keepdims=True))
        o_win = sum(_dot_nt(p_w[:, c * WIN_CHUNK:(c + 1) * WIN_CHUNK].astype(BF16), vw_ref[win_c0 + c])
                    for c in range(n_wc))
        o = (o_cmp[kvh] + acc_ref[kvh] * (jnp.where(l > 0.0, 1.0 / l, 0.0) * g_sel)
             + o_win * (g_win / jnp.sum(p_w, axis=-1, keepdims=True)))
        o_heads += [o[g * QB:(g + 1) * QB] for g in range(GROUP)]

    out_ref[0] = _place_heads(o_heads)


def _attn_prompt(q, kv_t, kvw_t, gate, w_pool_k, w_pool_v):
    nb, seq, _ = q.shape
    per_q = lambda w: pl.BlockSpec((1, QB, w), lambda b, i: (b, i, 0))
    per_b = lambda w: pl.BlockSpec((1, w, seq), lambda b, i: (b, 0, 0))
    wspec = pl.BlockSpec((seq, LANES), lambda b, i: (0, 0))
    rows = GROUP * QB
    return pl.pallas_call(
        _attn_prompt_kernel,
        grid=(nb, seq // QB),
        in_specs=[per_q(N_HEADS * HEAD_DIM), per_b(N_PAGED_SLOTS * KV_W), per_b(2 * KV_W), per_q(LANES), wspec, wspec],
        out_specs=per_q(N_HEADS * HEAD_DIM),
        out_shape=jax.ShapeDtypeStruct(q.shape, F32),
        scratch_shapes=[pltpu.VMEM((LANES, KV_W), BF16), pltpu.VMEM((LANES, KV_W), BF16)]
        + [pltpu.VMEM((seq // KT, KV_W, KT), BF16)] * 2
        + [pltpu.VMEM((seq // WIN_CHUNK, KV_W, WIN_CHUNK), BF16)] * 2
        + [pltpu.VMEM((LANES, seq), BF16), pltpu.VMEM((N_KV_HEADS, seq // KT, QB, KT), F32),
           pltpu.VMEM((N_KV_HEADS, seq // KT, rows, KT), F32)]
        + [pltpu.VMEM((N_KV_HEADS, rows, LANES), F32)] * 4
        + [pltpu.VMEM((QB // LANES, LANES, LANES), F32), pltpu.VMEM((LANES, rows), F32)],
        compiler_params=_cparams("arbitrary", "arbitrary"),
        name="attn_prompt",
    )(q, kv_t, kvw_t, gate, w_pool_k, w_pool_v)


def _pool_matrix(w_cmp, n_keys):
    eye = jnp.eye(n_keys // CMP_BLOCK, LANES // N_KV_HEADS, dtype=F32)
    per_head = [(eye[:, None, :] * w_cmp[None, :, h, None]).reshape(n_keys, -1) for h in range(N_KV_HEADS)]
    return jnp.concatenate(per_head, axis=1).astype(BF16)


def _pool_blocks_t(x_t, w_pool):
    res = jnp.dot(x_t, w_pool, preferred_element_type=F32)
    row = lax.broadcasted_iota(jnp.int32, res.shape, 0)
    own = jnp.where(row < HEAD_DIM, res, pltpu.roll(res, HEAD_DIM, 1))
    return jnp.where(_lane_iota(res.shape) < w_pool.shape[0] // CMP_BLOCK, own, 0.0)


DEC_GROUP = 4


def _attn_sample_kernel(pt_ref, q_ref, kvn_ref, kvwn_ref, gate_ref, wk_ref, wv_ref, kvwt_ref, cache_ref, win_ref,
                        out_ref, win_out_ref, past_ref, exp_ref, sem):
    step = pl.program_id(0)
    n_steps = pl.num_programs(0)
    n_pages = pt_ref.shape[1]
    page = cache_ref.shape[2]
    past_len = n_pages * page
    n_blk = past_len // SEL_BLOCK + 1
    w_buf = win_ref.shape[2]
    slot = step % 2

    def page_copy(st, sl, g, p):
        return pltpu.make_async_copy(cache_ref.at[pt_ref[st * DEC_GROUP + g, p]],
                                     past_ref.at[sl, g, :, pl.ds(p * page, page)], sem.at[sl, g, p])

    def start_pages(st, sl):
        for g in range(DEC_GROUP):
            for p in range(n_pages):
                page_copy(st, sl, g, p).start()

    @pl.when(step == 0)
    def _():
        start_pages(0, 0)
        exp_ref[...] = _block_expander(past_len, SEL_BLOCK // CMP_BLOCK)

    @pl.when(step + 1 < n_steps)
    def _():
        start_pages(step + 1, 1 - slot)

    q_pos = jnp.full((N_HEADS, 1), past_len, jnp.int32)

    def new_key_softmax(qp_f, s_past, ok_past, vt_past, k_new, v_new, ok_new):
        s_new = jnp.sum(qp_f * k_new, axis=-1, keepdims=True)
        s_past = jnp.where(ok_past, s_past, NEG_INF)
        s_new = jnp.where(ok_new, s_new, NEG_INF)
        m = jnp.maximum(jnp.max(s_past, axis=-1, keepdims=True), s_new)
        p_past = jnp.where(ok_past, jnp.exp2(s_past - m), 0.0)
        p_new = jnp.where(ok_new, jnp.exp2(s_new - m), 0.0)
        l = jnp.sum(p_past, axis=-1, keepdims=True) + p_new
        o = _dot_nt(p_past.astype(BF16), vt_past) + p_new * v_new
        return o * jnp.where(l > 0.0, 1.0 / l, 0.0)

    qp_f, o_win = [], []
    for g in range(DEC_GROUP):
        b = step * DEC_GROUP + g
        qf = jnp.concatenate([r for kvh in range(N_KV_HEADS) for r in _head_rows(q_ref[g], kvh)], axis=0)
        qp_f.append(qf)
        kvw_new = kvwn_ref[g]
        win = win_ref[g]
        rel = past_len - (past_len - w_buf + _lane_iota((N_HEADS, w_buf)))
        o_win.append(new_key_softmax(qf, jnp.dot(qf.astype(BF16), win[0:KV_W].astype(BF16),
                                                 preferred_element_type=F32),
                                     (rel >= 0) & (rel <= WINDOW), win[KV_W:2 * KV_W].astype(BF16),
                                     kvw_new[:, 0:KV_W], kvw_new[:, KV_W:2 * KV_W], jnp.full((N_HEADS, 1), True)))
        new_col = jnp.sum(jnp.where(_lane_iota(kvwt_ref.shape) == b, kvwt_ref[...], 0.0), axis=1, keepdims=True)
        win_out_ref[g] = jnp.where(_lane_iota(win.shape) == w_buf - 1, new_col, pltpu.roll(win, w_buf - 1, 1))

    for g in range(DEC_GROUP):
        for p in range(n_pages):
            page_copy(step, slot, g, p).wait()

    for g in range(DEC_GROUP):
        def slab(i):
            return past_ref[slot, g, i * KV_W:(i + 1) * KV_W, :].astype(BF16)

        qp = qp_f[g].astype(BF16)
        kv_new = kvn_ref[g]

        kc_t = _pool_blocks_t(slab(0), wk_ref[...]).astype(BF16)
        vc_t = _pool_blocks_t(slab(1), wv_ref[...]).astype(BF16)
        c_end = (_lane_iota((N_HEADS, LANES)) + 1) * CMP_BLOCK - 1
        s_c = jnp.where(c_end <= q_pos, jnp.dot(qp, kc_t, preferred_element_type=F32), NEG_INF)
        p_c = jnp.where(c_end <= q_pos, jnp.exp2(s_c - jnp.max(s_c, axis=-1, keepdims=True)), 0.0)
        l_c = jnp.sum(p_c, axis=-1, keepdims=True)
        inv_c = jnp.where(l_c > 0.0, 1.0 / l_c, 0.0)
        o_cmp = _dot_nt(p_c.astype(BF16), vc_t) * inv_c
        p_cmp = p_c * inv_c
        psum = jnp.concatenate(
            [jnp.broadcast_to(jnp.sum(p_cmp[kvh * GROUP:(kvh + 1) * GROUP], axis=0, keepdims=True), (GROUP, LANES))
             for kvh in range(N_KV_HEADS)], axis=0)

        sel = _select_blocks(psum, q_pos, n_blk)
        selexp = jnp.dot(sel.astype(BF16), exp_ref[...], preferred_element_type=F32)
        new_blk = past_len // SEL_BLOCK
        ok_new = sel[:, 2 * new_blk:2 * new_blk + 1] > 0.5
        ok_past = (selexp > 0.5) & (_lane_iota((N_HEADS, past_len)) <= q_pos)
        o_sel = new_key_softmax(qp_f[g], jnp.dot(qp, slab(2), preferred_element_type=F32), ok_past, slab(3),
                                kv_new[:, 2 * KV_W:3 * KV_W], kv_new[:, 3 * KV_W:4 * KV_W], ok_new)

        g_cmp, g_sel, g_win = _gate_cols(gate_ref[g], range(N_HEADS))
        o = o_cmp * g_cmp + o_sel * g_sel + o_win[g] * g_win
        out_ref[g] = _place_heads([o[h:h + 1] for h in range(N_HEADS)])


def _attn_sample(page_table, q, kv_new, kvw_new, gate, w_pool_k, w_pool_v, cache_t, win_t):
    nb, n_pages = page_table.shape
    n_feat, page = cache_t.shape[1], cache_t.shape[2]
    w_buf = win_t.shape[2]
    past_len = n_pages * page
    assert nb % DEC_GROUP == 0
    row3 = lambda w: pl.BlockSpec((DEC_GROUP, 1, w), lambda b, pt: (b, 0, 0))
    full = lambda shape: pl.BlockSpec(shape, lambda b, pt: (0,) * len(shape))
    win_spec = pl.BlockSpec((DEC_GROUP, 2 * KV_W, w_buf), lambda b, pt: (b, 0, 0))
    grid_spec = pltpu.PrefetchScalarGridSpec(
        num_scalar_prefetch=1,
        grid=(nb // DEC_GROUP,),
        in_specs=[row3(N_HEADS * HEAD_DIM), row3(N_PAGED_SLOTS * KV_W), row3(2 * KV_W), row3(LANES),
                  full((past_len, LANES)), full((past_len, LANES)), full((2 * KV_W, nb)),
                  pl.BlockSpec(memory_space=pl.ANY), win_spec],
        out_specs=[row3(N_HEADS * HEAD_DIM), win_spec],
        scratch_shapes=[pltpu.VMEM((2, DEC_GROUP, n_feat, past_len), F32),
                        pltpu.VMEM((LANES, past_len), BF16),
                        pltpu.SemaphoreType.DMA((2, DEC_GROUP, n_pages))],
    )
    r3 = lambda a: a.reshape(nb, 1, a.shape[-1])
    return pl.pallas_call(
        _attn_sample_kernel,
        grid_spec=grid_spec,
        out_shape=[jax.ShapeDtypeStruct((nb, 1, N_HEADS * HEAD_DIM), F32),
                   jax.ShapeDtypeStruct(win_t.shape, F32)],
        compiler_params=_cparams("arbitrary"),
        name="attn_sample",
    )(page_table, r3(q), r3(kv_new), r3(kvw_new), r3(gate), w_pool_k, w_pool_v, kvw_new.T, cache_t, win_t)


FF_CHUNK = 1024


def _rms(x, g):
    return x * lax.rsqrt(jnp.mean(x * x, axis=-1, keepdims=True) + RMS_EPS) * g


def _mlp_kernel(x_ref, conv_ref, attn_ref, wout_ref, gm_ref, wup_ref, wdn_ref, gf_ref, y_ref, h_ref):
    h_ref[...] = (x_ref[...]
                  + jnp.dot(conv_ref[...].astype(BF16), wout_ref[0:C_CONV, :], preferred_element_type=F32)
                  + jnp.dot(attn_ref[...].astype(BF16), wout_ref[C_CONV:, :], preferred_element_type=F32))
    hn = _rms(h_ref[...], gm_ref[...]).astype(BF16)
    acc = None
    for c in range(D_FF // FF_CHUNK):
        m = jnp.dot(hn, wup_ref[:, c * FF_CHUNK:(c + 1) * FF_CHUNK], preferred_element_type=F32)
        m = jnp.maximum(m, 0.0)
        t = jnp.dot((m * m).astype(BF16), wdn_ref[c * FF_CHUNK:(c + 1) * FF_CHUNK, :], preferred_element_type=F32)
        acc = t if acc is None else acc + t
    y_ref[...] = _rms(h_ref[...] + acc, gf_ref[...])


def _mlp(x, conv_y, attn_y, w_out, g_mlp, w_up, w_down, g_final, tm):
    n = x.shape[0]
    row = lambda w: pl.BlockSpec((tm, w), lambda i: (i, 0))
    once = lambda shape: pl.BlockSpec(shape, lambda i: (0,) * len(shape), pipeline_mode=pl.Buffered(1))
    return pl.pallas_call(
        _mlp_kernel,
        grid=(n // tm,),
        in_specs=[row(D_MODEL), row(C_CONV), row(N_HEADS * HEAD_DIM), once((D_MODEL, D_MODEL)), once((1, D_MODEL)),
                  once((D_MODEL, D_FF)), once((D_FF, D_MODEL)), once((1, D_MODEL))],
        out_specs=row(D_MODEL),
        out_shape=jax.ShapeDtypeStruct((n, D_MODEL), F32),
        scratch_shapes=[pltpu.VMEM((tm, D_MODEL), F32)],
        compiler_params=_cparams("parallel"),
        name="mlp",
    )(x, conv_y, attn_y, w_out, g_mlp, w_up, w_down, g_final)


def kernel(x_prompt, x_sample, cache_kv, cache_win, state_conv, page_table, g_attn_norm, w_in, w_dw, b_dw,
           conv_ln_g, conv_ln_b, w_cmp_k, w_cmp_v, w_out, g_mlp_norm, w_up, w_down, g_final):
    depth = w_in.shape[0]
    assert depth == 1, "single-layer trunk"
    nb, seq, _ = x_prompt.shape
    db, dseq, _ = x_sample.shape
    assert dseq == 1
    n_phys, page = cache_kv.shape[1], cache_kv.shape[2]
    w_buf = cache_win.shape[2]

    w_main = jnp.concatenate([w_in[0][:, :COL_KV], w_in[0][:, COL_GATE:],
                              jnp.zeros((D_MODEL, IN_COLS_PAD - IN_COLS), F32)], axis=1).astype(BF16)
    w_kv = jnp.transpose(w_in[0][:, COL_KV:COL_GATE]).astype(BF16)
    w_out_b, w_up_b, w_down_b = w_out[0].astype(BF16), w_up[0].astype(BF16), w_down[0].astype(BF16)
    row = lambda a: a.reshape(1, -1)
    g_attn, g_mlp, g_fin = row(g_attn_norm[0]), row(g_mlp_norm[0]), row(g_final)
    conv_w = (w_dw[0], row(b_dw[0]), row(conv_ln_g[0]), row(conv_ln_b[0]))
    mlp_w = (w_out_b, g_mlp, w_up_b, w_down_b, g_fin)

    xp = x_prompt.reshape(nb * seq, D_MODEL)
    glu, q, gate, kv_t, kvw_t = _project(xp, g_attn, w_main, w_kv, 512, seq)
    per_b = lambda a: a.reshape(nb, seq, a.shape[-1])
    glu3 = per_b(glu)
    conv_y = _conv_prompt(glu3, *conv_w)
    attn_y = _attn_prompt(per_b(q), kv_t, kvw_t, per_b(gate), _pool_matrix(w_cmp_k[0], seq),
                          _pool_matrix(w_cmp_v[0], seq))
    y_prompt = _mlp(xp, conv_y.reshape(nb * seq, C_CONV), attn_y.reshape(nb * seq, -1), *mlp_w, 512)
    pos_major = lambda a, n_slots: jnp.transpose(
        a.reshape(a.shape[0], n_slots, N_KV_HEADS, HEAD_DIM, a.shape[-1]), (0, 4, 1, 2, 3))[None]
    new_kv_p = pos_major(kv_t, N_PAGED_SLOTS)
    new_win_p = pos_major(kvw_t[:, :, seq - min(WINDOW, seq):], 2)
    new_conv_p = glu3[:, seq - (CONV_WIDTH - 1):][None]

    xs = x_sample.reshape(db, D_MODEL)
    glu_s, q_s, gate_s, kv_s, kvw_s = _project(xs, g_attn, w_main, w_kv, db)
    conv_y_s, conv_st = _conv_sample(jnp.transpose(state_conv[0], (1, 0, 2)), glu_s, *conv_w)
    cache_t = jnp.transpose(cache_kv[0], (0, 2, 3, 4, 1)).reshape(n_phys, N_PAGED_SLOTS * KV_W, page)
    win_t = jnp.transpose(cache_win[0], (0, 2, 3, 4, 1)).reshape(db, 2 * KV_W, w_buf)
    n_past = page_table.shape[1] * page
    attn_s, new_win_t = _attn_sample(page_table, q_s, kv_s, kvw_s, gate_s, _pool_matrix(w_cmp_k[0], n_past),
                                     _pool_matrix(w_cmp_v[0], n_past), cache_t, win_t)
    new_win_s = pos_major(new_win_t, 2)
    y_sample = _mlp(xs, conv_y_s, attn_s.reshape(db, -1), *mlp_w, db)

    return (y_prompt.reshape(nb, seq, D_MODEL), y_sample.reshape(db, 1, D_MODEL), new_kv_p, new_win_p, new_conv_p,
            kv_s.reshape(1, db, 1, N_PAGED_SLOTS, N_KV_HEADS, HEAD_DIM), new_win_s,
            jnp.transpose(conv_st, (1, 0, 2))[None])
```

```python
import functools

import jax
import jax.numpy as jnp
from jax import lax
from jax.experimental import pallas as pl
from jax.experimental.pallas import tpu as pltpu

F32 = jnp.float32
BF16 = jnp.bfloat16

D_MODEL = 1024
C_CONV = 512
CONV_WIDTH = 31
N_HEADS = 8
HEAD_DIM = 64
N_KV_HEADS = 2
GROUP = N_HEADS // N_KV_HEADS
KV_W = N_KV_HEADS * HEAD_DIM
N_PAGED_SLOTS = 4
D_FF = 4 * D_MODEL
CMP_BLOCK = 32
SEL_BLOCK = 64
TOP_N = 16
N_LOCAL = 2
WINDOW = 512
FORCE_BONUS = float(GROUP + 1)
NEG_INF = -1e30
RMS_EPS = 1e-6
LN_EPS = 1e-5
ATTN_SCALE = HEAD_DIM ** -0.5
LOG2E = 1.4426950408889634

COL_Q = 2 * C_CONV
COL_KV = COL_Q + N_HEADS * HEAD_DIM
COL_WIN = COL_KV + N_PAGED_SLOTS * KV_W
COL_GATE = COL_WIN + 2 * KV_W
IN_COLS = COL_GATE + 3 * N_HEADS
LANES = 128
SUBLANES = 8
IN_COLS_PAD = -(-IN_COLS // LANES) * LANES

VMEM_LIMIT = 56 * 1024 * 1024


def _cparams(*sem):
    return pltpu.CompilerParams(dimension_semantics=sem, vmem_limit_bytes=VMEM_LIMIT)


def _const_spec(shape):
    return pl.BlockSpec(shape, lambda *_: (0,) * len(shape))


def _dot_nt(a, b):
    return lax.dot_general(a, b, (((1,), (1,)), ((), ())), preferred_element_type=F32)


N_KV_COLS = COL_GATE - COL_KV


def _proj_kernel(x_ref, g_ref, w_ref, wkv_ref, glu_ref, q_ref, gate_ref, kv_ref, kvw_ref, *, kv_t):
    x = x_ref[...]
    ms = jnp.mean(x * x, axis=-1, keepdims=True)
    xn = (x * lax.rsqrt(ms + RMS_EPS) * g_ref[...]).astype(BF16)
    z = jnp.dot(xn, w_ref[...], preferred_element_type=F32)
    glu_ref[...] = z[:, :C_CONV] * jax.nn.sigmoid(z[:, C_CONV:COL_Q])
    q_ref[...] = z[:, COL_Q:COL_KV]
    gate_ref[...] = jax.nn.sigmoid(z[:, COL_KV:])
    n_paged = N_PAGED_SLOTS * KV_W
    if kv_t:
        kv = _dot_nt(wkv_ref[...], xn)
        kv_ref[0] = kv[:n_paged]
        kvw_ref[0] = kv[n_paged:]
    else:
        kv = _dot_nt(xn, wkv_ref[...])
        kv_ref[...] = kv[:, :n_paged]
        kvw_ref[...] = kv[:, n_paged:]


def _project(x, g, w_main, w_kv, tm, seq=None):
    n = x.shape[0]
    row = lambda w: pl.BlockSpec((tm, w), lambda i: (i, 0))
    n_paged = N_PAGED_SLOTS * KV_W
    widths = (C_CONV, N_HEADS * HEAD_DIM, LANES)
    out_specs = [row(w) for w in widths]
    out_shape = [jax.ShapeDtypeStruct((n, w), F32) for w in widths]
    for w in (n_paged, 2 * KV_W):
        if seq is None:
            out_specs.append(row(w))
            out_shape.append(jax.ShapeDtypeStruct((n, w), F32))
        else:
            per_seq = seq // tm
            out_specs.append(pl.BlockSpec((1, w, tm), lambda i: (i // per_seq, 0, i % per_seq)))
            out_shape.append(jax.ShapeDtypeStruct((n // seq, w, seq), F32))
    return pl.pallas_call(
        functools.partial(_proj_kernel, kv_t=seq is not None),
        grid=(n // tm,),
        in_specs=[row(D_MODEL), _const_spec((1, D_MODEL)), _const_spec(w_main.shape), _const_spec(w_kv.shape)],
        out_specs=out_specs,
        out_shape=out_shape,
        compiler_params=_cparams("parallel"),
        name="projection",
    )(x, g, w_main, w_kv)


def _ln_swish(y, lng, lnb):
    mu = jnp.mean(y, axis=-1, keepdims=True)
    d = y - mu
    var = jnp.mean(d * d, axis=-1, keepdims=True)
    y = d * lax.rsqrt(var + LN_EPS) * lng + lnb
    return y * jax.nn.sigmoid(y)


CONV_ROWS = 64
CONV_PAD = 32
NORM_ROWS = 256


def _conv_prompt_kernel(glu_ref, w_ref, b_ref, lng_ref, lnb_ref, y_ref, ext_ref, stage_ref):
    seq = glu_ref.shape[1]
    n_ct = C_CONV // LANES
    for ct in range(n_ct):
        ext_ref[ct, 0:CONV_PAD, :] = jnp.zeros((CONV_PAD, LANES), F32)
        ext_ref[ct, CONV_PAD:, :] = glu_ref[0, :, ct * LANES:(ct + 1) * LANES]
    shift = CONV_PAD - (CONV_WIDTH - 1)

    def body(c, carry):
        base = pl.multiple_of(c * CONV_ROWS, CONV_ROWS)

        def tile(ct, carry_ct):
            win = ext_ref[ct, pl.ds(base, CONV_ROWS + CONV_PAD), :]
            acc = jnp.zeros((CONV_ROWS, LANES), F32)
            for r in range(SUBLANES):
                shifted = win if r == 0 else pltpu.roll(win, CONV_ROWS + CONV_PAD - r, 0)
                for j in range(CONV_WIDTH):
                    if (j + shift) % SUBLANES == r:
                        a = j + shift - r
                        acc = acc + w_ref[ct, j:j + 1, :] * shifted[a:a + CONV_ROWS]
            stage_ref[ct, pl.ds(base, CONV_ROWS), :] = acc
            return carry_ct

        lax.fori_loop(0, n_ct, tile, 0)
        return carry

    lax.fori_loop(0, seq // CONV_ROWS, body, 0)

    def norm(c, carry):
        base = pl.multiple_of(c * NORM_ROWS, NORM_ROWS)
        y = jnp.concatenate([stage_ref[ct, pl.ds(base, NORM_ROWS), :] for ct in range(n_ct)], axis=1) + b_ref[...]
        y_ref[0, pl.ds(base, NORM_ROWS), :] = _ln_swish(y, lng_ref[...], lnb_ref[...])
        return carry

    lax.fori_loop(0, seq // NORM_ROWS, norm, 0)


def _conv_prompt(glu, w_dw, b_dw, lng, lnb):
    nb, seq, _ = glu.shape
    n_ct = C_CONV // LANES
    w_tiles = jnp.transpose(w_dw.reshape(CONV_WIDTH, n_ct, LANES), (1, 0, 2))
    blk = pl.BlockSpec((1, seq, C_CONV), lambda b: (b, 0, 0))
    return pl.pallas_call(
        _conv_prompt_kernel,
        grid=(nb,),
        in_specs=[blk, _const_spec((n_ct, CONV_WIDTH, LANES))] + [_const_spec((1, C_CONV))] * 3,
        out_specs=blk,
        out_shape=jax.ShapeDtypeStruct(glu.shape, F32),
        scratch_shapes=[pltpu.VMEM((n_ct, seq + CONV_PAD, LANES), F32), pltpu.VMEM((n_ct, seq, LANES), F32)],
        compiler_params=_cparams("parallel"),
        name="conv_prompt",
    )(glu, w_tiles, b_dw, lng, lnb)


def _conv_sample_kernel(st_ref, glu_ref, w_ref, b_ref, lng_ref, lnb_ref, y_ref, st_out_ref):
    n_prev = CONV_WIDTH - 1
    glu = glu_ref[...]
    acc = w_ref[n_prev:n_prev + 1, :] * glu
    for j in range(n_prev):
        acc = acc + w_ref[j:j + 1, :] * st_ref[j]
    y_ref[...] = _ln_swish(acc + b_ref[...], lng_ref[...], lnb_ref[...])
    st_out_ref[0:n_prev - 1] = st_ref[1:n_prev]
    st_out_ref[n_prev - 1] = glu


def _conv_sample(state_t, glu, w_dw, b_dw, lng, lnb):
    nb = glu.shape[0]
    return pl.pallas_call(
        _conv_sample_kernel,
        grid=(1,),
        in_specs=[_const_spec(state_t.shape), _const_spec((nb, C_CONV)), _const_spec((CONV_WIDTH, C_CONV))]
        + [_const_spec((1, C_CONV))] * 3,
        out_specs=[_const_spec((nb, C_CONV)), _const_spec(state_t.shape)],
        out_shape=[jax.ShapeDtypeStruct((nb, C_CONV), F32), jax.ShapeDtypeStruct(state_t.shape, F32)],
        compiler_params=_cparams("arbitrary"),
        name="conv_sample",
    )(state_t, glu, w_dw, b_dw, lng, lnb)


def _lane_iota(shape):
    return lax.broadcasted_iota(jnp.int32, shape, len(shape) - 1)


def _head_rows(q, kvh):
    rows = q.shape[0]
    half = _lane_iota((rows, KV_W)) // HEAD_DIM
    out = []
    for g in range(GROUP):
        h = kvh * GROUP + g
        slab = q[:, KV_W * (h // 2):KV_W * (h // 2 + 1)]
        if h % 2 != kvh:
            slab = pltpu.roll(slab, HEAD_DIM, 1)
        out.append(jnp.where(half == kvh, slab * (ATTN_SCALE * LOG2E), 0.0))
    return out


def _select_blocks(psum, q_pos, n_blk):
    shape = psum.shape
    lane = _lane_iota(shape)
    even = (lane & 1) == 0
    pair = psum + jnp.where(even, pltpu.roll(psum, LANES - 1, 1), pltpu.roll(psum, 1, 1))
    blk = lane // (SEL_BLOCK // CMP_BLOCK)
    back = q_pos // SEL_BLOCK - blk
    forced = (blk == 0) | ((back >= 0) & (back < N_LOCAL))
    score = jnp.where(back >= 0, pair + jnp.where(forced, FORCE_BONUS, 0.0), -jnp.inf)
    rank = jnp.zeros(shape, jnp.int32)
    for i in range(n_blk):
        si = score[:, 2 * i:2 * i + 1]
        beats = (si > score) | ((si == score) & (i < blk))
        rank = rank + beats.astype(jnp.int32)
    return jnp.where((rank < min(TOP_N, n_blk)) & even, 1.0, 0.0)


def _block_expander(n_keys, row_stride):
    li = lax.broadcasted_iota(jnp.int32, (LANES, n_keys), 0)
    ti = lax.broadcasted_iota(jnp.int32, (LANES, n_keys), 1)
    return jnp.where(li == row_stride * (ti // SEL_BLOCK), 1.0, 0.0).astype(BF16)


def _gate_cols(gates, heads):
    return [jnp.concatenate([gates[:, 3 * h + k:3 * h + k + 1] for h in heads], axis=0) for k in range(3)]


def _place_heads(o_heads):
    rows = o_heads[0].shape[0]
    low = _lane_iota((rows, KV_W)) < HEAD_DIM
    slabs = []
    for s in range(N_HEADS // 2):
        pair = []
        for h in (2 * s, 2 * s + 1):
            o = o_heads[h]
            if h // GROUP != h % 2:
                o = pltpu.roll(o, HEAD_DIM, 1)
            pair.append(o)
        slabs.append(jnp.where(low, pair[0], pair[1]))
    return jnp.concatenate(slabs, axis=1)


QB = 256
KT = 256
WIN_KEYS = WINDOW + QB


def _select_bias_t(p_blks, q_pos):
    n_blk = p_blks[0].shape[0]
    blk = lax.broadcasted_iota(jnp.int32, p_blks[0].shape, 0)
    back = q_pos // SEL_BLOCK - blk
    forced = (blk == 0) | ((back >= 0) & (back < N_LOCAL))
    bonus = jnp.where(forced, FORCE_BONUS, 0.0)
    scores = [jnp.where(back >= 0, p + bonus, -jnp.inf) for p in p_blks]
    ranks = [jnp.zeros(blk.shape, jnp.int32) for _ in p_blks]
    for i in range(n_blk):
        for k, score in enumerate(scores):
            si = score[i:i + 1, :]
            beats = (si > score) | ((si == score) & (i < blk))
            ranks[k] = ranks[k] + beats.astype(jnp.int32)
    return [jnp.where((rank < min(TOP_N, n_blk)) & (back >= 0), 0.0, NEG_INF) for rank in ranks]


WIN_CHUNK = 128


def _attn_prompt_kernel(q_ref, kv_ref, kvw_ref, gate_ref, wk_ref, wv_ref, out_ref,
                        kc_ref, vc_ref, ks_ref, vs_ref, kw_ref, vw_ref, exp_ref,
                        bias_ref, s_ref, mrun_ref, mb_ref, lrun_ref, acc_ref, psum_ref, oct_ref, ow_ref):
    seq = kv_ref.shape[2]
    n_blk = seq // SEL_BLOCK
    n_kt = seq // KT
    n_wc = WIN_KEYS // WIN_CHUNK
    rows = GROUP * QB
    qi = pl.program_id(1)
    qs = qi * QB

    @pl.when((pl.program_id(0) == 0) & (qi == 0))
    def _():
        exp_ref[...] = _block_expander(seq, 1)

    @pl.when(qi == 0)
    def _():
        kc_ref[...] = _pool_blocks_t(kv_ref[0, 0:KV_W, :].astype(BF16), wk_ref[...]).T.astype(BF16)
        vc_ref[...] = _pool_blocks_t(kv_ref[0, KV_W:2 * KV_W, :].astype(BF16), wv_ref[...]).astype(BF16)
        for t in range(n_kt):
            ks_ref[t] = kv_ref[0, 2 * KV_W:3 * KV_W, t * KT:(t + 1) * KT].astype(BF16)
            vs_ref[t] = kv_ref[0, 3 * KV_W:4 * KV_W, t * KT:(t + 1) * KT].astype(BF16)
        for c in range(seq // WIN_CHUNK):
            kw_ref[c] = kvw_ref[0, 0:KV_W, c * WIN_CHUNK:(c + 1) * WIN_CHUNK].astype(BF16)
            vw_ref[c] = kvw_ref[0, KV_W:2 * KV_W, c * WIN_CHUNK:(c + 1) * WIN_CHUNK].astype(BF16)

    q = q_ref[0]
    q_pos = qs + lax.broadcasted_iota(jnp.int32, (QB, 1), 0)
    q_pos_t = qs + lax.broadcasted_iota(jnp.int32, (1, QB), 1)
    add_bias = lambda s, b: (s.reshape(GROUP, QB, s.shape[-1]) + b[None]).reshape(s.shape)
    n_tiles = (qs + QB + KT - 1) // KT
    qp = [jnp.concatenate(_head_rows(q, kvh), axis=0).astype(BF16) for kvh in range(N_KV_HEADS)]

    win_c0 = jnp.maximum(qs - WINDOW, 0) // WIN_CHUNK
    rel = q_pos - (win_c0 * WIN_CHUNK + _lane_iota((QB, WIN_KEYS)))
    win_bias = jnp.where((rel >= 0) & (rel <= WINDOW), 0.0, NEG_INF)
    kv_heads = range(N_KV_HEADS)
    gates = [_gate_cols(gate_ref[0], [kvh * GROUP + g for g in range(GROUP)]) for kvh in kv_heads]
    s_w = [add_bias(jnp.concatenate([jnp.dot(qp[kvh], kw_ref[win_c0 + c], preferred_element_type=F32)
                                     for c in range(n_wc)], axis=1), win_bias) for kvh in kv_heads]
    p_w = [jnp.exp2(s_w[kvh] - jnp.max(s_w[kvh], axis=-1, keepdims=True)) for kvh in kv_heads]
    for kvh in kv_heads:
        o_win = sum(_dot_nt(p_w[kvh][:, c * WIN_CHUNK:(c + 1) * WIN_CHUNK].astype(BF16), vw_ref[win_c0 + c])
                    for c in range(n_wc))
        ow_ref[kvh] = o_win * (gates[kvh][2] / jnp.sum(p_w[kvh], axis=-1, keepdims=True))

    c_end_t = (lax.broadcasted_iota(jnp.int32, (LANES, rows), 0) + 1) * CMP_BLOCK - 1
    ok_t = c_end_t <= jnp.concatenate([q_pos_t] * GROUP, axis=1)
    gate_t = gate_ref[0].T
    per_sel = SEL_BLOCK // CMP_BLOCK
    o_cmp, p_blk = [], []
    for kvh in kv_heads:
        s_t = jnp.where(ok_t, _dot_nt(kc_ref[...], qp[kvh]), NEG_INF)
        p_t = jnp.where(ok_t, jnp.exp2(s_t - jnp.max(s_t, axis=0, keepdims=True)), 0.0)
        l_t = jnp.sum(p_t, axis=0, keepdims=True)
        p_t = p_t * jnp.where(l_t > 0.0, 1.0 / l_t, 0.0)
        g_row = jnp.concatenate([gate_t[3 * (kvh * GROUP + g):3 * (kvh * GROUP + g) + 1] for g in range(GROUP)],
                                axis=1)
        oct_ref[kvh] = jnp.dot(vc_ref[...], (p_t * g_row).astype(BF16), preferred_element_type=F32)
        o_cmp.append(oct_ref[kvh].T)
        psum = sum(p_t[:, g * QB:(g + 1) * QB] for g in range(GROUP))
        tiles = []
        for c in range(QB // LANES):
            psum_ref[kvh, c] = psum[:, c * LANES:(c + 1) * LANES]
            tiles.append(sum(psum_ref[kvh, c, pl.ds(j, n_blk, stride=per_sel), :] for j in range(per_sel)))
        p_blk.append(jnp.concatenate(tiles, axis=1))
    blk_bias = _select_bias_t(p_blk, q_pos_t)
    last = n_tiles - 1
    diag_bias = jnp.where(last * KT + _lane_iota((QB, KT)) <= q_pos, 0.0, NEG_INF)
    for kvh in kv_heads:
        bb = jnp.concatenate([blk_bias[kvh], jnp.zeros((LANES - n_blk, QB), F32)], axis=0).T
        key_bias = jnp.dot(bb.astype(BF16), exp_ref[...], preferred_element_type=F32)
        for t in range(n_kt):
            bias_ref[kvh, t] = key_bias[:, t * KT:(t + 1) * KT]
        bias_ref[kvh, last] += diag_bias

    mrun_ref[...] = jnp.full(mrun_ref.shape, NEG_INF, F32)

    def max_pass(t, carry):
        k_t = ks_ref[t]
        for kvh in range(N_KV_HEADS):
            s = add_bias(jnp.dot(qp[kvh], k_t, preferred_element_type=F32), bias_ref[kvh, t])
            s_ref[kvh, t] = s
            mrun_ref[kvh] = jnp.maximum(mrun_ref[kvh], jnp.maximum(s[:, :LANES], s[:, LANES:]))
        return carry

    lax.fori_loop(0, n_tiles, max_pass, 0)
    for kvh in range(N_KV_HEADS):
        mb_ref[kvh] = jnp.broadcast_to(jnp.max(mrun_ref[kvh], axis=-1, keepdims=True), (rows, LANES))
    lrun_ref[...] = jnp.zeros(lrun_ref.shape, F32)
    acc_ref[...] = jnp.zeros(acc_ref.shape, F32)

    def sum_pass(t, carry):
        v_t = vs_ref[t]
        for kvh in range(N_KV_HEADS):
            s = s_ref[kvh, t]
            mb = mb_ref[kvh]
            p = jnp.exp2(jnp.concatenate([s[:, :LANES] - mb, s[:, LANES:] - mb], axis=1))
            lrun_ref[kvh] += p[:, :LANES] + p[:, LANES:]
            acc_ref[kvh] += _dot_nt(p.astype(BF16), v_t)
        return carry

    lax.fori_loop(0, n_tiles, sum_pass, 0)

    o_heads = []
    for kvh in kv_heads:
        l = jnp.sum(lrun_ref[kvh], axis=-1, keepdims=True)
        o = o_cmp[kvh] + acc_ref[kvh] * (jnp.where(l > 0.0, 1.0 / l, 0.0) * gates[kvh][1]) + ow_ref[kvh]
        o_heads += [o[g * QB:(g + 1) * QB] for g in range(GROUP)]

    out_ref[0] = _place_heads(o_heads)


def _attn_prompt(q, kv_t, kvw_t, gate, w_pool_k, w_pool_v):
    nb, seq, _ = q.shape
    per_q = lambda w: pl.BlockSpec((1, QB, w), lambda b, i: (b, i, 0))
    per_b = lambda w: pl.BlockSpec((1, w, seq), lambda b, i: (b, 0, 0))
    wspec = pl.BlockSpec((seq, LANES), lambda b, i: (0, 0))
    rows = GROUP * QB
    return pl.pallas_call(
        _attn_prompt_kernel,
        grid=(nb, seq // QB),
        in_specs=[per_q(N_HEADS * HEAD_DIM), per_b(N_PAGED_SLOTS * KV_W), per_b(2 * KV_W), per_q(LANES), wspec, wspec],
        out_specs=per_q(N_HEADS * HEAD_DIM),
        out_shape=jax.ShapeDtypeStruct(q.shape, F32),
        scratch_shapes=[pltpu.VMEM((LANES, KV_W), BF16), pltpu.VMEM((LANES, KV_W), BF16)]
        + [pltpu.VMEM((seq // KT, KV_W, KT), BF16)] * 2
        + [pltpu.VMEM((seq // WIN_CHUNK, KV_W, WIN_CHUNK), BF16)] * 2
        + [pltpu.VMEM((LANES, seq), BF16), pltpu.VMEM((N_KV_HEADS, seq // KT, QB, KT), F32),
           pltpu.VMEM((N_KV_HEADS, seq // KT, rows, KT), F32)]
        + [pltpu.VMEM((N_KV_HEADS, rows, LANES), F32)] * 4
        + [pltpu.VMEM((N_KV_HEADS, QB // LANES, LANES, LANES), F32), pltpu.VMEM((N_KV_HEADS, LANES, rows), F32),
           pltpu.VMEM((N_KV_HEADS, rows, LANES), F32)],
        compiler_params=_cparams("arbitrary", "arbitrary"),
        name="attn_prompt",
    )(q, kv_t, kvw_t, gate, w_pool_k, w_pool_v)


def _pool_matrix(w_cmp, n_keys):
    eye = jnp.eye(n_keys // CMP_BLOCK, LANES // N_KV_HEADS, dtype=F32)
    per_head = [(eye[:, None, :] * w_cmp[None, :, h, None]).reshape(n_keys, -1) for h in range(N_KV_HEADS)]
    return jnp.concatenate(per_head, axis=1).astype(BF16)


def _pool_blocks_t(x_t, w_pool):
    res = jnp.dot(x_t, w_pool, preferred_element_type=F32)
    row = lax.broadcasted_iota(jnp.int32, res.shape, 0)
    own = jnp.where(row < HEAD_DIM, res, pltpu.roll(res, HEAD_DIM, 1))
    return jnp.where(_lane_iota(res.shape) < w_pool.shape[0] // CMP_BLOCK, own, 0.0)


DEC_GROUP = 4


def _attn_sample_kernel(pt_ref, q_ref, kvn_ref, kvwn_ref, gate_ref, wk_ref, wv_ref, kvwt_ref, cache_ref, win_ref,
                        out_ref, win_out_ref, past_ref, exp_ref, sem):
    step = pl.program_id(0)
    n_steps = pl.num_programs(0)
    n_pages = pt_ref.shape[1]
    page = cache_ref.shape[2]
    past_len = n_pages * page
    n_blk = past_len // SEL_BLOCK + 1
    w_buf = win_ref.shape[2]
    slot = step % 2

    def page_copy(st, sl, g, p):
        return pltpu.make_async_copy(cache_ref.at[pt_ref[st * DEC_GROUP + g, p]],
                                     past_ref.at[sl, g, :, pl.ds(p * page, page)], sem.at[sl, g, p])

    def start_pages(st, sl):
        for g in range(DEC_GROUP):
            for p in range(n_pages):
                page_copy(st, sl, g, p).start()

    @pl.when(step == 0)
    def _():
        start_pages(0, 0)
        exp_ref[...] = _block_expander(past_len, SEL_BLOCK // CMP_BLOCK)

    @pl.when(step + 1 < n_steps)
    def _():
        start_pages(step + 1, 1 - slot)

    q_pos = jnp.full((N_HEADS, 1), past_len, jnp.int32)

    def new_key_softmax(qp_f, s_past, ok_past, vt_past, k_new, v_new, ok_new):
        s_new = jnp.sum(qp_f * k_new, axis=-1, keepdims=True)
        s_past = jnp.where(ok_past, s_past, NEG_INF)
        s_new = jnp.where(ok_new, s_new, NEG_INF)
        m = jnp.maximum(jnp.max(s_past, axis=-1, keepdims=True), s_new)
        p_past = jnp.where(ok_past, jnp.exp2(s_past - m), 0.0)
        p_new = jnp.where(ok_new, jnp.exp2(s_new - m), 0.0)
        l = jnp.sum(p_past, axis=-1, keepdims=True) + p_new
        o = _dot_nt(p_past.astype(BF16), vt_past) + p_new * v_new
        return o * jnp.where(l > 0.0, 1.0 / l, 0.0)

    qp_f, o_win = [], []
    for g in range(DEC_GROUP):
        b = step * DEC_GROUP + g
        qf = jnp.concatenate([r for kvh in range(N_KV_HEADS) for r in _head_rows(q_ref[g], kvh)], axis=0)
        qp_f.append(qf)
        kvw_new = kvwn_ref[g]
        win = win_ref[g]
        rel = past_len - (past_len - w_buf + _lane_iota((N_HEADS, w_buf)))
        o_win.append(new_key_softmax(qf, jnp.dot(qf.astype(BF16), win[0:KV_W].astype(BF16),
                                                 preferred_element_type=F32),
                                     (rel >= 0) & (rel <= WINDOW), win[KV_W:2 * KV_W].astype(BF16),
                                     kvw_new[:, 0:KV_W], kvw_new[:, KV_W:2 * KV_W], jnp.full((N_HEADS, 1), True)))
        new_col = jnp.sum(jnp.where(_lane_iota(kvwt_ref.shape) == b, kvwt_ref[...], 0.0), axis=1, keepdims=True)
        win_out_ref[g] = jnp.where(_lane_iota(win.shape) == w_buf - 1, new_col, pltpu.roll(win, w_buf - 1, 1))

    for g in range(DEC_GROUP):
        for p in range(n_pages):
            page_copy(step, slot, g, p).wait()

    def slab(g, i):
        return past_ref[slot, g, i * KV_W:(i + 1) * KV_W, :].astype(BF16)

    elems = range(DEC_GROUP)
    qp = [qp_f[g].astype(BF16) for g in elems]
    kc_t = [_pool_blocks_t(slab(g, 0), wk_ref[...]).astype(BF16) for g in elems]
    vc_t = [_pool_blocks_t(slab(g, 1), wv_ref[...]).astype(BF16) for g in elems]
    s_sel = [jnp.dot(qp[g], slab(g, 2), preferred_element_type=F32) for g in elems]
    c_ok = (_lane_iota((N_HEADS, LANES)) + 1) * CMP_BLOCK - 1 <= q_pos
    o_cmp, sel = [], []
    for g in elems:
        s_c = jnp.where(c_ok, jnp.dot(qp[g], kc_t[g], preferred_element_type=F32), NEG_INF)
        p_c = jnp.where(c_ok, jnp.exp2(s_c - jnp.max(s_c, axis=-1, keepdims=True)), 0.0)
        l_c = jnp.sum(p_c, axis=-1, keepdims=True)
        inv_c = jnp.where(l_c > 0.0, 1.0 / l_c, 0.0)
        o_cmp.append(_dot_nt(p_c.astype(BF16), vc_t[g]) * inv_c)
        p_cmp = p_c * inv_c
        psum = jnp.concatenate(
            [jnp.broadcast_to(jnp.sum(p_cmp[kvh * GROUP:(kvh + 1) * GROUP], axis=0, keepdims=True), (GROUP, LANES))
             for kvh in range(N_KV_HEADS)], axis=0)
        sel.append(_select_blocks(psum, q_pos, n_blk))

    new_blk = past_len // SEL_BLOCK
    for g in elems:
        kv_new = kvn_ref[g]
        selexp = jnp.dot(sel[g].astype(BF16), exp_ref[...], preferred_element_type=F32)
        ok_new = sel[g][:, 2 * new_blk:2 * new_blk + 1] > 0.5
        ok_past = (selexp > 0.5) & (_lane_iota((N_HEADS, past_len)) <= q_pos)
        o_sel = new_key_softmax(qp_f[g], s_sel[g], ok_past, slab(g, 3),
                                kv_new[:, 2 * KV_W:3 * KV_W], kv_new[:, 3 * KV_W:4 * KV_W], ok_new)
        g_cmp, g_sel, g_win = _gate_cols(gate_ref[g], range(N_HEADS))
        o = o_cmp[g] * g_cmp + o_sel * g_sel + o_win[g] * g_win
        out_ref[g] = _place_heads([o[h:h + 1] for h in range(N_HEADS)])


def _attn_sample(page_table, q, kv_new, kvw_new, gate, w_pool_k, w_pool_v, cache_t, win_t):
    nb, n_pages = page_table.shape
    n_feat, page = cache_t.shape[1], cache_t.shape[2]
    w_buf = win_t.shape[2]
    past_len = n_pages * page
    assert nb % DEC_GROUP == 0
    row3 = lambda w: pl.BlockSpec((DEC_GROUP, 1, w), lambda b, pt: (b, 0, 0))
    full = lambda shape: pl.BlockSpec(shape, lambda b, pt: (0,) * len(shape))
    win_spec = pl.BlockSpec((DEC_GROUP, 2 * KV_W, w_buf), lambda b, pt: (b, 0, 0))
    grid_spec = pltpu.PrefetchScalarGridSpec(
        num_scalar_prefetch=1,
        grid=(nb // DEC_GROUP,),
        in_specs=[row3(N_HEADS * HEAD_DIM), row3(N_PAGED_SLOTS * KV_W), row3(2 * KV_W), row3(LANES),
                  full((past_len, LANES)), full((past_len, LANES)), full((2 * KV_W, nb)),
                  pl.BlockSpec(memory_space=pl.ANY), win_spec],
        out_specs=[row3(N_HEADS * HEAD_DIM), win_spec],
        scratch_shapes=[pltpu.VMEM((2, DEC_GROUP, n_feat, past_len), F32),
                        pltpu.VMEM((LANES, past_len), BF16),
                        pltpu.SemaphoreType.DMA((2, DEC_GROUP, n_pages))],
    )
    r3 = lambda a: a.reshape(nb, 1, a.shape[-1])
    return pl.pallas_call(
        _attn_sample_kernel,
        grid_spec=grid_spec,
        out_shape=[jax.ShapeDtypeStruct((nb, 1, N_HEADS * HEAD_DIM), F32),
                   jax.ShapeDtypeStruct(win_t.shape, F32)],
        compiler_params=_cparams("arbitrary"),
        name="attn_sample",
    )(page_table, r3(q), r3(kv_new), r3(kvw_new), r3(gate), w_pool_k, w_pool_v, kvw_new.T, cache_t, win_t)


FF_CHUNK = 1024


def _rms(x, g):
    return x * lax.rsqrt(jnp.mean(x * x, axis=-1, keepdims=True) + RMS_EPS) * g


def _mlp_kernel(x_ref, conv_ref, attn_ref, wout_ref, gm_ref, wup_ref, wdn_ref, gf_ref, y_ref, h_ref):
    h_ref[...] = (x_ref[...]
                  + jnp.dot(conv_ref[...].astype(BF16), wout_ref[0:C_CONV, :], preferred_element_type=F32)
                  + jnp.dot(attn_ref[...].astype(BF16), wout_ref[C_CONV:, :], preferred_element_type=F32))
    hn = _rms(h_ref[...], gm_ref[...]).astype(BF16)
    acc = None
    for c in range(D_FF // FF_CHUNK):
        m = jnp.dot(hn, wup_ref[:, c * FF_CHUNK:(c + 1) * FF_CHUNK], preferred_element_type=F32)
        m = jnp.maximum(m, 0.0)
        t = jnp.dot((m * m).astype(BF16), wdn_ref[c * FF_CHUNK:(c + 1) * FF_CHUNK, :], preferred_element_type=F32)
        acc = t if acc is None else acc + t
    y_ref[...] = _rms(h_ref[...] + acc, gf_ref[...])


def _mlp(x, conv_y, attn_y, w_out, g_mlp, w_up, w_down, g_final, tm):
    n = x.shape[0]
    row = lambda w: pl.BlockSpec((tm, w), lambda i: (i, 0))
    once = lambda shape: pl.BlockSpec(shape, lambda i: (0,) * len(shape), pipeline_mode=pl.Buffered(1))
    return pl.pallas_call(
        _mlp_kernel,
        grid=(n // tm,),
        in_specs=[row(D_MODEL), row(C_CONV), row(N_HEADS * HEAD_DIM), once((D_MODEL, D_MODEL)), once((1, D_MODEL)),
                  once((D_MODEL, D_FF)), once((D_FF, D_MODEL)), once((1, D_MODEL))],
        out_specs=row(D_MODEL),
        out_shape=jax.ShapeDtypeStruct((n, D_MODEL), F32),
        scratch_shapes=[pltpu.VMEM((tm, D_MODEL), F32)],
        compiler_params=_cparams("parallel"),
        name="mlp",
    )(x, conv_y, attn_y, w_out, g_mlp, w_up, w_down, g_final)


def kernel(x_prompt, x_sample, cache_kv, cache_win, state_conv, page_table, g_attn_norm, w_in, w_dw, b_dw,
           conv_ln_g, conv_ln_b, w_cmp_k, w_cmp_v, w_out, g_mlp_norm, w_up, w_down, g_final):
    depth = w_in.shape[0]
    assert depth == 1, "single-layer trunk"
    nb, seq, _ = x_prompt.shape
    db, dseq, _ = x_sample.shape
    assert dseq == 1
    n_phys, page = cache_kv.shape[1], cache_kv.shape[2]
    w_buf = cache_win.shape[2]

    w_main = jnp.concatenate([w_in[0][:, :COL_KV], w_in[0][:, COL_GATE:],
                              jnp.zeros((D_MODEL, IN_COLS_PAD - IN_COLS), F32)], axis=1).astype(BF16)
    w_kv = jnp.transpose(w_in[0][:, COL_KV:COL_GATE]).astype(BF16)
    w_out_b, w_up_b, w_down_b = w_out[0].astype(BF16), w_up[0].astype(BF16), w_down[0].astype(BF16)
    row = lambda a: a.reshape(1, -1)
    g_attn, g_mlp, g_fin = row(g_attn_norm[0]), row(g_mlp_norm[0]), row(g_final)
    conv_w = (w_dw[0], row(b_dw[0]), row(conv_ln_g[0]), row(conv_ln_b[0]))
    mlp_w = (w_out_b, g_mlp, w_up_b, w_down_b, g_fin)

    xp = x_prompt.reshape(nb * seq, D_MODEL)
    glu, q, gate, kv_t, kvw_t = _project(xp, g_attn, w_main, w_kv, 512, seq)
    per_b = lambda a: a.reshape(nb, seq, a.shape[-1])
    glu3 = per_b(glu)
    conv_y = _conv_prompt(glu3, *conv_w)
    attn_y = _attn_prompt(per_b(q), kv_t, kvw_t, per_b(gate), _pool_matrix(w_cmp_k[0], seq),
                          _pool_matrix(w_cmp_v[0], seq))
    y_prompt = _mlp(xp, conv_y.reshape(nb * seq, C_CONV), attn_y.reshape(nb * seq, -1), *mlp_w, 512)
    pos_major = lambda a, n_slots: jnp.transpose(
        a.reshape(a.shape[0], n_slots, N_KV_HEADS, HEAD_DIM, a.shape[-1]), (0, 4, 1, 2, 3))[None]
    new_kv_p = pos_major(kv_t, N_PAGED_SLOTS)
    new_win_p = pos_major(kvw_t[:, :, seq - min(WINDOW, seq):], 2)
    new_conv_p = glu3[:, seq - (CONV_WIDTH - 1):][None]

    xs = x_sample.reshape(db, D_MODEL)
    glu_s, q_s, gate_s, kv_s, kvw_s = _project(xs, g_attn, w_main, w_kv, db)
    conv_y_s, conv_st = _conv_sample(jnp.transpose(state_conv[0], (1, 0, 2)), glu_s, *conv_w)
    cache_t = jnp.transpose(cache_kv[0], (0, 2, 3, 4, 1)).reshape(n_phys, N_PAGED_SLOTS * KV_W, page)
    win_t = jnp.transpose(cache_win[0], (0, 2, 3, 4, 1)).reshape(db, 2 * KV_W, w_buf)
    n_past = page_table.shape[1] * page
    attn_s, new_win_t = _attn_sample(page_table, q_s, kv_s, kvw_s, gate_s, _pool_matrix(w_cmp_k[0], n_past),
                                     _pool_matrix(w_cmp_v[0], n_past), cache_t, win_t)
    new_win_s = pos_major(new_win_t, 2)
    y_sample = _mlp(xs, conv_y_s, attn_s.reshape(db, -1), *mlp_w, db)

    return (y_prompt.reshape(nb, seq, D_MODEL), y_sample.reshape(db, 1, D_MODEL), new_kv_p, new_win_p, new_conv_p,
            kv_s.reshape(1, db, 1, N_PAGED_SLOTS, N_KV_HEADS, HEAD_DIM), new_win_s,
            jnp.transpose(conv_st, (1, 0, 2))[None])
```

```python
import functools

import jax
import jax.numpy as jnp
from jax import lax
from jax.experimental import pallas as pl
from jax.experimental.pallas import tpu as pltpu

F32 = jnp.float32
BF16 = jnp.bfloat16

D_MODEL = 1024
C_CONV = 512
CONV_WIDTH = 31
N_HEADS = 8
HEAD_DIM = 64
N_KV_HEADS = 2
GROUP = N_HEADS // N_KV_HEADS
KV_W = N_KV_HEADS * HEAD_DIM
N_PAGED_SLOTS = 4
D_FF = 4 * D_MODEL
CMP_BLOCK = 32
SEL_BLOCK = 64
TOP_N = 16
N_LOCAL = 2
WINDOW = 512
FORCE_BONUS = float(GROUP + 1)
NEG_INF = -1e30
RMS_EPS = 1e-6
LN_EPS = 1e-5
ATTN_SCALE = HEAD_DIM ** -0.5
LOG2E = 1.4426950408889634

COL_Q = 2 * C_CONV
COL_KV = COL_Q + N_HEADS * HEAD_DIM
COL_WIN = COL_KV + N_PAGED_SLOTS * KV_W
COL_GATE = COL_WIN + 2 * KV_W
IN_COLS = COL_GATE + 3 * N_HEADS
LANES = 128
SUBLANES = 8
IN_COLS_PAD = -(-IN_COLS // LANES) * LANES

VMEM_LIMIT = 56 * 1024 * 1024


def _cparams(*sem):
    return pltpu.CompilerParams(dimension_semantics=sem, vmem_limit_bytes=VMEM_LIMIT)


def _const_spec(shape):
    return pl.BlockSpec(shape, lambda *_: (0,) * len(shape))


def _dot_nt(a, b):
    return lax.dot_general(a, b, (((1,), (1,)), ((), ())), preferred_element_type=F32)


N_KV_COLS = COL_GATE - COL_KV


def _proj_kernel(x_ref, g_ref, w_ref, wkv_ref, glu_ref, q_ref, gate_ref, kv_ref, kvw_ref, *, kv_t):
    x = x_ref[...]
    ms = jnp.mean(x * x, axis=-1, keepdims=True)
    xn = (x * lax.rsqrt(ms + RMS_EPS) * g_ref[...]).astype(BF16)
    z = jnp.dot(xn, w_ref[...], preferred_element_type=F32)
    glu_ref[...] = z[:, :C_CONV] * jax.nn.sigmoid(z[:, C_CONV:COL_Q])
    q_ref[...] = z[:, COL_Q:COL_KV]
    gate_ref[...] = jax.nn.sigmoid(z[:, COL_KV:])
    n_paged = N_PAGED_SLOTS * KV_W
    if kv_t:
        kv = _dot_nt(wkv_ref[...], xn)
        kv_ref[0] = kv[:n_paged]
        kvw_ref[0] = kv[n_paged:]
    else:
        kv = _dot_nt(xn, wkv_ref[...])
        kv_ref[...] = kv[:, :n_paged]
        kvw_ref[...] = kv[:, n_paged:]


def _project(x, g, w_main, w_kv, tm, seq=None):
    n = x.shape[0]
    row = lambda w: pl.BlockSpec((tm, w), lambda i: (i, 0))
    n_paged = N_PAGED_SLOTS * KV_W
    widths = (C_CONV, N_HEADS * HEAD_DIM, LANES)
    out_specs = [row(w) for w in widths]
    out_shape = [jax.ShapeDtypeStruct((n, w), F32) for w in widths]
    for w in (n_paged, 2 * KV_W):
        if seq is None:
            out_specs.append(row(w))
            out_shape.append(jax.ShapeDtypeStruct((n, w), F32))
        else:
            per_seq = seq // tm
            out_specs.append(pl.BlockSpec((1, w, tm), lambda i: (i // per_seq, 0, i % per_seq)))
            out_shape.append(jax.ShapeDtypeStruct((n // seq, w, seq), F32))
    return pl.pallas_call(
        functools.partial(_proj_kernel, kv_t=seq is not None),
        grid=(n // tm,),
        in_specs=[row(D_MODEL), _const_spec((1, D_MODEL)), _const_spec(w_main.shape), _const_spec(w_kv.shape)],
        out_specs=out_specs,
        out_shape=out_shape,
        compiler_params=_cparams("parallel"),
        name="projection",
    )(x, g, w_main, w_kv)


def _ln_swish(y, lng, lnb):
    mu = jnp.mean(y, axis=-1, keepdims=True)
    d = y - mu
    var = jnp.mean(d * d, axis=-1, keepdims=True)
    y = d * lax.rsqrt(var + LN_EPS) * lng + lnb
    return y * jax.nn.sigmoid(y)


CONV_ROWS = 128
CONV_PAD = 32
NORM_ROWS = 256


def _conv_prompt_kernel(glu_ref, w_ref, b_ref, lng_ref, lnb_ref, y_ref, ext_ref, stage_ref):
    seq = glu_ref.shape[1]
    n_ct = C_CONV // LANES
    for ct in range(n_ct):
        ext_ref[ct, 0:CONV_PAD, :] = jnp.zeros((CONV_PAD, LANES), F32)
        ext_ref[ct, CONV_PAD:, :] = glu_ref[0, :, ct * LANES:(ct + 1) * LANES]
    shift = CONV_PAD - (CONV_WIDTH - 1)

    def body(c, carry):
        base = pl.multiple_of(c * CONV_ROWS, CONV_ROWS)

        def tile(ct, carry_ct):
            win = ext_ref[ct, pl.ds(base, CONV_ROWS + CONV_PAD), :]
            acc = jnp.zeros((CONV_ROWS, LANES), F32)
            for r in range(SUBLANES):
                shifted = win if r == 0 else pltpu.roll(win, CONV_ROWS + CONV_PAD - r, 0)
                for j in range(CONV_WIDTH):
                    if (j + shift) % SUBLANES == r:
                        a = j + shift - r
                        acc = acc + w_ref[ct, j:j + 1, :] * shifted[a:a + CONV_ROWS]
            stage_ref[ct, pl.ds(base, CONV_ROWS), :] = acc
            return carry_ct

        lax.fori_loop(0, n_ct, tile, 0)
        return carry

    lax.fori_loop(0, seq // CONV_ROWS, body, 0)

    def norm(c, carry):
        base = pl.multiple_of(c * NORM_ROWS, NORM_ROWS)
        y = jnp.concatenate([stage_ref[ct, pl.ds(base, NORM_ROWS), :] for ct in range(n_ct)], axis=1) + b_ref[...]
        y_ref[0, pl.ds(base, NORM_ROWS), :] = _ln_swish(y, lng_ref[...], lnb_ref[...])
        return carry

    lax.fori_loop(0, seq // NORM_ROWS, norm, 0)


def _conv_prompt(glu, w_dw, b_dw, lng, lnb):
    nb, seq, _ = glu.shape
    n_ct = C_CONV // LANES
    w_tiles = jnp.transpose(w_dw.reshape(CONV_WIDTH, n_ct, LANES), (1, 0, 2))
    blk = pl.BlockSpec((1, seq, C_CONV), lambda b: (b, 0, 0))
    return pl.pallas_call(
        _conv_prompt_kernel,
        grid=(nb,),
        in_specs=[blk, _const_spec((n_ct, CONV_WIDTH, LANES))] + [_const_spec((1, C_CONV))] * 3,
        out_specs=blk,
        out_shape=jax.ShapeDtypeStruct(glu.shape, F32),
        scratch_shapes=[pltpu.VMEM((n_ct, seq + CONV_PAD, LANES), F32), pltpu.VMEM((n_ct, seq, LANES), F32)],
        compiler_params=_cparams("parallel"),
        name="conv_prompt",
    )(glu, w_tiles, b_dw, lng, lnb)


def _conv_sample_kernel(st_ref, glu_ref, w_ref, b_ref, lng_ref, lnb_ref, y_ref, st_out_ref):
    n_prev = CONV_WIDTH - 1
    glu = glu_ref[...]
    acc = w_ref[n_prev:n_prev + 1, :] * glu
    for j in range(n_prev):
        acc = acc + w_ref[j:j + 1, :] * st_ref[j]
    y_ref[...] = _ln_swish(acc + b_ref[...], lng_ref[...], lnb_ref[...])
    st_out_ref[0:n_prev - 1] = st_ref[1:n_prev]
    st_out_ref[n_prev - 1] = glu


def _conv_sample(state_t, glu, w_dw, b_dw, lng, lnb):
    nb = glu.shape[0]
    return pl.pallas_call(
        _conv_sample_kernel,
        grid=(1,),
        in_specs=[_const_spec(state_t.shape), _const_spec((nb, C_CONV)), _const_spec((CONV_WIDTH, C_CONV))]
        + [_const_spec((1, C_CONV))] * 3,
        out_specs=[_const_spec((nb, C_CONV)), _const_spec(state_t.shape)],
        out_shape=[jax.ShapeDtypeStruct((nb, C_CONV), F32), jax.ShapeDtypeStruct(state_t.shape, F32)],
        compiler_params=_cparams("arbitrary"),
        name="conv_sample",
    )(state_t, glu, w_dw, b_dw, lng, lnb)


def _lane_iota(shape):
    return lax.broadcasted_iota(jnp.int32, shape, len(shape) - 1)


def _head_rows(q, kvh):
    rows = q.shape[0]
    half = _lane_iota((rows, KV_W)) // HEAD_DIM
    out = []
    for g in range(GROUP):
        h = kvh * GROUP + g
        slab = q[:, KV_W * (h // 2):KV_W * (h // 2 + 1)]
        if h % 2 != kvh:
            slab = pltpu.roll(slab, HEAD_DIM, 1)
        out.append(jnp.where(half == kvh, slab * (ATTN_SCALE * LOG2E), 0.0))
    return out


def _select_blocks(psum, q_pos, n_blk):
    shape = psum.shape
    lane = _lane_iota(shape)
    even = (lane & 1) == 0
    pair = psum + jnp.where(even, pltpu.roll(psum, LANES - 1, 1), pltpu.roll(psum, 1, 1))
    blk = lane // (SEL_BLOCK // CMP_BLOCK)
    back = q_pos // SEL_BLOCK - blk
    forced = (blk == 0) | ((back >= 0) & (back < N_LOCAL))
    score = jnp.where(back >= 0, pair + jnp.where(forced, FORCE_BONUS, 0.0), -jnp.inf)
    rank = jnp.zeros(shape, jnp.int32)
    for i in range(n_blk):
        si = score[:, 2 * i:2 * i + 1]
        beats = (si > score) | ((si == score) & (i < blk))
        rank = rank + beats.astype(jnp.int32)
    return jnp.where((rank < min(TOP_N, n_blk)) & even, 1.0, 0.0)


def _block_expander(n_keys, row_stride):
    li = lax.broadcasted_iota(jnp.int32, (LANES, n_keys), 0)
    ti = lax.broadcasted_iota(jnp.int32, (LANES, n_keys), 1)
    return jnp.where(li == row_stride * (ti // SEL_BLOCK), 1.0, 0.0).astype(BF16)


def _gate_cols(gates, heads):
    return [jnp.concatenate([gates[:, 3 * h + k:3 * h + k + 1] for h in heads], axis=0) for k in range(3)]


def _place_heads(o_heads):
    rows = o_heads[0].shape[0]
    low = _lane_iota((rows, KV_W)) < HEAD_DIM
    slabs = []
    for s in range(N_HEADS // 2):
        pair = []
        for h in (2 * s, 2 * s + 1):
            o = o_heads[h]
            if h // GROUP != h % 2:
                o = pltpu.roll(o, HEAD_DIM, 1)
            pair.append(o)
        slabs.append(jnp.where(low, pair[0], pair[1]))
    return jnp.concatenate(slabs, axis=1)


QB = 256
KT = 256
WIN_KEYS = WINDOW + QB


def _select_bias_t(p_blks, q_pos):
    n_blk = p_blks[0].shape[0]
    blk = lax.broadcasted_iota(jnp.int32, p_blks[0].shape, 0)
    back = q_pos // SEL_BLOCK - blk
    forced = (blk == 0) | ((back >= 0) & (back < N_LOCAL))
    bonus = jnp.where(forced, FORCE_BONUS, 0.0)
    scores = [jnp.where(back >= 0, p + bonus, -jnp.inf) for p in p_blks]
    ranks = [jnp.zeros(blk.shape, jnp.int32) for _ in p_blks]
    for i in range(n_blk):
        for k, score in enumerate(scores):
            si = score[i:i + 1, :]
            beats = (si > score) | ((si == score) & (i < blk))
            ranks[k] = ranks[k] + beats.astype(jnp.int32)
    return [jnp.where((rank < min(TOP_N, n_blk)) & (back >= 0), 0.0, NEG_INF) for rank in ranks]


assert QB % KT == 0 and WIN_KEYS % KT == 0


def _attn_prompt_kernel(q_ref, kv_ref, kvw_ref, gate_ref, wk_ref, wv_ref, out_ref,
                        kc_ref, vc_ref, ks_ref, vs_ref, kw_ref, vw_ref, exp_ref,
                        bias_ref, s_ref, mrun_ref, mb_ref, lrun_ref, acc_ref, psum_ref, oct_ref, ow_ref):
    seq = kv_ref.shape[2]
    n_blk = seq // SEL_BLOCK
    n_kt = seq // KT
    n_wt = WIN_KEYS // KT
    rows = GROUP * QB
    qi = pl.program_id(1)
    qs = qi * QB

    @pl.when((pl.program_id(0) == 0) & (qi == 0))
    def _():
        exp_ref[...] = _block_expander(seq, 1)

    @pl.when(qi == 0)
    def _():
        kc_ref[...] = _pool_blocks_t(kv_ref[0, 0:KV_W, :].astype(BF16), wk_ref[...]).T.astype(BF16)
        vc_ref[...] = _pool_blocks_t(kv_ref[0, KV_W:2 * KV_W, :].astype(BF16), wv_ref[...]).astype(BF16)
        for t in range(n_kt):
            ks_ref[t] = kv_ref[0, 2 * KV_W:3 * KV_W, t * KT:(t + 1) * KT].astype(BF16)
            vs_ref[t] = kv_ref[0, 3 * KV_W:4 * KV_W, t * KT:(t + 1) * KT].astype(BF16)
            kw_ref[t] = kvw_ref[0, 0:KV_W, t * KT:(t + 1) * KT].astype(BF16)
            vw_ref[t] = kvw_ref[0, KV_W:2 * KV_W, t * KT:(t + 1) * KT].astype(BF16)

    q = q_ref[0]
    q_pos = qs + lax.broadcasted_iota(jnp.int32, (QB, 1), 0)
    q_pos_t = qs + lax.broadcasted_iota(jnp.int32, (1, QB), 1)
    add_bias = lambda s, b: (s.reshape(GROUP, QB, s.shape[-1]) + b[None]).reshape(s.shape)
    n_tiles = (qs + QB + KT - 1) // KT
    qp = [jnp.concatenate(_head_rows(q, kvh), axis=0).astype(BF16) for kvh in range(N_KV_HEADS)]

    win_t0 = jnp.maximum(qs - WINDOW, 0) // KT
    rel = q_pos - (win_t0 * KT + _lane_iota((QB, WIN_KEYS)))
    win_bias = jnp.where((rel >= 0) & (rel <= WINDOW), 0.0, NEG_INF)
    kv_heads = range(N_KV_HEADS)
    gates = [_gate_cols(gate_ref[0], [kvh * GROUP + g for g in range(GROUP)]) for kvh in kv_heads]
    s_w = [add_bias(jnp.concatenate([jnp.dot(qp[kvh], kw_ref[win_t0 + c], preferred_element_type=F32)
                                     for c in range(n_wt)], axis=1), win_bias) for kvh in kv_heads]
    p_w = [jnp.exp2(s_w[kvh] - jnp.max(s_w[kvh], axis=-1, keepdims=True)) for kvh in kv_heads]
    for kvh in kv_heads:
        o_win = sum(_dot_nt(p_w[kvh][:, c * KT:(c + 1) * KT].astype(BF16), vw_ref[win_t0 + c]) for c in range(n_wt))
        ow_ref[kvh] = o_win * (gates[kvh][2] / jnp.sum(p_w[kvh], axis=-1, keepdims=True))

    c_end_t = (lax.broadcasted_iota(jnp.int32, (LANES, rows), 0) + 1) * CMP_BLOCK - 1
    ok_t = c_end_t <= jnp.concatenate([q_pos_t] * GROUP, axis=1)
    gate_t = gate_ref[0].T
    per_sel = SEL_BLOCK // CMP_BLOCK
    o_cmp, p_blk = [], []
    for kvh in kv_heads:
        s_t = jnp.where(ok_t, _dot_nt(kc_ref[...], qp[kvh]), NEG_INF)
        p_t = jnp.where(ok_t, jnp.exp2(s_t - jnp.max(s_t, axis=0, keepdims=True)), 0.0)
        l_t = jnp.sum(p_t, axis=0, keepdims=True)
        p_t = p_t * jnp.where(l_t > 0.0, 1.0 / l_t, 0.0)
        g_row = jnp.concatenate([gate_t[3 * (kvh * GROUP + g):3 * (kvh * GROUP + g) + 1] for g in range(GROUP)],
                                axis=1)
        oct_ref[kvh] = jnp.dot(vc_ref[...], (p_t * g_row).astype(BF16), preferred_element_type=F32)
        o_cmp.append(oct_ref[kvh].T)
        psum = sum(p_t[:, g * QB:(g + 1) * QB] for g in range(GROUP))
        tiles = []
        for c in range(QB // LANES):
            psum_ref[kvh, c] = psum[:, c * LANES:(c + 1) * LANES]
            tiles.append(sum(psum_ref[kvh, c, pl.ds(j, n_blk, stride=per_sel), :] for j in range(per_sel)))
        p_blk.append(jnp.concatenate(tiles, axis=1))
    blk_bias = _select_bias_t(p_blk, q_pos_t)
    last = n_tiles - 1
    diag_bias = jnp.where(last * KT + _lane_iota((QB, KT)) <= q_pos, 0.0, NEG_INF)
    for kvh in kv_heads:
        bb = jnp.concatenate([blk_bias[kvh], jnp.zeros((LANES - n_blk, QB), F32)], axis=0).T
        key_bias = jnp.dot(bb.astype(BF16), exp_ref[...], preferred_element_type=F32)
        for t in range(n_kt):
            bias_ref[kvh, t] = key_bias[:, t * KT:(t + 1) * KT]
        bias_ref[kvh, last] += diag_bias

    mrun_ref[...] = jnp.full(mrun_ref.shape, NEG_INF, F32)

    def tile_pairs(tiles_fn):
        def pair(u, carry):
            tiles_fn(2 * u, 2)
            return carry

        lax.fori_loop(0, n_tiles // 2, pair, 0)

        @pl.when(n_tiles % 2 == 1)
        def _():
            tiles_fn(last, 1)

    def max_tiles(t0, n):
        raw = [[jnp.dot(qp[kvh], ks_ref[t0 + i], preferred_element_type=F32) for kvh in kv_heads]
               for i in range(n)]
        for kvh in kv_heads:
            m = None
            for i in range(n):
                s = add_bias(raw[i][kvh], bias_ref[kvh, t0 + i])
                s_ref[kvh, t0 + i] = s
                m_i = jnp.maximum(s[:, :LANES], s[:, LANES:])
                m = m_i if m is None else jnp.maximum(m, m_i)
            mrun_ref[kvh] = jnp.maximum(mrun_ref[kvh], m)

    tile_pairs(max_tiles)
    for kvh in range(N_KV_HEADS):
        mb_ref[kvh] = jnp.broadcast_to(jnp.max(mrun_ref[kvh], axis=-1, keepdims=True), (rows, LANES))
    lrun_ref[...] = jnp.zeros(lrun_ref.shape, F32)
    acc_ref[...] = jnp.zeros(acc_ref.shape, F32)

    def sum_tiles(t0, n):
        for kvh in kv_heads:
            mb = mb_ref[kvh]
            l_add, o_add = None, None
            for i in range(n):
                s = s_ref[kvh, t0 + i]
                p = jnp.exp2(jnp.concatenate([s[:, :LANES] - mb, s[:, LANES:] - mb], axis=1))
                l_i = p[:, :LANES] + p[:, LANES:]
                o_i = _dot_nt(p.astype(BF16), vs_ref[t0 + i])
                l_add = l_i if l_add is None else l_add + l_i
                o_add = o_i if o_add is None else o_add + o_i
            lrun_ref[kvh] += l_add
            acc_ref[kvh] += o_add

    tile_pairs(sum_tiles)

    o_heads = []
    for kvh in kv_heads:
        l = jnp.sum(lrun_ref[kvh], axis=-1, keepdims=True)
        o = o_cmp[kvh] + acc_ref[kvh] * (jnp.where(l > 0.0, 1.0 / l, 0.0) * gates[kvh][1]) + ow_ref[kvh]
        o_heads += [o[g * QB:(g + 1) * QB] for g in range(GROUP)]

    out_ref[0] = _place_heads(o_heads)


def _attn_prompt(q, kv_t, kvw_t, gate, w_pool_k, w_pool_v):
    nb, seq, _ = q.shape
    per_q = lambda w: pl.BlockSpec((1, QB, w), lambda b, i: (b, i, 0))
    per_b = lambda w: pl.BlockSpec((1, w, seq), lambda b, i: (b, 0, 0))
    wspec = pl.BlockSpec((seq, LANES), lambda b, i: (0, 0))
    rows = GROUP * QB
    return pl.pallas_call(
        _attn_prompt_kernel,
        grid=(nb, seq // QB),
        in_specs=[per_q(N_HEADS * HEAD_DIM), per_b(N_PAGED_SLOTS * KV_W), per_b(2 * KV_W), per_q(LANES), wspec, wspec],
        out_specs=per_q(N_HEADS * HEAD_DIM),
        out_shape=jax.ShapeDtypeStruct(q.shape, F32),
        scratch_shapes=[pltpu.VMEM((LANES, KV_W), BF16), pltpu.VMEM((LANES, KV_W), BF16)]
        + [pltpu.VMEM((seq // KT, KV_W, KT), BF16)] * 4
        + [pltpu.VMEM((LANES, seq), BF16), pltpu.VMEM((N_KV_HEADS, seq // KT, QB, KT), F32),
           pltpu.VMEM((N_KV_HEADS, seq // KT, rows, KT), F32)]
        + [pltpu.VMEM((N_KV_HEADS, rows, LANES), F32)] * 4
        + [pltpu.VMEM((N_KV_HEADS, QB // LANES, LANES, LANES), F32), pltpu.VMEM((N_KV_HEADS, LANES, rows), F32),
           pltpu.VMEM((N_KV_HEADS, rows, LANES), F32)],
        compiler_params=_cparams("arbitrary", "arbitrary"),
        name="attn_prompt",
    )(q, kv_t, kvw_t, gate, w_pool_k, w_pool_v)


def _pool_matrix(w_cmp, n_keys):
    eye = jnp.eye(n_keys // CMP_BLOCK, LANES // N_KV_HEADS, dtype=F32)
    per_head = [(eye[:, None, :] * w_cmp[None, :, h, None]).reshape(n_keys, -1) for h in range(N_KV_HEADS)]
    return jnp.concatenate(per_head, axis=1).astype(BF16)


def _pool_blocks_t(x_t, w_pool):
    res = jnp.dot(x_t, w_pool, preferred_element_type=F32)
    row = lax.broadcasted_iota(jnp.int32, res.shape, 0)
    own = jnp.where(row < HEAD_DIM, res, pltpu.roll(res, HEAD_DIM, 1))
    return jnp.where(_lane_iota(res.shape) < w_pool.shape[0] // CMP_BLOCK, own, 0.0)


DEC_GROUP = 4


def _attn_sample_kernel(pt_ref, q_ref, kvn_ref, kvwn_ref, gate_ref, wk_ref, wv_ref, kvwt_ref, cache_ref, win_ref,
                        out_ref, win_out_ref, past_ref, exp_ref, sem):
    step = pl.program_id(0)
    n_steps = pl.num_programs(0)
    n_pages = pt_ref.shape[1]
    page = cache_ref.shape[2]
    past_len = n_pages * page
    n_blk = past_len // SEL_BLOCK + 1
    w_buf = win_ref.shape[2]
    slot = step % 2

    def page_copy(st, sl, g, p):
        return pltpu.make_async_copy(cache_ref.at[pt_ref[st * DEC_GROUP + g, p]],
                                     past_ref.at[sl, g, :, pl.ds(p * page, page)], sem.at[sl, g, p])

    def start_pages(st, sl):
        for g in range(DEC_GROUP):
            for p in range(n_pages):
                page_copy(st, sl, g, p).start()

    @pl.when(step == 0)
    def _():
        start_pages(0, 0)
        exp_ref[...] = _block_expander(past_len, SEL_BLOCK // CMP_BLOCK)

    @pl.when(step + 1 < n_steps)
    def _():
        start_pages(step + 1, 1 - slot)

    q_pos = jnp.full((N_HEADS, 1), past_len, jnp.int32)

    def new_key_softmax(qp_f, s_past, ok_past, vt_past, k_new, v_new, ok_new):
        s_new = jnp.sum(qp_f * k_new, axis=-1, keepdims=True)
        s_past = jnp.where(ok_past, s_past, NEG_INF)
        s_new = jnp.where(ok_new, s_new, NEG_INF)
        m = jnp.maximum(jnp.max(s_past, axis=-1, keepdims=True), s_new)
        p_past = jnp.where(ok_past, jnp.exp2(s_past - m), 0.0)
        p_new = jnp.where(ok_new, jnp.exp2(s_new - m), 0.0)
        l = jnp.sum(p_past, axis=-1, keepdims=True) + p_new
        o = _dot_nt(p_past.astype(BF16), vt_past) + p_new * v_new
        return o * jnp.where(l > 0.0, 1.0 / l, 0.0)

    qp_f, o_win = [], []
    for g in range(DEC_GROUP):
        b = step * DEC_GROUP + g
        qf = jnp.concatenate([r for kvh in range(N_KV_HEADS) for r in _head_rows(q_ref[g], kvh)], axis=0)
        qp_f.append(qf)
        kvw_new = kvwn_ref[g]
        win = win_ref[g]
        rel = past_len - (past_len - w_buf + _lane_iota((N_HEADS, w_buf)))
        o_win.append(new_key_softmax(qf, jnp.dot(qf.astype(BF16), win[0:KV_W].astype(BF16),
                                                 preferred_element_type=F32),
                                     (rel >= 0) & (rel <= WINDOW), win[KV_W:2 * KV_W].astype(BF16),
                                     kvw_new[:, 0:KV_W], kvw_new[:, KV_W:2 * KV_W], jnp.full((N_HEADS, 1), True)))
        new_col = jnp.sum(jnp.where(_lane_iota(kvwt_ref.shape) == b, kvwt_ref[...], 0.0), axis=1, keepdims=True)
        win_out_ref[g] = jnp.where(_lane_iota(win.shape) == w_buf - 1, new_col, pltpu.roll(win, w_buf - 1, 1))

    for g in range(DEC_GROUP):
        for p in range(n_pages):
            page_copy(step, slot, g, p).wait()

    def slab(g, i):
        return past_ref[slot, g, i * KV_W:(i + 1) * KV_W, :].astype(BF16)

    elems = range(DEC_GROUP)
    qp = [qp_f[g].astype(BF16) for g in elems]
    kc_t = [_pool_blocks_t(slab(g, 0), wk_ref[...]).astype(BF16) for g in elems]
    vc_t = [_pool_blocks_t(slab(g, 1), wv_ref[...]).astype(BF16) for g in elems]
    s_sel = [jnp.dot(qp[g], slab(g, 2), preferred_element_type=F32) for g in elems]
    c_ok = (_lane_iota((N_HEADS, LANES)) + 1) * CMP_BLOCK - 1 <= q_pos
    o_cmp, sel = [], []
    for g in elems:
        s_c = jnp.where(c_ok, jnp.dot(qp[g], kc_t[g], preferred_element_type=F32), NEG_INF)
        p_c = jnp.where(c_ok, jnp.exp2(s_c - jnp.max(s_c, axis=-1, keepdims=True)), 0.0)
        l_c = jnp.sum(p_c, axis=-1, keepdims=True)
        inv_c = jnp.where(l_c > 0.0, 1.0 / l_c, 0.0)
        o_cmp.append(_dot_nt(p_c.astype(BF16), vc_t[g]) * inv_c)
        p_cmp = p_c * inv_c
        psum = jnp.concatenate(
            [jnp.broadcast_to(jnp.sum(p_cmp[kvh * GROUP:(kvh + 1) * GROUP], axis=0, keepdims=True), (GROUP, LANES))
             for kvh in range(N_KV_HEADS)], axis=0)
        sel.append(_select_blocks(psum, q_pos, n_blk))

    new_blk = past_len // SEL_BLOCK
    for g in elems:
        kv_new = kvn_ref[g]
        selexp = jnp.dot(sel[g].astype(BF16), exp_ref[...], preferred_element_type=F32)
        ok_new = sel[g][:, 2 * new_blk:2 * new_blk + 1] > 0.5
        ok_past = (selexp > 0.5) & (_lane_iota((N_HEADS, past_len)) <= q_pos)
        o_sel = new_key_softmax(qp_f[g], s_sel[g], ok_past, slab(g, 3),
                                kv_new[:, 2 * KV_W:3 * KV_W], kv_new[:, 3 * KV_W:4 * KV_W], ok_new)
        g_cmp, g_sel, g_win = _gate_cols(gate_ref[g], range(N_HEADS))
        o = o_cmp[g] * g_cmp + o_sel * g_sel + o_win[g] * g_win
        out_ref[g] = _place_heads([o[h:h + 1] for h in range(N_HEADS)])


def _attn_sample(page_table, q, kv_new, kvw_new, gate, w_pool_k, w_pool_v, cache_t, win_t):
    nb, n_pages = page_table.shape
    n_feat, page = cache_t.shape[1], cache_t.shape[2]
    w_buf = win_t.shape[2]
    past_len = n_pages * page
    assert nb % DEC_GROUP == 0
    row3 = lambda w: pl.BlockSpec((DEC_GROUP, 1, w), lambda b, pt: (b, 0, 0))
    full = lambda shape: pl.BlockSpec(shape, lambda b, pt: (0,) * len(shape))
    win_spec = pl.BlockSpec((DEC_GROUP, 2 * KV_W, w_buf), lambda b, pt: (b, 0, 0))
    grid_spec = pltpu.PrefetchScalarGridSpec(
        num_scalar_prefetch=1,
        grid=(nb // DEC_GROUP,),
        in_specs=[row3(N_HEADS * HEAD_DIM), row3(N_PAGED_SLOTS * KV_W), row3(2 * KV_W), row3(LANES),
                  full((past_len, LANES)), full((past_len, LANES)), full((2 * KV_W, nb)),
                  pl.BlockSpec(memory_space=pl.ANY), win_spec],
        out_specs=[row3(N_HEADS * HEAD_DIM), win_spec],
        scratch_shapes=[pltpu.VMEM((2, DEC_GROUP, n_feat, past_len), F32),
                        pltpu.VMEM((LANES, past_len), BF16),
                        pltpu.SemaphoreType.DMA((2, DEC_GROUP, n_pages))],
    )
    r3 = lambda a: a.reshape(nb, 1, a.shape[-1])
    return pl.pallas_call(
        _attn_sample_kernel,
        grid_spec=grid_spec,
        out_shape=[jax.ShapeDtypeStruct((nb, 1, N_HEADS * HEAD_DIM), F32),
                   jax.ShapeDtypeStruct(win_t.shape, F32)],
        compiler_params=_cparams("arbitrary"),
        name="attn_sample",
    )(page_table, r3(q), r3(kv_new), r3(kvw_new), r3(gate), w_pool_k, w_pool_v, kvw_new.T, cache_t, win_t)


FF_CHUNK = 1024


def _rms(x, g):
    return x * lax.rsqrt(jnp.mean(x * x, axis=-1, keepdims=True) + RMS_EPS) * g


def _mlp_kernel(x_ref, conv_ref, attn_ref, wout_ref, gm_ref, wup_ref, wdn_ref, gf_ref, y_ref, h_ref):
    h_ref[...] = (x_ref[...]
                  + jnp.dot(conv_ref[...].astype(BF16), wout_ref[0:C_CONV, :], preferred_element_type=F32)
                  + jnp.dot(attn_ref[...].astype(BF16), wout_ref[C_CONV:, :], preferred_element_type=F32))
    hn = _rms(h_ref[...], gm_ref[...]).astype(BF16)
    acc = None
    for c in range(D_FF // FF_CHUNK):
        m = jnp.dot(hn, wup_ref[:, c * FF_CHUNK:(c + 1) * FF_CHUNK], preferred_element_type=F32)
        m = jnp.maximum(m, 0.0)
        t = jnp.dot((m * m).astype(BF16), wdn_ref[c * FF_CHUNK:(c + 1) * FF_CHUNK, :], preferred_element_type=F32)
        acc = t if acc is None else acc + t
    y_ref[...] = _rms(h_ref[...] + acc, gf_ref[...])


def _mlp(x, conv_y, attn_y, w_out, g_mlp, w_up, w_down, g_final, tm):
    n = x.shape[0]
    row = lambda w: pl.BlockSpec((tm, w), lambda i: (i, 0))
    once = lambda shape: pl.BlockSpec(shape, lambda i: (0,) * len(shape), pipeline_mode=pl.Buffered(1))
    return pl.pallas_call(
        _mlp_kernel,
        grid=(n // tm,),
        in_specs=[row(D_MODEL), row(C_CONV), row(N_HEADS * HEAD_DIM), once((D_MODEL, D_MODEL)), once((1, D_MODEL)),
                  once((D_MODEL, D_FF)), once((D_FF, D_MODEL)), once((1, D_MODEL))],
        out_specs=row(D_MODEL),
        out_shape=jax.ShapeDtypeStruct((n, D_MODEL), F32),
        scratch_shapes=[pltpu.VMEM((tm, D_MODEL), F32)],
        compiler_params=_cparams("parallel"),
        name="mlp",
    )(x, conv_y, attn_y, w_out, g_mlp, w_up, w_down, g_final)


def kernel(x_prompt, x_sample, cache_kv, cache_win, state_conv, page_table, g_attn_norm, w_in, w_dw, b_dw,
           conv_ln_g, conv_ln_b, w_cmp_k, w_cmp_v, w_out, g_mlp_norm, w_up, w_down, g_final):
    depth = w_in.shape[0]
    assert depth == 1, "single-layer trunk"
    nb, seq, _ = x_prompt.shape
    db, dseq, _ = x_sample.shape
    assert dseq == 1
    n_phys, page = cache_kv.shape[1], cache_kv.shape[2]
    w_buf = cache_win.shape[2]

    w_main = jnp.concatenate([w_in[0][:, :COL_KV], w_in[0][:, COL_GATE:],
                              jnp.zeros((D_MODEL, IN_COLS_PAD - IN_COLS), F32)], axis=1).astype(BF16)
    w_kv = jnp.transpose(w_in[0][:, COL_KV:COL_GATE]).astype(BF16)
    w_out_b, w_up_b, w_down_b = w_out[0].astype(BF16), w_up[0].astype(BF16), w_down[0].astype(BF16)
    row = lambda a: a.reshape(1, -1)
    g_attn, g_mlp, g_fin = row(g_attn_norm[0]), row(g_mlp_norm[0]), row(g_final)
    conv_w = (w_dw[0], row(b_dw[0]), row(conv_ln_g[0]), row(conv_ln_b[0]))
    mlp_w = (w_out_b, g_mlp, w_up_b, w_down_b, g_fin)

    xp = x_prompt.reshape(nb * seq, D_MODEL)
    glu, q, gate, kv_t, kvw_t = _project(xp, g_attn, w_main, w_kv, 512, seq)
    per_b = lambda a: a.reshape(nb, seq, a.shape[-1])
    glu3 = per_b(glu)
    conv_y = _conv_prompt(glu3, *conv_w)
    attn_y = _attn_prompt(per_b(q), kv_t, kvw_t, per_b(gate), _pool_matrix(w_cmp_k[0], seq),
                          _pool_matrix(w_cmp_v[0], seq))
    y_prompt = _mlp(xp, conv_y.reshape(nb * seq, C_CONV), attn_y.reshape(nb * seq, -1), *mlp_w, 512)
    pos_major = lambda a, n_slots: jnp.transpose(
        a.reshape(a.shape[0], n_slots, N_KV_HEADS, HEAD_DIM, a.shape[-1]), (0, 4, 1, 2, 3))[None]
    new_kv_p = pos_major(kv_t, N_PAGED_SLOTS)
    new_win_p = pos_major(kvw_t[:, :, seq - min(WINDOW, seq):], 2)
    new_conv_p = glu3[:, seq - (CONV_WIDTH - 1):][None]

    xs = x_sample.reshape(db, D_MODEL)
    glu_s, q_s, gate_s, kv_s, kvw_s = _project(xs, g_attn, w_main, w_kv, db)
    conv_y_s, conv_st = _conv_sample(jnp.transpose(state_conv[0], (1, 0, 2)), glu_s, *conv_w)
    cache_t = jnp.transpose(cache_kv[0], (0, 2, 3, 4, 1)).reshape(n_phys, N_PAGED_SLOTS * KV_W, page)
    win_t = jnp.transpose(cache_win[0], (0, 2, 3, 4, 1)).reshape(db, 2 * KV_W, w_buf)
    n_past = page_table.shape[1] * page
    attn_s, new_win_t = _attn_sample(page_table, q_s, kv_s, kvw_s, gate_s, _pool_matrix(w_cmp_k[0], n_past),
                                     _pool_matrix(w_cmp_v[0], n_past), cache_t, win_t)
    new_win_s = pos_major(new_win_t, 2)
    y_sample = _mlp(xs, conv_y_s, attn_s.reshape(db, -1), *mlp_w, db)

    return (y_prompt.reshape(nb, seq, D_MODEL), y_sample.reshape(db, 1, D_MODEL), new_kv_p, new_win_p, new_conv_p,
            kv_s.reshape(1, db, 1, N_PAGED_SLOTS, N_KV_HEADS, HEAD_DIM), new_win_s,
            jnp.transpose(conv_st, (1, 0, 2))[None])
```

```python
import functools

import jax
import jax.numpy as jnp
from jax import lax
from jax.experimental import pallas as pl
from jax.experimental.pallas import tpu as pltpu

F32 = jnp.float32
BF16 = jnp.bfloat16

D_MODEL = 1024
C_CONV = 512
CONV_WIDTH = 31
N_HEADS = 8
HEAD_DIM = 64
N_KV_HEADS = 2
GROUP = N_HEADS // N_KV_HEADS
KV_W = N_KV_HEADS * HEAD_DIM
N_PAGED_SLOTS = 4
D_FF = 4 * D_MODEL
CMP_BLOCK = 32
SEL_BLOCK = 64
TOP_N = 16
N_LOCAL = 2
WINDOW = 512
FORCE_BONUS = float(GROUP + 1)
NEG_INF = -1e30
RMS_EPS = 1e-6
LN_EPS = 1e-5
ATTN_SCALE = HEAD_DIM ** -0.5
LOG2E = 1.4426950408889634

COL_Q = 2 * C_CONV
COL_KV = COL_Q + N_HEADS * HEAD_DIM
COL_WIN = COL_KV + N_PAGED_SLOTS * KV_W
COL_GATE = COL_WIN + 2 * KV_W
IN_COLS = COL_GATE + 3 * N_HEADS
LANES = 128
SUBLANES = 8
IN_COLS_PAD = -(-IN_COLS // LANES) * LANES

VMEM_LIMIT = 56 * 1024 * 1024

PROJ_ROWS = 1024
MLP_ROWS = 512
CONV_ROWS = 128
CONV_PAD = 32
NORM_ROWS = 256
QB = 256
KT = 256
WIN_KEYS = WINDOW + QB
assert QB % KT == 0 and WIN_KEYS % KT == 0
DEC_GROUP = 4
FF_CHUNK = 1024


def _cparams(*sem):
    return pltpu.CompilerParams(dimension_semantics=sem, vmem_limit_bytes=VMEM_LIMIT)


def _const_spec(shape):
    return pl.BlockSpec(shape, lambda *_: (0,) * len(shape))


def _dot_nt(a, b):
    return lax.dot_general(a, b, (((1,), (1,)), ((), ())), preferred_element_type=F32)


def _lane_iota(shape):
    return lax.broadcasted_iota(jnp.int32, shape, len(shape) - 1)


def _proj_kernel(x_ref, g_ref, w_ref, wkv_ref, glu_ref, q_ref, gate_ref, kv_ref, kvw_ref, *, kv_t):
    x = x_ref[...]
    ms = jnp.mean(x * x, axis=-1, keepdims=True)
    xn = (x * lax.rsqrt(ms + RMS_EPS) * g_ref[...]).astype(BF16)
    z = jnp.dot(xn, w_ref[...], preferred_element_type=F32)
    glu_ref[...] = z[:, :C_CONV] * jax.nn.sigmoid(z[:, C_CONV:COL_Q])
    q_ref[...] = z[:, COL_Q:COL_KV]
    gate_ref[...] = jax.nn.sigmoid(z[:, COL_KV:])
    n_paged = N_PAGED_SLOTS * KV_W
    if kv_t:
        kv = _dot_nt(wkv_ref[...], xn)
        kv_ref[0] = kv[:n_paged]
        kvw_ref[0] = kv[n_paged:]
    else:
        kv = _dot_nt(xn, wkv_ref[...])
        kv_ref[...] = kv[:, :n_paged]
        kvw_ref[...] = kv[:, n_paged:]


def _project(x, g, w_main, w_kv, tm, seq=None):
    n = x.shape[0]
    row = lambda w: pl.BlockSpec((tm, w), lambda i: (i, 0))
    n_paged = N_PAGED_SLOTS * KV_W
    widths = (C_CONV, N_HEADS * HEAD_DIM, LANES)
    out_specs = [row(w) for w in widths]
    out_shape = [jax.ShapeDtypeStruct((n, w), F32) for w in widths]
    for w in (n_paged, 2 * KV_W):
        if seq is None:
            out_specs.append(row(w))
            out_shape.append(jax.ShapeDtypeStruct((n, w), F32))
        else:
            per_seq = seq // tm
            out_specs.append(pl.BlockSpec((1, w, tm), lambda i: (i // per_seq, 0, i % per_seq)))
            out_shape.append(jax.ShapeDtypeStruct((n // seq, w, seq), F32))
    return pl.pallas_call(
        functools.partial(_proj_kernel, kv_t=seq is not None),
        grid=(n // tm,),
        in_specs=[row(D_MODEL), _const_spec((1, D_MODEL)), _const_spec(w_main.shape), _const_spec(w_kv.shape)],
        out_specs=out_specs,
        out_shape=out_shape,
        compiler_params=_cparams("parallel"),
        name="projection",
    )(x, g, w_main, w_kv)


def _ln_swish(y, lng, lnb):
    mu = jnp.mean(y, axis=-1, keepdims=True)
    d = y - mu
    var = jnp.mean(d * d, axis=-1, keepdims=True)
    y = d * lax.rsqrt(var + LN_EPS) * lng + lnb
    return y * jax.nn.sigmoid(y)


def _conv_prompt_kernel(glu_ref, w_ref, b_ref, lng_ref, lnb_ref, y_ref, ext_ref, stage_ref):
    seq = glu_ref.shape[1]
    n_ct = C_CONV // LANES
    for ct in range(n_ct):
        ext_ref[ct, 0:CONV_PAD, :] = jnp.zeros((CONV_PAD, LANES), F32)
        ext_ref[ct, CONV_PAD:, :] = glu_ref[0, :, ct * LANES:(ct + 1) * LANES]
    shift = CONV_PAD - (CONV_WIDTH - 1)

    def body(c, carry):
        base = pl.multiple_of(c * CONV_ROWS, CONV_ROWS)

        def tile(ct, carry_ct):
            win = ext_ref[ct, pl.ds(base, CONV_ROWS + CONV_PAD), :]
            acc = jnp.zeros((CONV_ROWS, LANES), F32)
            for r in range(SUBLANES):
                shifted = win if r == 0 else pltpu.roll(win, CONV_ROWS + CONV_PAD - r, 0)
                for j in range(CONV_WIDTH):
                    if (j + shift) % SUBLANES == r:
                        a = j + shift - r
                        acc = acc + w_ref[ct, j:j + 1, :] * shifted[a:a + CONV_ROWS]
            stage_ref[ct, pl.ds(base, CONV_ROWS), :] = acc
            return carry_ct

        lax.fori_loop(0, n_ct, tile, 0)
        return carry

    lax.fori_loop(0, seq // CONV_ROWS, body, 0)

    def norm(c, carry):
        base = pl.multiple_of(c * NORM_ROWS, NORM_ROWS)
        y = jnp.concatenate([stage_ref[ct, pl.ds(base, NORM_ROWS), :] for ct in range(n_ct)], axis=1) + b_ref[...]
        y_ref[0, pl.ds(base, NORM_ROWS), :] = _ln_swish(y, lng_ref[...], lnb_ref[...])
        return carry

    lax.fori_loop(0, seq // NORM_ROWS, norm, 0)


def _conv_prompt(glu, w_dw, b_dw, lng, lnb):
    nb, seq, _ = glu.shape
    n_ct = C_CONV // LANES
    w_tiles = jnp.transpose(w_dw.reshape(CONV_WIDTH, n_ct, LANES), (1, 0, 2))
    blk = pl.BlockSpec((1, seq, C_CONV), lambda b: (b, 0, 0))
    return pl.pallas_call(
        _conv_prompt_kernel,
        grid=(nb,),
        in_specs=[blk, _const_spec((n_ct, CONV_WIDTH, LANES))] + [_const_spec((1, C_CONV))] * 3,
        out_specs=blk,
        out_shape=jax.ShapeDtypeStruct(glu.shape, F32),
        scratch_shapes=[pltpu.VMEM((n_ct, seq + CONV_PAD, LANES), F32), pltpu.VMEM((n_ct, seq, LANES), F32)],
        compiler_params=_cparams("parallel"),
        name="conv_prompt",
    )(glu, w_tiles, b_dw, lng, lnb)


def _conv_sample_kernel(st_ref, glu_ref, w_ref, b_ref, lng_ref, lnb_ref, y_ref, st_out_ref):
    n_prev = CONV_WIDTH - 1
    glu = glu_ref[...]
    acc = w_ref[n_prev:n_prev + 1, :] * glu
    for j in range(n_prev):
        acc = acc + w_ref[j:j + 1, :] * st_ref[j]
    y_ref[...] = _ln_swish(acc + b_ref[...], lng_ref[...], lnb_ref[...])
    st_out_ref[0:n_prev - 1] = st_ref[1:n_prev]
    st_out_ref[n_prev - 1] = glu


def _conv_sample(state_t, glu, w_dw, b_dw, lng, lnb):
    nb = glu.shape[0]
    return pl.pallas_call(
        _conv_sample_kernel,
        grid=(1,),
        in_specs=[_const_spec(state_t.shape), _const_spec((nb, C_CONV)), _const_spec((CONV_WIDTH, C_CONV))]
        + [_const_spec((1, C_CONV))] * 3,
        out_specs=[_const_spec((nb, C_CONV)), _const_spec(state_t.shape)],
        out_shape=[jax.ShapeDtypeStruct((nb, C_CONV), F32), jax.ShapeDtypeStruct(state_t.shape, F32)],
        compiler_params=_cparams("arbitrary"),
        name="conv_sample",
    )(state_t, glu, w_dw, b_dw, lng, lnb)


def _head_rows(q, kvh):
    rows = q.shape[0]
    half = _lane_iota((rows, KV_W)) // HEAD_DIM
    out = []
    for g in range(GROUP):
        h = kvh * GROUP + g
        slab = q[:, KV_W * (h // 2):KV_W * (h // 2 + 1)]
        if h % 2 != kvh:
            slab = pltpu.roll(slab, HEAD_DIM, 1)
        out.append(jnp.where(half == kvh, slab * (ATTN_SCALE * LOG2E), 0.0))
    return out


def _pool_matrix(w_cmp, n_keys):
    eye = jnp.eye(n_keys // CMP_BLOCK, LANES // N_KV_HEADS, dtype=F32)
    per_head = [(eye[:, None, :] * w_cmp[None, :, h, None]).reshape(n_keys, -1) for h in range(N_KV_HEADS)]
    return jnp.concatenate(per_head, axis=1).astype(BF16)


def _pool_blocks_t(x_t, w_pool):
    res = jnp.dot(x_t, w_pool, preferred_element_type=F32)
    row = lax.broadcasted_iota(jnp.int32, res.shape, 0)
    own = jnp.where(row < HEAD_DIM, res, pltpu.roll(res, HEAD_DIM, 1))
    return jnp.where(_lane_iota(res.shape) < w_pool.shape[0] // CMP_BLOCK, own, 0.0)


def _select_blocks(psum, q_pos, n_blk):
    shape = psum.shape
    lane = _lane_iota(shape)
    even = (lane & 1) == 0
    pair = psum + jnp.where(even, pltpu.roll(psum, LANES - 1, 1), pltpu.roll(psum, 1, 1))
    blk = lane // (SEL_BLOCK // CMP_BLOCK)
    back = q_pos // SEL_BLOCK - blk
    forced = (blk == 0) | ((back >= 0) & (back < N_LOCAL))
    score = jnp.where(back >= 0, pair + jnp.where(forced, FORCE_BONUS, 0.0), -jnp.inf)
    rank = jnp.zeros(shape, jnp.int32)
    for i in range(n_blk):
        si = score[:, 2 * i:2 * i + 1]
        beats = (si > score) | ((si == score) & (i < blk))
        rank = rank + beats.astype(jnp.int32)
    return jnp.where((rank < min(TOP_N, n_blk)) & even, 1.0, 0.0)


def _select_bias_t(p_blks, q_pos):
    n_blk = p_blks[0].shape[0]
    blk = lax.broadcasted_iota(jnp.int32, p_blks[0].shape, 0)
    back = q_pos // SEL_BLOCK - blk
    forced = (blk == 0) | ((back >= 0) & (back < N_LOCAL))
    bonus = jnp.where(forced, FORCE_BONUS, 0.0)
    scores = [jnp.where(back >= 0, p + bonus, -jnp.inf) for p in p_blks]
    ranks = [jnp.zeros(blk.shape, jnp.int32) for _ in p_blks]
    for i in range(n_blk):
        for k, score in enumerate(scores):
            si = score[i:i + 1, :]
            beats = (si > score) | ((si == score) & (i < blk))
            ranks[k] = ranks[k] + beats.astype(jnp.int32)
    return [jnp.where((rank < min(TOP_N, n_blk)) & (back >= 0), 0.0, NEG_INF) for rank in ranks]


def _block_expander(n_keys, row_stride):
    li = lax.broadcasted_iota(jnp.int32, (LANES, n_keys), 0)
    ti = lax.broadcasted_iota(jnp.int32, (LANES, n_keys), 1)
    return jnp.where(li == row_stride * (ti // SEL_BLOCK), 1.0, 0.0).astype(BF16)


def _gate_cols(gates, heads):
    return [jnp.concatenate([gates[:, 3 * h + k:3 * h + k + 1] for h in heads], axis=0) for k in range(3)]


def _place_heads(o_heads):
    rows = o_heads[0].shape[0]
    low = _lane_iota((rows, KV_W)) < HEAD_DIM
    slabs = []
    for s in range(N_HEADS // 2):
        pair = []
        for h in (2 * s, 2 * s + 1):
            o = o_heads[h]
            if h // GROUP != h % 2:
                o = pltpu.roll(o, HEAD_DIM, 1)
            pair.append(o)
        slabs.append(jnp.where(low, pair[0], pair[1]))
    return jnp.concatenate(slabs, axis=1)


def _attn_prompt_kernel(q_ref, kv_ref, kvw_ref, gate_ref, wk_ref, wv_ref, out_ref,
                        kc_ref, vc_ref, ks_ref, vs_ref, kw_ref, vw_ref, exp_ref,
                        bias_ref, s_ref, mrun_ref, mb_ref, lrun_ref, acc_ref, psum_ref, oct_ref, ow_ref):
    seq = kv_ref.shape[2]
    n_blk = seq // SEL_BLOCK
    n_kt = seq // KT
    n_wt = WIN_KEYS // KT
    rows = GROUP * QB
    qi = pl.program_id(1)
    qs = qi * QB

    @pl.when((pl.program_id(0) == 0) & (qi == 0))
    def _():
        exp_ref[...] = _block_expander(seq, 1)

    @pl.when(qi == 0)
    def _():
        kc_ref[...] = _pool_blocks_t(kv_ref[0, 0:KV_W, :].astype(BF16), wk_ref[...]).T.astype(BF16)
        vc_ref[...] = _pool_blocks_t(kv_ref[0, KV_W:2 * KV_W, :].astype(BF16), wv_ref[...]).astype(BF16)
        for t in range(n_kt):
            ks_ref[t] = kv_ref[0, 2 * KV_W:3 * KV_W, t * KT:(t + 1) * KT].astype(BF16)
            vs_ref[t] = kv_ref[0, 3 * KV_W:4 * KV_W, t * KT:(t + 1) * KT].astype(BF16)
            kw_ref[t] = kvw_ref[0, 0:KV_W, t * KT:(t + 1) * KT].astype(BF16)
            vw_ref[t] = kvw_ref[0, KV_W:2 * KV_W, t * KT:(t + 1) * KT].astype(BF16)

    q = q_ref[0]
    q_pos = qs + lax.broadcasted_iota(jnp.int32, (QB, 1), 0)
    q_pos_t = qs + lax.broadcasted_iota(jnp.int32, (1, QB), 1)
    add_bias = lambda s, b: (s.reshape(GROUP, QB, s.shape[-1]) + b[None]).reshape(s.shape)
    n_tiles = (qs + QB + KT - 1) // KT
    qp = [jnp.concatenate(_head_rows(q, kvh), axis=0).astype(BF16) for kvh in range(N_KV_HEADS)]

    win_t0 = jnp.maximum(qs - WINDOW, 0) // KT
    rel = q_pos - (win_t0 * KT + _lane_iota((QB, WIN_KEYS)))
    win_bias = jnp.where((rel >= 0) & (rel <= WINDOW), 0.0, NEG_INF)
    kv_heads = range(N_KV_HEADS)
    gates = [_gate_cols(gate_ref[0], [kvh * GROUP + g for g in range(GROUP)]) for kvh in kv_heads]
    s_w = [add_bias(jnp.concatenate([jnp.dot(qp[kvh], kw_ref[win_t0 + c], preferred_element_type=F32)
                                     for c in range(n_wt)], axis=1), win_bias) for kvh in kv_heads]
    p_w = [jnp.exp2(s_w[kvh] - jnp.max(s_w[kvh], axis=-1, keepdims=True)) for kvh in kv_heads]
    for kvh in kv_heads:
        o_win = sum(_dot_nt(p_w[kvh][:, c * KT:(c + 1) * KT].astype(BF16), vw_ref[win_t0 + c]) for c in range(n_wt))
        ow_ref[kvh] = o_win * (gates[kvh][2] / jnp.sum(p_w[kvh], axis=-1, keepdims=True))

    c_end_t = (lax.broadcasted_iota(jnp.int32, (LANES, rows), 0) + 1) * CMP_BLOCK - 1
    ok_t = c_end_t <= jnp.concatenate([q_pos_t] * GROUP, axis=1)
    gate_t = gate_ref[0].T
    per_sel = SEL_BLOCK // CMP_BLOCK
    o_cmp, p_blk = [], []
    for kvh in kv_heads:
        s_t = jnp.where(ok_t, _dot_nt(kc_ref[...], qp[kvh]), NEG_INF)
        p_t = jnp.where(ok_t, jnp.exp2(s_t - jnp.max(s_t, axis=0, keepdims=True)), 0.0)
        l_t = jnp.sum(p_t, axis=0, keepdims=True)
        p_t = p_t * jnp.where(l_t > 0.0, 1.0 / l_t, 0.0)
        g_row = jnp.concatenate([gate_t[3 * (kvh * GROUP + g):3 * (kvh * GROUP + g) + 1] for g in range(GROUP)],
                                axis=1)
        oct_ref[kvh] = jnp.dot(vc_ref[...], (p_t * g_row).astype(BF16), preferred_element_type=F32)
        o_cmp.append(oct_ref[kvh].T)
        psum = sum(p_t[:, g * QB:(g + 1) * QB] for g in range(GROUP))
        tiles = []
        for c in range(QB // LANES):
            psum_ref[kvh, c] = psum[:, c * LANES:(c + 1) * LANES]
            tiles.append(sum(psum_ref[kvh, c, pl.ds(j, n_blk, stride=per_sel), :] for j in range(per_sel)))
        p_blk.append(jnp.concatenate(tiles, axis=1))
    blk_bias = _select_bias_t(p_blk, q_pos_t)
    last = n_tiles - 1
    diag_bias = jnp.where(last * KT + _lane_iota((QB, KT)) <= q_pos, 0.0, NEG_INF)
    for kvh in kv_heads:
        bb = jnp.concatenate([blk_bias[kvh], jnp.zeros((LANES - n_blk, QB), F32)], axis=0).T
        key_bias = jnp.dot(bb.astype(BF16), exp_ref[...], preferred_element_type=F32)
        for t in range(n_kt):
            bias_ref[kvh, t] = key_bias[:, t * KT:(t + 1) * KT]
        bias_ref[kvh, last] += diag_bias

    mrun_ref[...] = jnp.full(mrun_ref.shape, NEG_INF, F32)

    def tile_pairs(tiles_fn):
        def pair(u, carry):
            tiles_fn(2 * u, 2)
            return carry

        lax.fori_loop(0, n_tiles // 2, pair, 0)

        @pl.when(n_tiles % 2 == 1)
        def _():
            tiles_fn(last, 1)

    def max_tiles(t0, n):
        raw = [[jnp.dot(qp[kvh], ks_ref[t0 + i], preferred_element_type=F32) for kvh in kv_heads]
               for i in range(n)]
        for kvh in kv_heads:
            m = None
            for i in range(n):
                s = add_bias(raw[i][kvh], bias_ref[kvh, t0 + i])
                s_ref[kvh, t0 + i] = s
                m_i = jnp.maximum(s[:, :LANES], s[:, LANES:])
                m = m_i if m is None else jnp.maximum(m, m_i)
            mrun_ref[kvh] = jnp.maximum(mrun_ref[kvh], m)

    tile_pairs(max_tiles)
    for kvh in kv_heads:
        mb_ref[kvh] = jnp.broadcast_to(jnp.max(mrun_ref[kvh], axis=-1, keepdims=True), (rows, LANES))
    lrun_ref[...] = jnp.zeros(lrun_ref.shape, F32)
    acc_ref[...] = jnp.zeros(acc_ref.shape, F32)

    def sum_tiles(t0, n):
        for kvh in kv_heads:
            mb = mb_ref[kvh]
            l_add, o_add = None, None
            for i in range(n):
                s = s_ref[kvh, t0 + i]
                p = jnp.exp2(jnp.concatenate([s[:, :LANES] - mb, s[:, LANES:] - mb], axis=1))
                l_i = p[:, :LANES] + p[:, LANES:]
                o_i = _dot_nt(p.astype(BF16), vs_ref[t0 + i])
                l_add = l_i if l_add is None else l_add + l_i
                o_add = o_i if o_add is None else o_add + o_i
            lrun_ref[kvh] += l_add
            acc_ref[kvh] += o_add

    tile_pairs(sum_tiles)

    o_heads = []
    for kvh in kv_heads:
        l = jnp.sum(lrun_ref[kvh], axis=-1, keepdims=True)
        o = o_cmp[kvh] + acc_ref[kvh] * (jnp.where(l > 0.0, 1.0 / l, 0.0) * gates[kvh][1]) + ow_ref[kvh]
        o_heads += [o[g * QB:(g + 1) * QB] for g in range(GROUP)]

    out_ref[0] = _place_heads(o_heads)


def _attn_prompt(q, kv_t, kvw_t, gate, w_pool_k, w_pool_v):
    nb, seq, _ = q.shape
    per_q = lambda w: pl.BlockSpec((1, QB, w), lambda b, i: (b, i, 0))
    per_b = lambda w: pl.BlockSpec((1, w, seq), lambda b, i: (b, 0, 0))
    wspec = pl.BlockSpec((seq, LANES), lambda b, i: (0, 0))
    rows = GROUP * QB
    return pl.pallas_call(
        _attn_prompt_kernel,
        grid=(nb, seq // QB),
        in_specs=[per_q(N_HEADS * HEAD_DIM), per_b(N_PAGED_SLOTS * KV_W), per_b(2 * KV_W), per_q(LANES), wspec, wspec],
        out_specs=per_q(N_HEADS * HEAD_DIM),
        out_shape=jax.ShapeDtypeStruct(q.shape, F32),
        scratch_shapes=[pltpu.VMEM((LANES, KV_W), BF16), pltpu.VMEM((LANES, KV_W), BF16)]
        + [pltpu.VMEM((seq // KT, KV_W, KT), BF16)] * 4
        + [pltpu.VMEM((LANES, seq), BF16), pltpu.VMEM((N_KV_HEADS, seq // KT, QB, KT), F32),
           pltpu.VMEM((N_KV_HEADS, seq // KT, rows, KT), F32)]
        + [pltpu.VMEM((N_KV_HEADS, rows, LANES), F32)] * 4
        + [pltpu.VMEM((N_KV_HEADS, QB // LANES, LANES, LANES), F32), pltpu.VMEM((N_KV_HEADS, LANES, rows), F32),
           pltpu.VMEM((N_KV_HEADS, rows, LANES), F32)],
        compiler_params=_cparams("arbitrary", "arbitrary"),
        name="attn_prompt",
    )(q, kv_t, kvw_t, gate, w_pool_k, w_pool_v)


def _attn_sample_kernel(pt_ref, q_ref, kvn_ref, kvwn_ref, gate_ref, wk_ref, wv_ref, kvwt_ref, cache_ref, win_ref,
                        out_ref, win_out_ref, past_ref, exp_ref, sem):
    step = pl.program_id(0)
    n_steps = pl.num_programs(0)
    n_pages = pt_ref.shape[1]
    page = cache_ref.shape[2]
    past_len = n_pages * page
    n_blk = past_len // SEL_BLOCK + 1
    w_buf = win_ref.shape[2]
    slot = step % 2

    def page_copy(st, sl, g, p):
        return pltpu.make_async_copy(cache_ref.at[pt_ref[st * DEC_GROUP + g, p]],
                                     past_ref.at[sl, g, :, pl.ds(p * page, page)], sem.at[sl, g, p])

    def start_pages(st, sl):
        for g in range(DEC_GROUP):
            for p in range(n_pages):
                page_copy(st, sl, g, p).start()

    @pl.when(step == 0)
    def _():
        start_pages(0, 0)
        exp_ref[...] = _block_expander(past_len, SEL_BLOCK // CMP_BLOCK)

    @pl.when(step + 1 < n_steps)
    def _():
        start_pages(step + 1, 1 - slot)

    q_pos = jnp.full((N_HEADS, 1), past_len, jnp.int32)

    def new_key_softmax(qp_f, s_past, ok_past, vt_past, k_new, v_new, ok_new):
        s_new = jnp.sum(qp_f * k_new, axis=-1, keepdims=True)
        s_past = jnp.where(ok_past, s_past, NEG_INF)
        s_new = jnp.where(ok_new, s_new, NEG_INF)
        m = jnp.maximum(jnp.max(s_past, axis=-1, keepdims=True), s_new)
        p_past = jnp.where(ok_past, jnp.exp2(s_past - m), 0.0)
        p_new = jnp.where(ok_new, jnp.exp2(s_new - m), 0.0)
        l = jnp.sum(p_past, axis=-1, keepdims=True) + p_new
        o = _dot_nt(p_past.astype(BF16), vt_past) + p_new * v_new
        return o * jnp.where(l > 0.0, 1.0 / l, 0.0)

    qp_f, o_win = [], []
    for g in range(DEC_GROUP):
        b = step * DEC_GROUP + g
        qf = jnp.concatenate([r for kvh in range(N_KV_HEADS) for r in _head_rows(q_ref[g], kvh)], axis=0)
        qp_f.append(qf)
        kvw_new = kvwn_ref[g]
        win = win_ref[g]
        rel = past_len - (past_len - w_buf + _lane_iota((N_HEADS, w_buf)))
        o_win.append(new_key_softmax(qf, jnp.dot(qf.astype(BF16), win[0:KV_W].astype(BF16),
                                                 preferred_element_type=F32),
                                     (rel >= 0) & (rel <= WINDOW), win[KV_W:2 * KV_W].astype(BF16),
                                     kvw_new[:, 0:KV_W], kvw_new[:, KV_W:2 * KV_W], jnp.full((N_HEADS, 1), True)))
        new_col = jnp.sum(jnp.where(_lane_iota(kvwt_ref.shape) == b, kvwt_ref[...], 0.0), axis=1, keepdims=True)
        win_out_ref[g] = jnp.where(_lane_iota(win.shape) == w_buf - 1, new_col, pltpu.roll(win, w_buf - 1, 1))

    for g in range(DEC_GROUP):
        for p in range(n_pages):
            page_copy(step, slot, g, p).wait()

    def slab(g, i):
        return past_ref[slot, g, i * KV_W:(i + 1) * KV_W, :].astype(BF16)

    elems = range(DEC_GROUP)
    qp = [qp_f[g].astype(BF16) for g in elems]
    kc_t = [_pool_blocks_t(slab(g, 0), wk_ref[...]).astype(BF16) for g in elems]
    vc_t = [_pool_blocks_t(slab(g, 1), wv_ref[...]).astype(BF16) for g in elems]
    s_sel = [jnp.dot(qp[g], slab(g, 2), preferred_element_type=F32) for g in elems]
    c_ok = (_lane_iota((N_HEADS, LANES)) + 1) * CMP_BLOCK - 1 <= q_pos
    o_cmp, sel = [], []
    for g in elems:
        s_c = jnp.where(c_ok, jnp.dot(qp[g], kc_t[g], preferred_element_type=F32), NEG_INF)
        p_c = jnp.where(c_ok, jnp.exp2(s_c - jnp.max(s_c, axis=-1, keepdims=True)), 0.0)
        l_c = jnp.sum(p_c, axis=-1, keepdims=True)
        inv_c = jnp.where(l_c > 0.0, 1.0 / l_c, 0.0)
        o_cmp.append(_dot_nt(p_c.astype(BF16), vc_t[g]) * inv_c)
        p_cmp = p_c * inv_c
        psum = jnp.concatenate(
            [jnp.broadcast_to(jnp.sum(p_cmp[kvh * GROUP:(kvh + 1) * GROUP], axis=0, keepdims=True), (GROUP, LANES))
             for kvh in range(N_KV_HEADS)], axis=0)
        sel.append(_select_blocks(psum, q_pos, n_blk))

    new_blk = past_len // SEL_BLOCK
    for g in elems:
        kv_new = kvn_ref[g]
        selexp = jnp.dot(sel[g].astype(BF16), exp_ref[...], preferred_element_type=F32)
        ok_new = sel[g][:, 2 * new_blk:2 * new_blk + 1] > 0.5
        ok_past = (selexp > 0.5) & (_lane_iota((N_HEADS, past_len)) <= q_pos)
        o_sel = new_key_softmax(qp_f[g], s_sel[g], ok_past, slab(g, 3),
                                kv_new[:, 2 * KV_W:3 * KV_W], kv_new[:, 3 * KV_W:4 * KV_W], ok_new)
        g_cmp, g_sel, g_win = _gate_cols(gate_ref[g], range(N_HEADS))
        o = o_cmp[g] * g_cmp + o_sel * g_sel + o_win[g] * g_win
        out_ref[g] = _place_heads([o[h:h + 1] for h in range(N_HEADS)])


def _attn_sample(page_table, q, kv_new, kvw_new, gate, w_pool_k, w_pool_v, cache_t, win_t):
    nb, n_pages = page_table.shape
    n_feat, page = cache_t.shape[1], cache_t.shape[2]
    w_buf = win_t.shape[2]
    past_len = n_pages * page
    assert nb % DEC_GROUP == 0
    row3 = lambda w: pl.BlockSpec((DEC_GROUP, 1, w), lambda b, pt: (b, 0, 0))
    full = lambda shape: pl.BlockSpec(shape, lambda b, pt: (0,) * len(shape))
    win_spec = pl.BlockSpec((DEC_GROUP, 2 * KV_W, w_buf), lambda b, pt: (b, 0, 0))
    grid_spec = pltpu.PrefetchScalarGridSpec(
        num_scalar_prefetch=1,
        grid=(nb // DEC_GROUP,),
        in_specs=[row3(N_HEADS * HEAD_DIM), row3(N_PAGED_SLOTS * KV_W), row3(2 * KV_W), row3(LANES),
                  full((past_len, LANES)), full((past_len, LANES)), full((2 * KV_W, nb)),
                  pl.BlockSpec(memory_space=pl.ANY), win_spec],
        out_specs=[row3(N_HEADS * HEAD_DIM), win_spec],
        scratch_shapes=[pltpu.VMEM((2, DEC_GROUP, n_feat, past_len), F32),
                        pltpu.VMEM((LANES, past_len), BF16),
                        pltpu.SemaphoreType.DMA((2, DEC_GROUP, n_pages))],
    )
    r3 = lambda a: a.reshape(nb, 1, a.shape[-1])
    return pl.pallas_call(
        _attn_sample_kernel,
        grid_spec=grid_spec,
        out_shape=[jax.ShapeDtypeStruct((nb, 1, N_HEADS * HEAD_DIM), F32),
                   jax.ShapeDtypeStruct(win_t.shape, F32)],
        compiler_params=_cparams("arbitrary"),
        name="attn_sample",
    )(page_table, r3(q), r3(kv_new), r3(kvw_new), r3(gate), w_pool_k, w_pool_v, kvw_new.T, cache_t, win_t)


def _rms(x, g):
    return x * lax.rsqrt(jnp.mean(x * x, axis=-1, keepdims=True) + RMS_EPS) * g


def _mlp_kernel(x_ref, conv_ref, attn_ref, wout_ref, gm_ref, wup_ref, wdn_ref, gf_ref, y_ref, h_ref):
    h_ref[...] = (x_ref[...]
                  + jnp.dot(conv_ref[...].astype(BF16), wout_ref[0:C_CONV, :], preferred_element_type=F32)
                  + jnp.dot(attn_ref[...].astype(BF16), wout_ref[C_CONV:, :], preferred_element_type=F32))
    hn = _rms(h_ref[...], gm_ref[...]).astype(BF16)
    acc = None
    for c in range(D_FF // FF_CHUNK):
        m = jnp.dot(hn, wup_ref[:, c * FF_CHUNK:(c + 1) * FF_CHUNK], preferred_element_type=F32)
        m = jnp.maximum(m, 0.0)
        t = jnp.dot((m * m).astype(BF16), wdn_ref[c * FF_CHUNK:(c + 1) * FF_CHUNK, :], preferred_element_type=F32)
        acc = t if acc is None else acc + t
    y_ref[...] = _rms(h_ref[...] + acc, gf_ref[...])


def _mlp(x, conv_y, attn_y, w_out, g_mlp, w_up, w_down, g_final, tm):
    n = x.shape[0]
    row = lambda w: pl.BlockSpec((tm, w), lambda i: (i, 0))
    once = lambda shape: pl.BlockSpec(shape, lambda i: (0,) * len(shape), pipeline_mode=pl.Buffered(1))
    return pl.pallas_call(
        _mlp_kernel,
        grid=(n // tm,),
        in_specs=[row(D_MODEL), row(C_CONV), row(N_HEADS * HEAD_DIM), once((D_MODEL, D_MODEL)), once((1, D_MODEL)),
                  once((D_MODEL, D_FF)), once((D_FF, D_MODEL)), once((1, D_MODEL))],
        out_specs=row(D_MODEL),
        out_shape=jax.ShapeDtypeStruct((n, D_MODEL), F32),
        scratch_shapes=[pltpu.VMEM((tm, D_MODEL), F32)],
        compiler_params=_cparams("parallel"),
        name="mlp",
    )(x, conv_y, attn_y, w_out, g_mlp, w_up, w_down, g_final)


def kernel(x_prompt, x_sample, cache_kv, cache_win, state_conv, page_table, g_attn_norm, w_in, w_dw, b_dw,
           conv_ln_g, conv_ln_b, w_cmp_k, w_cmp_v, w_out, g_mlp_norm, w_up, w_down, g_final):
    depth = w_in.shape[0]
    assert depth == 1, "single-layer trunk"
    nb, seq, _ = x_prompt.shape
    db, dseq, _ = x_sample.shape
    assert dseq == 1
    n_phys, page = cache_kv.shape[1], cache_kv.shape[2]
    w_buf = cache_win.shape[2]

    w_main = jnp.concatenate([w_in[0][:, :COL_KV], w_in[0][:, COL_GATE:],
                              jnp.zeros((D_MODEL, IN_COLS_PAD - IN_COLS), F32)], axis=1).astype(BF16)
    w_kv = jnp.transpose(w_in[0][:, COL_KV:COL_GATE]).astype(BF16)
    w_out_b, w_up_b, w_down_b = w_out[0].astype(BF16), w_up[0].astype(BF16), w_down[0].astype(BF16)
    row = lambda a: a.reshape(1, -1)
    g_attn, g_mlp, g_fin = row(g_attn_norm[0]), row(g_mlp_norm[0]), row(g_final)
    conv_w = (w_dw[0], row(b_dw[0]), row(conv_ln_g[0]), row(conv_ln_b[0]))
    mlp_w = (w_out_b, g_mlp, w_up_b, w_down_b, g_fin)

    xp = x_prompt.reshape(nb * seq, D_MODEL)
    glu, q, gate, kv_t, kvw_t = _project(xp, g_attn, w_main, w_kv, PROJ_ROWS, seq)
    per_b = lambda a: a.reshape(nb, seq, a.shape[-1])
    glu3 = per_b(glu)
    conv_y = _conv_prompt(glu3, *conv_w)
    attn_y = _attn_prompt(per_b(q), kv_t, kvw_t, per_b(gate), _pool_matrix(w_cmp_k[0], seq),
                          _pool_matrix(w_cmp_v[0], seq))
    y_prompt = _mlp(xp, conv_y.reshape(nb * seq, C_CONV), attn_y.reshape(nb * seq, -1), *mlp_w, MLP_ROWS)
    pos_major = lambda a, n_slots: jnp.transpose(
        a.reshape(a.shape[0], n_slots, N_KV_HEADS, HEAD_DIM, a.shape[-1]), (0, 4, 1, 2, 3))[None]
    new_kv_p = pos_major(kv_t, N_PAGED_SLOTS)
    new_win_p = pos_major(kvw_t[:, :, seq - min(WINDOW, seq):], 2)
    new_conv_p = glu3[:, seq - (CONV_WIDTH - 1):][None]

    xs = x_sample.reshape(db, D_MODEL)
    glu_s, q_s, gate_s, kv_s, kvw_s = _project(xs, g_attn, w_main, w_kv, db)
    conv_y_s, conv_st = _conv_sample(jnp.transpose(state_conv[0], (1, 0, 2)), glu_s, *conv_w)
    cache_t = jnp.transpose(cache_kv[0], (0, 2, 3, 4, 1)).reshape(n_phys, N_PAGED_SLOTS * KV_W, page)
    win_t = jnp.transpose(cache_win[0], (0, 2, 3, 4, 1)).reshape(db, 2 * KV_W, w_buf)
    n_past = page_table.shape[1] * page
    attn_s, new_win_t = _attn_sample(page_table, q_s, kv_s, kvw_s, gate_s, _pool_matrix(w_cmp_k[0], n_past),
                                     _pool_matrix(w_cmp_v[0], n_past), cache_t, win_t)
    new_win_s = pos_major(new_win_t, 2)
    y_sample = _mlp(xs, conv_y_s, attn_s.reshape(db, -1), *mlp_w, db)

    return (y_prompt.reshape(nb, seq, D_MODEL), y_sample.reshape(db, 1, D_MODEL), new_kv_p, new_win_p, new_conv_p,
            kv_s.reshape(1, db, 1, N_PAGED_SLOTS, N_KV_HEADS, HEAD_DIM), new_win_s,
            jnp.transpose(conv_st, (1, 0, 2))[None])
```

```python
import functools

import jax
import jax.numpy as jnp
from jax import lax
from jax.experimental import pallas as pl
from jax.experimental.pallas import tpu as pltpu

F32 = jnp.float32
BF16 = jnp.bfloat16

D_MODEL = 1024
C_CONV = 512
CONV_WIDTH = 31
N_HEADS = 8
HEAD_DIM = 64
N_KV_HEADS = 2
GROUP = N_HEADS // N_KV_HEADS
KV_W = N_KV_HEADS * HEAD_DIM
N_PAGED_SLOTS = 4
D_FF = 4 * D_MODEL
CMP_BLOCK = 32
SEL_BLOCK = 64
TOP_N = 16
N_LOCAL = 2
WINDOW = 512
FORCE_BONUS = float(GROUP + 1)
NEG_INF = -1e30
RMS_EPS = 1e-6
LN_EPS = 1e-5
ATTN_SCALE = HEAD_DIM ** -0.5
LOG2E = 1.4426950408889634

COL_Q = 2 * C_CONV
COL_KV = COL_Q + N_HEADS * HEAD_DIM
COL_WIN = COL_KV + N_PAGED_SLOTS * KV_W
COL_GATE = COL_WIN + 2 * KV_W
IN_COLS = COL_GATE + 3 * N_HEADS
LANES = 128
SUBLANES = 8
IN_COLS_PAD = -(-IN_COLS // LANES) * LANES

VMEM_LIMIT = 56 * 1024 * 1024

PROJ_ROWS = 1024
MLP_ROWS = 512
CONV_ROWS = 128
CONV_PAD = 32
NORM_ROWS = 256
QB = 256
KT = 256
WIN_KEYS = WINDOW + QB
assert QB % KT == 0 and WIN_KEYS % KT == 0
DEC_GROUP = 4
FF_CHUNK = 1024


def _cparams(*sem):
    return pltpu.CompilerParams(dimension_semantics=sem, vmem_limit_bytes=VMEM_LIMIT)


def _const_spec(shape):
    return pl.BlockSpec(shape, lambda *_: (0,) * len(shape))


def _dot_nt(a, b):
    return lax.dot_general(a, b, (((1,), (1,)), ((), ())), preferred_element_type=F32)


def _lane_iota(shape):
    return lax.broadcasted_iota(jnp.int32, shape, len(shape) - 1)


def _proj_kernel(x_ref, g_ref, w_ref, wkv_ref, glu_ref, q_ref, gate_ref, kv_ref, kvw_ref, *, kv_t):
    x = x_ref[...]
    ms = jnp.mean(x * x, axis=-1, keepdims=True)
    xn = (x * lax.rsqrt(ms + RMS_EPS) * g_ref[...]).astype(BF16)
    z = jnp.dot(xn, w_ref[...], preferred_element_type=F32)
    glu_ref[...] = z[:, :C_CONV] * jax.nn.sigmoid(z[:, C_CONV:COL_Q])
    q_ref[...] = z[:, COL_Q:COL_KV]
    gate_ref[...] = jax.nn.sigmoid(z[:, COL_KV:])
    n_paged = N_PAGED_SLOTS * KV_W
    if kv_t:
        kv = _dot_nt(wkv_ref[...], xn)
        kv_ref[0] = kv[:n_paged]
        kvw_ref[0] = kv[n_paged:]
    else:
        kv = _dot_nt(xn, wkv_ref[...])
        kv_ref[...] = kv[:, :n_paged]
        kvw_ref[...] = kv[:, n_paged:]


def _project(x, g, w_main, w_kv, tm, seq=None):
    n = x.shape[0]
    row = lambda w: pl.BlockSpec((tm, w), lambda i: (i, 0))
    n_paged = N_PAGED_SLOTS * KV_W
    widths = (C_CONV, N_HEADS * HEAD_DIM, LANES)
    out_specs = [row(w) for w in widths]
    out_shape = [jax.ShapeDtypeStruct((n, w), F32) for w in widths]
    for w in (n_paged, 2 * KV_W):
        if seq is None:
            out_specs.append(row(w))
            out_shape.append(jax.ShapeDtypeStruct((n, w), F32))
        else:
            per_seq = seq // tm
            out_specs.append(pl.BlockSpec((1, w, tm), lambda i: (i // per_seq, 0, i % per_seq)))
            out_shape.append(jax.ShapeDtypeStruct((n // seq, w, seq), F32))
    return pl.pallas_call(
        functools.partial(_proj_kernel, kv_t=seq is not None),
        grid=(n // tm,),
        in_specs=[row(D_MODEL), _const_spec((1, D_MODEL)), _const_spec(w_main.shape), _const_spec(w_kv.shape)],
        out_specs=out_specs,
        out_shape=out_shape,
        compiler_params=_cparams("parallel"),
        name="projection",
    )(x, g, w_main, w_kv)


def _ln_swish(y, lng, lnb):
    mu = jnp.mean(y, axis=-1, keepdims=True)
    d = y - mu
    var = jnp.mean(d * d, axis=-1, keepdims=True)
    y = d * lax.rsqrt(var + LN_EPS) * lng + lnb
    return y * jax.nn.sigmoid(y)


def _conv_prompt_kernel(glu_ref, w_ref, b_ref, lng_ref, lnb_ref, y_ref, ext_ref, stage_ref):
    seq = glu_ref.shape[1]
    n_ct = C_CONV // LANES
    for ct in range(n_ct):
        ext_ref[ct, 0:CONV_PAD, :] = jnp.zeros((CONV_PAD, LANES), F32)
        ext_ref[ct, CONV_PAD:, :] = glu_ref[0, :, ct * LANES:(ct + 1) * LANES]
    shift = CONV_PAD - (CONV_WIDTH - 1)

    def body(c, carry):
        base = pl.multiple_of(c * CONV_ROWS, CONV_ROWS)

        def tile(ct, carry_ct):
            win = ext_ref[ct, pl.ds(base, CONV_ROWS + CONV_PAD), :]
            acc = jnp.zeros((CONV_ROWS, LANES), F32)
            for r in range(SUBLANES):
                shifted = win if r == 0 else pltpu.roll(win, CONV_ROWS + CONV_PAD - r, 0)
                for j in range(CONV_WIDTH):
                    if (j + shift) % SUBLANES == r:
                        a = j + shift - r
                        acc = acc + w_ref[ct, j:j + 1, :] * shifted[a:a + CONV_ROWS]
            stage_ref[ct, pl.ds(base, CONV_ROWS), :] = acc
            return carry_ct

        lax.fori_loop(0, n_ct, tile, 0)
        return carry

    lax.fori_loop(0, seq // CONV_ROWS, body, 0)

    def norm(c, carry):
        base = pl.multiple_of(c * NORM_ROWS, NORM_ROWS)
        y = jnp.concatenate([stage_ref[ct, pl.ds(base, NORM_ROWS), :] for ct in range(n_ct)], axis=1) + b_ref[...]
        y_ref[0, pl.ds(base, NORM_ROWS), :] = _ln_swish(y, lng_ref[...], lnb_ref[...])
        return carry

    lax.fori_loop(0, seq // NORM_ROWS, norm, 0)


def _conv_prompt(glu, w_dw, b_dw, lng, lnb):
    nb, seq, _ = glu.shape
    n_ct = C_CONV // LANES
    w_tiles = jnp.transpose(w_dw.reshape(CONV_WIDTH, n_ct, LANES), (1, 0, 2))
    blk = pl.BlockSpec((1, seq, C_CONV), lambda b: (b, 0, 0))
    return pl.pallas_call(
        _conv_prompt_kernel,
        grid=(nb,),
        in_specs=[blk, _const_spec((n_ct, CONV_WIDTH, LANES))] + [_const_spec((1, C_CONV))] * 3,
        out_specs=blk,
        out_shape=jax.ShapeDtypeStruct(glu.shape, F32),
        scratch_shapes=[pltpu.VMEM((n_ct, seq + CONV_PAD, LANES), F32), pltpu.VMEM((n_ct, seq, LANES), F32)],
        compiler_params=_cparams("parallel"),
        name="conv_prompt",
    )(glu, w_tiles, b_dw, lng, lnb)


def _conv_sample_kernel(st_ref, glu_ref, w_ref, b_ref, lng_ref, lnb_ref, y_ref, st_out_ref):
    n_prev = CONV_WIDTH - 1
    glu = glu_ref[...]
    acc = w_ref[n_prev:n_prev + 1, :] * glu
    for j in range(n_prev):
        acc = acc + w_ref[j:j + 1, :] * st_ref[j]
    y_ref[...] = _ln_swish(acc + b_ref[...], lng_ref[...], lnb_ref[...])
    st_out_ref[0:n_prev - 1] = st_ref[1:n_prev]
    st_out_ref[n_prev - 1] = glu


def _conv_sample(state_t, glu, w_dw, b_dw, lng, lnb):
    nb = glu.shape[0]
    return pl.pallas_call(
        _conv_sample_kernel,
        grid=(1,),
        in_specs=[_const_spec(state_t.shape), _const_spec((nb, C_CONV)), _const_spec((CONV_WIDTH, C_CONV))]
        + [_const_spec((1, C_CONV))] * 3,
        out_specs=[_const_spec((nb, C_CONV)), _const_spec(state_t.shape)],
        out_shape=[jax.ShapeDtypeStruct((nb, C_CONV), F32), jax.ShapeDtypeStruct(state_t.shape, F32)],
        compiler_params=_cparams("arbitrary"),
        name="conv_sample",
    )(state_t, glu, w_dw, b_dw, lng, lnb)


def _head_rows(q, kvh):
    rows = q.shape[0]
    half = _lane_iota((rows, KV_W)) // HEAD_DIM
    out = []
    for g in range(GROUP):
        h = kvh * GROUP + g
        slab = q[:, KV_W * (h // 2):KV_W * (h // 2 + 1)]
        if h % 2 != kvh:
            slab = pltpu.roll(slab, HEAD_DIM, 1)
        out.append(jnp.where(half == kvh, slab * (ATTN_SCALE * LOG2E), 0.0))
    return out


def _pool_matrix(w_cmp, n_keys):
    eye = jnp.eye(n_keys // CMP_BLOCK, LANES // N_KV_HEADS, dtype=F32)
    per_head = [(eye[:, None, :] * w_cmp[None, :, h, None]).reshape(n_keys, -1) for h in range(N_KV_HEADS)]
    return jnp.concatenate(per_head, axis=1).astype(BF16)


def _pool_blocks_t(x_t, w_pool):
    res = jnp.dot(x_t, w_pool, preferred_element_type=F32)
    row = lax.broadcasted_iota(jnp.int32, res.shape, 0)
    own = jnp.where(row < HEAD_DIM, res, pltpu.roll(res, HEAD_DIM, 1))
    return jnp.where(_lane_iota(res.shape) < w_pool.shape[0] // CMP_BLOCK, own, 0.0)


def _select_blocks(psum, q_pos, n_blk):
    shape = psum.shape
    lane = _lane_iota(shape)
    even = (lane & 1) == 0
    pair = psum + jnp.where(even, pltpu.roll(psum, LANES - 1, 1), pltpu.roll(psum, 1, 1))
    blk = lane // (SEL_BLOCK // CMP_BLOCK)
    back = q_pos // SEL_BLOCK - blk
    forced = (blk == 0) | ((back >= 0) & (back < N_LOCAL))
    score = jnp.where(back >= 0, pair + jnp.where(forced, FORCE_BONUS, 0.0), -jnp.inf)
    rank = jnp.zeros(shape, jnp.int32)
    for i in range(n_blk):
        si = score[:, 2 * i:2 * i + 1]
        beats = (si > score) | ((si == score) & (i < blk))
        rank = rank + beats.astype(jnp.int32)
    return jnp.where((rank < min(TOP_N, n_blk)) & even, 1.0, 0.0)


def _select_bias_t(p_blks, q_pos):
    n_blk = p_blks[0].shape[0]
    blk = lax.broadcasted_iota(jnp.int32, p_blks[0].shape, 0)
    back = q_pos // SEL_BLOCK - blk
    forced = (blk == 0) | ((back >= 0) & (back < N_LOCAL))
    bonus = jnp.where(forced, FORCE_BONUS, 0.0)
    scores = [jnp.where(back >= 0, p + bonus, -jnp.inf) for p in p_blks]
    ranks = [jnp.zeros(blk.shape, jnp.int32) for _ in p_blks]
    for i in range(n_blk):
        for k, score in enumerate(scores):
            si = score[i:i + 1, :]
            beats = (si > score) | ((si == score) & (i < blk))
            ranks[k] = ranks[k] + beats.astype(jnp.int32)
    return [jnp.where((rank < min(TOP_N, n_blk)) & (back >= 0), 0.0, NEG_INF) for rank in ranks]


def _block_expander(n_keys, row_stride):
    li = lax.broadcasted_iota(jnp.int32, (LANES, n_keys), 0)
    ti = lax.broadcasted_iota(jnp.int32, (LANES, n_keys), 1)
    return jnp.where(li == row_stride * (ti // SEL_BLOCK), 1.0, 0.0).astype(BF16)


def _gate_cols(gates, heads):
    return [jnp.concatenate([gates[:, 3 * h + k:3 * h + k + 1] for h in heads], axis=0) for k in range(3)]


def _place_heads(o_heads):
    rows = o_heads[0].shape[0]
    low = _lane_iota((rows, KV_W)) < HEAD_DIM
    slabs = []
    for s in range(N_HEADS // 2):
        pair = []
        for h in (2 * s, 2 * s + 1):
            o = o_heads[h]
            if h // GROUP != h % 2:
                o = pltpu.roll(o, HEAD_DIM, 1)
            pair.append(o)
        slabs.append(jnp.where(low, pair[0], pair[1]))
    return jnp.concatenate(slabs, axis=1)


def _attn_prompt_kernel(q_ref, kv_ref, kvw_ref, gate_ref, wk_ref, wv_ref, out_ref,
                        kc_ref, vc_ref, ks_ref, vs_ref, kw_ref, vw_ref, exp_ref,
                        bias_ref, s_ref, mrun_ref, mb_ref, lrun_ref, acc_ref, psum_ref, oct_ref, ow_ref):
    seq = kv_ref.shape[2]
    n_cmp = seq // CMP_BLOCK
    n_blk = seq // SEL_BLOCK
    n_kt = seq // KT
    n_wt = WIN_KEYS // KT
    rows = GROUP * QB
    qi = pl.program_id(1)
    qs = qi * QB

    @pl.when((pl.program_id(0) == 0) & (qi == 0))
    def _():
        exp_ref[...] = _block_expander(seq, 1)

    @pl.when(qi == 0)
    def _():
        kc_ref[...] = _pool_blocks_t(kv_ref[0, 0:KV_W, :].astype(BF16), wk_ref[...]).T[0:n_cmp].astype(BF16)
        vc_ref[...] = _pool_blocks_t(kv_ref[0, KV_W:2 * KV_W, :].astype(BF16), wv_ref[...]).astype(BF16)
        for t in range(n_kt):
            ks_ref[t] = kv_ref[0, 2 * KV_W:3 * KV_W, t * KT:(t + 1) * KT].astype(BF16)
            vs_ref[t] = kv_ref[0, 3 * KV_W:4 * KV_W, t * KT:(t + 1) * KT].astype(BF16)
            kw_ref[t] = kvw_ref[0, 0:KV_W, t * KT:(t + 1) * KT].astype(BF16)
            vw_ref[t] = kvw_ref[0, KV_W:2 * KV_W, t * KT:(t + 1) * KT].astype(BF16)

    q = q_ref[0]
    q_pos = qs + lax.broadcasted_iota(jnp.int32, (QB, 1), 0)
    q_pos_t = qs + lax.broadcasted_iota(jnp.int32, (1, QB), 1)
    add_bias = lambda s, b: (s.reshape(GROUP, QB, s.shape[-1]) + b[None]).reshape(s.shape)
    n_tiles = (qs + QB + KT - 1) // KT
    qp = [jnp.concatenate(_head_rows(q, kvh), axis=0).astype(BF16) for kvh in range(N_KV_HEADS)]

    win_t0 = jnp.maximum(qs - WINDOW, 0) // KT
    rel = q_pos - (win_t0 * KT + _lane_iota((QB, WIN_KEYS)))
    win_bias = jnp.where((rel >= 0) & (rel <= WINDOW), 0.0, NEG_INF)
    kv_heads = range(N_KV_HEADS)
    gates = [_gate_cols(gate_ref[0], [kvh * GROUP + g for g in range(GROUP)]) for kvh in kv_heads]
    s_w = [add_bias(jnp.concatenate([jnp.dot(qp[kvh], kw_ref[win_t0 + c], preferred_element_type=F32)
                                     for c in range(n_wt)], axis=1), win_bias) for kvh in kv_heads]
    p_w = [jnp.exp2(s_w[kvh] - jnp.max(s_w[kvh], axis=-1, keepdims=True)) for kvh in kv_heads]
    for kvh in kv_heads:
        o_win = sum(_dot_nt(p_w[kvh][:, c * KT:(c + 1) * KT].astype(BF16), vw_ref[win_t0 + c]) for c in range(n_wt))
        ow_ref[kvh] = o_win * (gates[kvh][2] / jnp.sum(p_w[kvh], axis=-1, keepdims=True))

    c_end_t = (lax.broadcasted_iota(jnp.int32, (n_cmp, rows), 0) + 1) * CMP_BLOCK - 1
    ok_t = c_end_t <= jnp.concatenate([q_pos_t] * GROUP, axis=1)
    gate_t = gate_ref[0].T
    per_sel = SEL_BLOCK // CMP_BLOCK
    o_cmp, p_blk = [], []
    for kvh in kv_heads:
        s_t = jnp.where(ok_t, _dot_nt(kc_ref[...], qp[kvh]), NEG_INF)
        p_t = jnp.where(ok_t, jnp.exp2(s_t - jnp.max(s_t, axis=0, keepdims=True)), 0.0)
        l_t = jnp.sum(p_t, axis=0, keepdims=True)
        p_t = p_t * jnp.where(l_t > 0.0, 1.0 / l_t, 0.0)
        g_row = jnp.concatenate([gate_t[3 * (kvh * GROUP + g):3 * (kvh * GROUP + g) + 1] for g in range(GROUP)],
                                axis=1)
        oct_ref[kvh] = jnp.dot(vc_ref[:, 0:n_cmp], (p_t * g_row).astype(BF16), preferred_element_type=F32)
        o_cmp.append(oct_ref[kvh].T)
        psum = sum(p_t[:, g * QB:(g + 1) * QB] for g in range(GROUP))
        tiles = []
        for c in range(QB // LANES):
            psum_ref[kvh, c] = psum[:, c * LANES:(c + 1) * LANES]
            tiles.append(sum(psum_ref[kvh, c, pl.ds(j, n_blk, stride=per_sel), :] for j in range(per_sel)))
        p_blk.append(jnp.concatenate(tiles, axis=1))
    blk_bias = _select_bias_t(p_blk, q_pos_t)
    last = n_tiles - 1
    diag_bias = jnp.where(last * KT + _lane_iota((QB, KT)) <= q_pos, 0.0, NEG_INF)
    for kvh in kv_heads:
        bb = jnp.concatenate([blk_bias[kvh], jnp.zeros((LANES - n_blk, QB), F32)], axis=0).T
        key_bias = jnp.dot(bb.astype(BF16), exp_ref[...], preferred_element_type=F32)
        for t in range(n_kt):
            bias_ref[kvh, t] = key_bias[:, t * KT:(t + 1) * KT]
        bias_ref[kvh, last] += diag_bias

    mrun_ref[...] = jnp.full(mrun_ref.shape, NEG_INF, F32)

    def tile_pairs(tiles_fn):
        def pair(u, carry):
            tiles_fn(2 * u, 2)
            return carry

        lax.fori_loop(0, n_tiles // 2, pair, 0)

        @pl.when(n_tiles % 2 == 1)
        def _():
            tiles_fn(last, 1)

    def max_tiles(t0, n):
        raw = [[jnp.dot(qp[kvh], ks_ref[t0 + i], preferred_element_type=F32) for kvh in kv_heads]
               for i in range(n)]
        for kvh in kv_heads:
            m = None
            for i in range(n):
                s = add_bias(raw[i][kvh], bias_ref[kvh, t0 + i])
                s_ref[kvh, t0 + i] = s
                m_i = jnp.maximum(s[:, :LANES], s[:, LANES:])
                m = m_i if m is None else jnp.maximum(m, m_i)
            mrun_ref[kvh] = jnp.maximum(mrun_ref[kvh], m)

    tile_pairs(max_tiles)
    for kvh in kv_heads:
        mb_ref[kvh] = jnp.broadcast_to(jnp.max(mrun_ref[kvh], axis=-1, keepdims=True), (rows, LANES))
    lrun_ref[...] = jnp.zeros(lrun_ref.shape, F32)
    acc_ref[...] = jnp.zeros(acc_ref.shape, F32)

    def sum_tiles(t0, n):
        for kvh in kv_heads:
            mb = mb_ref[kvh]
            l_add, o_add = None, None
            for i in range(n):
                s = s_ref[kvh, t0 + i]
                p = jnp.exp2(jnp.concatenate([s[:, :LANES] - mb, s[:, LANES:] - mb], axis=1))
                l_i = p[:, :LANES] + p[:, LANES:]
                o_i = _dot_nt(p.astype(BF16), vs_ref[t0 + i])
                l_add = l_i if l_add is None else l_add + l_i
                o_add = o_i if o_add is None else o_add + o_i
            lrun_ref[kvh] += l_add
            acc_ref[kvh] += o_add

    tile_pairs(sum_tiles)

    o_heads = []
    for kvh in kv_heads:
        l = jnp.sum(lrun_ref[kvh], axis=-1, keepdims=True)
        o = o_cmp[kvh] + acc_ref[kvh] * (jnp.where(l > 0.0, 1.0 / l, 0.0) * gates[kvh][1]) + ow_ref[kvh]
        o_heads += [o[g * QB:(g + 1) * QB] for g in range(GROUP)]

    out_ref[0] = _place_heads(o_heads)


def _attn_prompt(q, kv_t, kvw_t, gate, w_pool_k, w_pool_v):
    nb, seq, _ = q.shape
    per_q = lambda w: pl.BlockSpec((1, QB, w), lambda b, i: (b, i, 0))
    per_b = lambda w: pl.BlockSpec((1, w, seq), lambda b, i: (b, 0, 0))
    wspec = pl.BlockSpec((seq, LANES), lambda b, i: (0, 0))
    rows = GROUP * QB
    return pl.pallas_call(
        _attn_prompt_kernel,
        grid=(nb, seq // QB),
        in_specs=[per_q(N_HEADS * HEAD_DIM), per_b(N_PAGED_SLOTS * KV_W), per_b(2 * KV_W), per_q(LANES), wspec, wspec],
        out_specs=per_q(N_HEADS * HEAD_DIM),
        out_shape=jax.ShapeDtypeStruct(q.shape, F32),
        scratch_shapes=[pltpu.VMEM((seq // CMP_BLOCK, KV_W), BF16), pltpu.VMEM((KV_W, LANES), BF16)]
        + [pltpu.VMEM((seq // KT, KV_W, KT), BF16)] * 4
        + [pltpu.VMEM((LANES, seq), BF16), pltpu.VMEM((N_KV_HEADS, seq // KT, QB, KT), F32),
           pltpu.VMEM((N_KV_HEADS, seq // KT, rows, KT), F32)]
        + [pltpu.VMEM((N_KV_HEADS, rows, LANES), F32)] * 4
        + [pltpu.VMEM((N_KV_HEADS, QB // LANES, seq // CMP_BLOCK, LANES), F32),
           pltpu.VMEM((N_KV_HEADS, LANES, rows), F32),
           pltpu.VMEM((N_KV_HEADS, rows, LANES), F32)],
        compiler_params=_cparams("arbitrary", "arbitrary"),
        name="attn_prompt",
    )(q, kv_t, kvw_t, gate, w_pool_k, w_pool_v)


def _attn_sample_kernel(pt_ref, q_ref, kvn_ref, kvwn_ref, gate_ref, wk_ref, wv_ref, kvwt_ref, cache_ref, win_ref,
                        out_ref, win_out_ref, past_ref, exp_ref, sem):
    step = pl.program_id(0)
    n_steps = pl.num_programs(0)
    n_pages = pt_ref.shape[1]
    page = cache_ref.shape[2]
    past_len = n_pages * page
    n_blk = past_len // SEL_BLOCK + 1
    w_buf = win_ref.shape[2]
    slot = step % 2

    def page_copy(st, sl, g, p):
        return pltpu.make_async_copy(cache_ref.at[pt_ref[st * DEC_GROUP + g, p]],
                                     past_ref.at[sl, g, :, pl.ds(p * page, page)], sem.at[sl, g, p])

    def start_pages(st, sl):
        for g in range(DEC_GROUP):
            for p in range(n_pages):
                page_copy(st, sl, g, p).start()

    @pl.when(step == 0)
    def _():
        start_pages(0, 0)
        exp_ref[...] = _block_expander(past_len, SEL_BLOCK // CMP_BLOCK)

    @pl.when(step + 1 < n_steps)
    def _():
        start_pages(step + 1, 1 - slot)

    q_pos = jnp.full((N_HEADS, 1), past_len, jnp.int32)

    def new_key_softmax(qp_f, s_past, ok_past, vt_past, k_new, v_new, ok_new):
        s_new = jnp.sum(qp_f * k_new, axis=-1, keepdims=True)
        s_past = jnp.where(ok_past, s_past, NEG_INF)
        s_new = jnp.where(ok_new, s_new, NEG_INF)
        m = jnp.maximum(jnp.max(s_past, axis=-1, keepdims=True), s_new)
        p_past = jnp.where(ok_past, jnp.exp2(s_past - m), 0.0)
        p_new = jnp.where(ok_new, jnp.exp2(s_new - m), 0.0)
        l = jnp.sum(p_past, axis=-1, keepdims=True) + p_new
        o = _dot_nt(p_past.astype(BF16), vt_past) + p_new * v_new
        return o * jnp.where(l > 0.0, 1.0 / l, 0.0)

    qp_f, o_win = [], []
    for g in range(DEC_GROUP):
        b = step * DEC_GROUP + g
        qf = jnp.concatenate([r for kvh in range(N_KV_HEADS) for r in _head_rows(q_ref[g], kvh)], axis=0)
        qp_f.append(qf)
        kvw_new = kvwn_ref[g]
        win = win_ref[g]
        rel = past_len - (past_len - w_buf + _lane_iota((N_HEADS, w_buf)))
        o_win.append(new_key_softmax(qf, jnp.dot(qf.astype(BF16), win[0:KV_W].astype(BF16),
                                                 preferred_element_type=F32),
                                     (rel >= 0) & (rel <= WINDOW), win[KV_W:2 * KV_W].astype(BF16),
                                     kvw_new[:, 0:KV_W], kvw_new[:, KV_W:2 * KV_W], jnp.full((N_HEADS, 1), True)))
        new_col = jnp.sum(jnp.where(_lane_iota(kvwt_ref.shape) == b, kvwt_ref[...], 0.0), axis=1, keepdims=True)
        win_out_ref[g] = jnp.where(_lane_iota(win.shape) == w_buf - 1, new_col, pltpu.roll(win, w_buf - 1, 1))

    for g in range(DEC_GROUP):
        for p in range(n_pages):
            page_copy(step, slot, g, p).wait()

    def slab(g, i):
        return past_ref[slot, g, i * KV_W:(i + 1) * KV_W, :].astype(BF16)

    elems = range(DEC_GROUP)
    qp = [qp_f[g].astype(BF16) for g in elems]
    kc_t = [_pool_blocks_t(slab(g, 0), wk_ref[...]).astype(BF16) for g in elems]
    vc_t = [_pool_blocks_t(slab(g, 1), wv_ref[...]).astype(BF16) for g in elems]
    s_sel = [jnp.dot(qp[g], slab(g, 2), preferred_element_type=F32) for g in elems]
    c_ok = (_lane_iota((N_HEADS, LANES)) + 1) * CMP_BLOCK - 1 <= q_pos
    o_cmp, sel = [], []
    for g in elems:
        s_c = jnp.where(c_ok, jnp.dot(qp[g], kc_t[g], preferred_element_type=F32), NEG_INF)
        p_c = jnp.where(c_ok, jnp.exp2(s_c - jnp.max(s_c, axis=-1, keepdims=True)), 0.0)
        l_c = jnp.sum(p_c, axis=-1, keepdims=True)
        inv_c = jnp.where(l_c > 0.0, 1.0 / l_c, 0.0)
        o_cmp.append(_dot_nt(p_c.astype(BF16), vc_t[g]) * inv_c)
        p_cmp = p_c * inv_c
        psum = jnp.concatenate(
            [jnp.broadcast_to(jnp.sum(p_cmp[kvh * GROUP:(kvh + 1) * GROUP], axis=0, keepdims=True), (GROUP, LANES))
             for kvh in range(N_KV_HEADS)], axis=0)
        sel.append(_select_blocks(psum, q_pos, n_blk))

    new_blk = past_len // SEL_BLOCK
    for g in elems:
        kv_new = kvn_ref[g]
        selexp = jnp.dot(sel[g].astype(BF16), exp_ref[...], preferred_element_type=F32)
        ok_new = sel[g][:, 2 * new_blk:2 * new_blk + 1] > 0.5
        ok_past = (selexp > 0.5) & (_lane_iota((N_HEADS, past_len)) <= q_pos)
        o_sel = new_key_softmax(qp_f[g], s_sel[g], ok_past, slab(g, 3),
                                kv_new[:, 2 * KV_W:3 * KV_W], kv_new[:, 3 * KV_W:4 * KV_W], ok_new)
        g_cmp, g_sel, g_win = _gate_cols(gate_ref[g], range(N_HEADS))
        o = o_cmp[g] * g_cmp + o_sel * g_sel + o_win[g] * g_win
        out_ref[g] = _place_heads([o[h:h + 1] for h in range(N_HEADS)])


def _attn_sample(page_table, q, kv_new, kvw_new, gate, w_pool_k, w_pool_v, cache_t, win_t):
    nb, n_pages = page_table.shape
    n_feat, page = cache_t.shape[1], cache_t.shape[2]
    w_buf = win_t.shape[2]
    past_len = n_pages * page
    assert nb % DEC_GROUP == 0
    row3 = lambda w: pl.BlockSpec((DEC_GROUP, 1, w), lambda b, pt: (b, 0, 0))
    full = lambda shape: pl.BlockSpec(shape, lambda b, pt: (0,) * len(shape))
    win_spec = pl.BlockSpec((DEC_GROUP, 2 * KV_W, w_buf), lambda b, pt: (b, 0, 0))
    grid_spec = pltpu.PrefetchScalarGridSpec(
        num_scalar_prefetch=1,
        grid=(nb // DEC_GROUP,),
        in_specs=[row3(N_HEADS * HEAD_DIM), row3(N_PAGED_SLOTS * KV_W), row3(2 * KV_W), row3(LANES),
                  full((past_len, LANES)), full((past_len, LANES)), full((2 * KV_W, nb)),
                  pl.BlockSpec(memory_space=pl.ANY), win_spec],
        out_specs=[row3(N_HEADS * HEAD_DIM), win_spec],
        scratch_shapes=[pltpu.VMEM((2, DEC_GROUP, n_feat, past_len), F32),
                        pltpu.VMEM((LANES, past_len), BF16),
                        pltpu.SemaphoreType.DMA((2, DEC_GROUP, n_pages))],
    )
    r3 = lambda a: a.reshape(nb, 1, a.shape[-1])
    return pl.pallas_call(
        _attn_sample_kernel,
        grid_spec=grid_spec,
        out_shape=[jax.ShapeDtypeStruct((nb, 1, N_HEADS * HEAD_DIM), F32),
                   jax.ShapeDtypeStruct(win_t.shape, F32)],
        compiler_params=_cparams("arbitrary"),
        name="attn_sample",
    )(page_table, r3(q), r3(kv_new), r3(kvw_new), r3(gate), w_pool_k, w_pool_v, kvw_new.T, cache_t, win_t)


def _rms(x, g):
    return x * lax.rsqrt(jnp.mean(x * x, axis=-1, keepdims=True) + RMS_EPS) * g


def _mlp_kernel(x_ref, conv_ref, attn_ref, wout_ref, gm_ref, wup_ref, wdn_ref, gf_ref, y_ref, h_ref):
    h_ref[...] = (x_ref[...]
                  + jnp.dot(conv_ref[...].astype(BF16), wout_ref[0:C_CONV, :], preferred_element_type=F32)
                  + jnp.dot(attn_ref[...].astype(BF16), wout_ref[C_CONV:, :], preferred_element_type=F32))
    hn = _rms(h_ref[...], gm_ref[...]).astype(BF16)
    acc = None
    for c in range(D_FF // FF_CHUNK):
        m = jnp.dot(hn, wup_ref[:, c * FF_CHUNK:(c + 1) * FF_CHUNK], preferred_element_type=F32)
        m = jnp.maximum(m, 0.0)
        t = jnp.dot((m * m).astype(BF16), wdn_ref[c * FF_CHUNK:(c + 1) * FF_CHUNK, :], preferred_element_type=F32)
        acc = t if acc is None else acc + t
    y_ref[...] = _rms(h_ref[...] + acc, gf_ref[...])


def _mlp(x, conv_y, attn_y, w_out, g_mlp, w_up, w_down, g_final, tm):
    n = x.shape[0]
    row = lambda w: pl.BlockSpec((tm, w), lambda i: (i, 0))
    once = lambda shape: pl.BlockSpec(shape, lambda i: (0,) * len(shape), pipeline_mode=pl.Buffered(1))
    return pl.pallas_call(
        _mlp_kernel,
        grid=(n // tm,),
        in_specs=[row(D_MODEL), row(C_CONV), row(N_HEADS * HEAD_DIM), once((D_MODEL, D_MODEL)), once((1, D_MODEL)),
                  once((D_MODEL, D_FF)), once((D_FF, D_MODEL)), once((1, D_MODEL))],
        out_specs=row(D_MODEL),
        out_shape=jax.ShapeDtypeStruct((n, D_MODEL), F32),
        scratch_shapes=[pltpu.VMEM((tm, D_MODEL), F32)],
        compiler_params=_cparams("parallel"),
        name="mlp",
    )(x, conv_y, attn_y, w_out, g_mlp, w_up, w_down, g_final)


def kernel(x_prompt, x_sample, cache_kv, cache_win, state_conv, page_table, g_attn_norm, w_in, w_dw, b_dw,
           conv_ln_g, conv_ln_b, w_cmp_k, w_cmp_v, w_out, g_mlp_norm, w_up, w_down, g_final):
    depth = w_in.shape[0]
    assert depth == 1, "single-layer trunk"
    nb, seq, _ = x_prompt.shape
    db, dseq, _ = x_sample.shape
    assert dseq == 1
    n_phys, page = cache_kv.shape[1], cache_kv.shape[2]
    w_buf = cache_win.shape[2]

    w_main = jnp.concatenate([w_in[0][:, :COL_KV], w_in[0][:, COL_GATE:],
                              jnp.zeros((D_MODEL, IN_COLS_PAD - IN_COLS), F32)], axis=1).astype(BF16)
    w_kv = jnp.transpose(w_in[0][:, COL_KV:COL_GATE]).astype(BF16)
    w_out_b, w_up_b, w_down_b = w_out[0].astype(BF16), w_up[0].astype(BF16), w_down[0].astype(BF16)
    row = lambda a: a.reshape(1, -1)
    g_attn, g_mlp, g_fin = row(g_attn_norm[0]), row(g_mlp_norm[0]), row(g_final)
    conv_w = (w_dw[0], row(b_dw[0]), row(conv_ln_g[0]), row(conv_ln_b[0]))
    mlp_w = (w_out_b, g_mlp, w_up_b, w_down_b, g_fin)

    xp = x_prompt.reshape(nb * seq, D_MODEL)
    glu, q, gate, kv_t, kvw_t = _project(xp, g_attn, w_main, w_kv, PROJ_ROWS, seq)
    per_b = lambda a: a.reshape(nb, seq, a.shape[-1])
    glu3 = per_b(glu)
    conv_y = _conv_prompt(glu3, *conv_w)
    attn_y = _attn_prompt(per_b(q), kv_t, kvw_t, per_b(gate), _pool_matrix(w_cmp_k[0], seq),
                          _pool_matrix(w_cmp_v[0], seq))
    y_prompt = _mlp(xp, conv_y.reshape(nb * seq, C_CONV), attn_y.reshape(nb * seq, -1), *mlp_w, MLP_ROWS)
    pos_major = lambda a, n_slots: jnp.transpose(
        a.reshape(a.shape[0], n_slots, N_KV_HEADS, HEAD_DIM, a.shape[-1]), (0, 4, 1, 2, 3))[None]
    new_kv_p = pos_major(kv_t, N_PAGED_SLOTS)
    new_win_p = pos_major(kvw_t[:, :, seq - min(WINDOW, seq):], 2)
    new_conv_p = glu3[:, seq - (CONV_WIDTH - 1):][None]

    xs = x_sample.reshape(db, D_MODEL)
    glu_s, q_s, gate_s, kv_s, kvw_s = _project(xs, g_attn, w_main, w_kv, db)
    conv_y_s, conv_st = _conv_sample(jnp.transpose(state_conv[0], (1, 0, 2)), glu_s, *conv_w)
    cache_t = jnp.transpose(cache_kv[0], (0, 2, 3, 4, 1)).reshape(n_phys, N_PAGED_SLOTS * KV_W, page)
    win_t = jnp.transpose(cache_win[0], (0, 2, 3, 4, 1)).reshape(db, 2 * KV_W, w_buf)
    n_past = page_table.shape[1] * page
    attn_s, new_win_t = _attn_sample(page_table, q_s, kv_s, kvw_s, gate_s, _pool_matrix(w_cmp_k[0], n_past),
                                     _pool_matrix(w_cmp_v[0], n_past), cache_t, win_t)
    new_win_s = pos_major(new_win_t, 2)
    y_sample = _mlp(xs, conv_y_s, attn_s.reshape(db, -1), *mlp_w, db)

    return (y_prompt.reshape(nb, seq, D_MODEL), y_sample.reshape(db, 1, D_MODEL), new_kv_p, new_win_p, new_conv_p,
            kv_s.reshape(1, db, 1, N_PAGED_SLOTS, N_KV_HEADS, HEAD_DIM), new_win_s,
            jnp.transpose(conv_st, (1, 0, 2))[None])
```

```python
import functools

import jax
import jax.numpy as jnp
from jax import lax
from jax.experimental import pallas as pl
from jax.experimental.pallas import tpu as pltpu

F32 = jnp.float32
BF16 = jnp.bfloat16

D_MODEL = 1024
C_CONV = 512
CONV_WIDTH = 31
N_HEADS = 8
HEAD_DIM = 64
N_KV_HEADS = 2
GROUP = N_HEADS // N_KV_HEADS
KV_W = N_KV_HEADS * HEAD_DIM
N_PAGED_SLOTS = 4
D_FF = 4 * D_MODEL
CMP_BLOCK = 32
SEL_BLOCK = 64
TOP_N = 16
N_LOCAL = 2
WINDOW = 512
FORCE_BONUS = float(GROUP + 1)
NEG_INF = -1e30
RMS_EPS = 1e-6
LN_EPS = 1e-5
ATTN_SCALE = HEAD_DIM ** -0.5
LOG2E = 1.4426950408889634

COL_Q = 2 * C_CONV
COL_KV = COL_Q + N_HEADS * HEAD_DIM
COL_WIN = COL_KV + N_PAGED_SLOTS * KV_W
COL_GATE = COL_WIN + 2 * KV_W
IN_COLS = COL_GATE + 3 * N_HEADS
LANES = 128
SUBLANES = 8
IN_COLS_PAD = -(-IN_COLS // LANES) * LANES

VMEM_LIMIT = 56 * 1024 * 1024

PROJ_ROWS = 1024
MLP_ROWS = 512
CONV_ROWS = 128
CONV_PAD = 32
NORM_ROWS = 256
QB = 256
KT = 256
WIN_KEYS = WINDOW + QB
assert QB % KT == 0 and WIN_KEYS % KT == 0
DEC_GROUP = 4
FF_CHUNK = 1024


def _cparams(*sem):
    return pltpu.CompilerParams(dimension_semantics=sem, vmem_limit_bytes=VMEM_LIMIT)


def _const_spec(shape):
    return pl.BlockSpec(shape, lambda *_: (0,) * len(shape))


def _dot_nt(a, b):
    return lax.dot_general(a, b, (((1,), (1,)), ((), ())), preferred_element_type=F32)


def _lane_iota(shape):
    return lax.broadcasted_iota(jnp.int32, shape, len(shape) - 1)


def _proj_kernel(x_ref, g_ref, w_ref, wkv_ref, glu_ref, q_ref, gate_ref, kv_ref, kvw_ref, *, kv_t):
    x = x_ref[...]
    ms = jnp.mean(x * x, axis=-1, keepdims=True)
    xn = (x * lax.rsqrt(ms + RMS_EPS) * g_ref[...]).astype(BF16)
    z = jnp.dot(xn, w_ref[...], preferred_element_type=F32)
    glu_ref[...] = z[:, :C_CONV] * jax.nn.sigmoid(z[:, C_CONV:COL_Q])
    q_ref[...] = z[:, COL_Q:COL_KV]
    gate_ref[...] = jax.nn.sigmoid(z[:, COL_KV:])
    n_paged = N_PAGED_SLOTS * KV_W
    if kv_t:
        kv = _dot_nt(wkv_ref[...], xn)
        kv_ref[0] = kv[:n_paged]
        kvw_ref[0] = kv[n_paged:]
    else:
        kv = _dot_nt(xn, wkv_ref[...])
        kv_ref[...] = kv[:, :n_paged]
        kvw_ref[...] = kv[:, n_paged:]


def _project(x, g, w_main, w_kv, tm, seq=None):
    n = x.shape[0]
    row = lambda w: pl.BlockSpec((tm, w), lambda i: (i, 0))
    n_paged = N_PAGED_SLOTS * KV_W
    widths = (C_CONV, N_HEADS * HEAD_DIM, LANES)
    out_specs = [row(w) for w in widths]
    out_shape = [jax.ShapeDtypeStruct((n, w), F32) for w in widths]
    for w in (n_paged, 2 * KV_W):
        if seq is None:
            out_specs.append(row(w))
            out_shape.append(jax.ShapeDtypeStruct((n, w), F32))
        else:
            per_seq = seq // tm
            out_specs.append(pl.BlockSpec((1, w, tm), lambda i: (i // per_seq, 0, i % per_seq)))
            out_shape.append(jax.ShapeDtypeStruct((n // seq, w, seq), F32))
    return pl.pallas_call(
        functools.partial(_proj_kernel, kv_t=seq is not None),
        grid=(n // tm,),
        in_specs=[row(D_MODEL), _const_spec((1, D_MODEL)), _const_spec(w_main.shape), _const_spec(w_kv.shape)],
        out_specs=out_specs,
        out_shape=out_shape,
        compiler_params=_cparams("parallel"),
        name="projection",
    )(x, g, w_main, w_kv)


def _ln_swish(y, lng, lnb):
    mu = jnp.mean(y, axis=-1, keepdims=True)
    d = y - mu
    var = jnp.mean(d * d, axis=-1, keepdims=True)
    y = d * lax.rsqrt(var + LN_EPS) * lng + lnb
    return y * jax.nn.sigmoid(y)


def _conv_prompt_kernel(glu_ref, w_ref, b_ref, lng_ref, lnb_ref, y_ref, ext_ref, stage_ref):
    seq = glu_ref.shape[1]
    n_ct = C_CONV // LANES
    for ct in range(n_ct):
        ext_ref[ct, 0:CONV_PAD, :] = jnp.zeros((CONV_PAD, LANES), F32)
        ext_ref[ct, CONV_PAD:, :] = glu_ref[0, :, ct * LANES:(ct + 1) * LANES]
    shift = CONV_PAD - (CONV_WIDTH - 1)

    def body(c, carry):
        base = pl.multiple_of(c * CONV_ROWS, CONV_ROWS)

        def tile(ct, carry_ct):
            win = ext_ref[ct, pl.ds(base, CONV_ROWS + CONV_PAD), :]
            acc = jnp.zeros((CONV_ROWS, LANES), F32)
            for r in range(SUBLANES):
                shifted = win if r == 0 else pltpu.roll(win, CONV_ROWS + CONV_PAD - r, 0)
                for j in range(CONV_WIDTH):
                    if (j + shift) % SUBLANES == r:
                        a = j + shift - r
                        acc = acc + w_ref[ct, j:j + 1, :] * shifted[a:a + CONV_ROWS]
            stage_ref[ct, pl.ds(base, CONV_ROWS), :] = acc
            return carry_ct

        lax.fori_loop(0, n_ct, tile, 0)
        return carry

    lax.fori_loop(0, seq // CONV_ROWS, body, 0)

    def norm(c, carry):
        base = pl.multiple_of(c * NORM_ROWS, NORM_ROWS)
        y = jnp.concatenate([stage_ref[ct, pl.ds(base, NORM_ROWS), :] for ct in range(n_ct)], axis=1) + b_ref[...]
        y_ref[0, pl.ds(base, NORM_ROWS), :] = _ln_swish(y, lng_ref[...], lnb_ref[...])
        return carry

    lax.fori_loop(0, seq // NORM_ROWS, norm, 0)


def _conv_prompt(glu, w_dw, b_dw, lng, lnb):
    nb, seq, _ = glu.shape
    n_ct = C_CONV // LANES
    w_tiles = jnp.transpose(w_dw.reshape(CONV_WIDTH, n_ct, LANES), (1, 0, 2))
    blk = pl.BlockSpec((1, seq, C_CONV), lambda b: (b, 0, 0))
    return pl.pallas_call(
        _conv_prompt_kernel,
        grid=(nb,),
        in_specs=[blk, _const_spec((n_ct, CONV_WIDTH, LANES))] + [_const_spec((1, C_CONV))] * 3,
        out_specs=blk,
        out_shape=jax.ShapeDtypeStruct(glu.shape, F32),
        scratch_shapes=[pltpu.VMEM((n_ct, seq + CONV_PAD, LANES), F32), pltpu.VMEM((n_ct, seq, LANES), F32)],
        compiler_params=_cparams("parallel"),
        name="conv_prompt",
    )(glu, w_tiles, b_dw, lng, lnb)


def _conv_sample_kernel(st_ref, glu_ref, w_ref, b_ref, lng_ref, lnb_ref, y_ref, st_out_ref):
    n_prev = CONV_WIDTH - 1
    glu = glu_ref[...]
    acc = w_ref[n_prev:n_prev + 1, :] * glu
    for j in range(n_prev):
        acc = acc + w_ref[j:j + 1, :] * st_ref[j]
    y_ref[...] = _ln_swish(acc + b_ref[...], lng_ref[...], lnb_ref[...])
    st_out_ref[0:n_prev - 1] = st_ref[1:n_prev]
    st_out_ref[n_prev - 1] = glu


def _conv_sample(state_t, glu, w_dw, b_dw, lng, lnb):
    nb = glu.shape[0]
    return pl.pallas_call(
        _conv_sample_kernel,
        grid=(1,),
        in_specs=[_const_spec(state_t.shape), _const_spec((nb, C_CONV)), _const_spec((CONV_WIDTH, C_CONV))]
        + [_const_spec((1, C_CONV))] * 3,
        out_specs=[_const_spec((nb, C_CONV)), _const_spec(state_t.shape)],
        out_shape=[jax.ShapeDtypeStruct((nb, C_CONV), F32), jax.ShapeDtypeStruct(state_t.shape, F32)],
        compiler_params=_cparams("arbitrary"),
        name="conv_sample",
    )(state_t, glu, w_dw, b_dw, lng, lnb)


def _head_rows(q, kvh):
    rows = q.shape[0]
    half = _lane_iota((rows, KV_W)) // HEAD_DIM
    out = []
    for g in range(GROUP):
        h = kvh * GROUP + g
        slab = q[:, KV_W * (h // 2):KV_W * (h // 2 + 1)]
        if h % 2 != kvh:
            slab = pltpu.roll(slab, HEAD_DIM, 1)
        out.append(jnp.where(half == kvh, slab * (ATTN_SCALE * LOG2E), 0.0))
    return out


def _pool_matrix(w_cmp, n_keys):
    eye = jnp.eye(n_keys // CMP_BLOCK, LANES // N_KV_HEADS, dtype=F32)
    per_head = [(eye[:, None, :] * w_cmp[None, :, h, None]).reshape(n_keys, -1) for h in range(N_KV_HEADS)]
    return jnp.concatenate(per_head, axis=1).astype(BF16)


def _pool_blocks_t(x_t, w_pool):
    res = jnp.dot(x_t, w_pool, preferred_element_type=F32)
    row = lax.broadcasted_iota(jnp.int32, res.shape, 0)
    own = jnp.where(row < HEAD_DIM, res, pltpu.roll(res, HEAD_DIM, 1))
    return jnp.where(_lane_iota(res.shape) < w_pool.shape[0] // CMP_BLOCK, own, 0.0)


def _select_blocks(psum, q_pos, n_blk):
    shape = psum.shape
    lane = _lane_iota(shape)
    even = (lane & 1) == 0
    pair = psum + jnp.where(even, pltpu.roll(psum, LANES - 1, 1), pltpu.roll(psum, 1, 1))
    blk = lane // (SEL_BLOCK // CMP_BLOCK)
    back = q_pos // SEL_BLOCK - blk
    forced = (blk == 0) | ((back >= 0) & (back < N_LOCAL))
    score = jnp.where(back >= 0, pair + jnp.where(forced, FORCE_BONUS, 0.0), -jnp.inf)
    rank = jnp.zeros(shape, jnp.int32)
    for i in range(n_blk):
        si = score[:, 2 * i:2 * i + 1]
        beats = (si > score) | ((si == score) & (i < blk))
        rank = rank + beats.astype(jnp.int32)
    return jnp.where((rank < min(TOP_N, n_blk)) & even, 1.0, 0.0)


def _select_bias_t(p_blks, q_pos):
    n_blk = p_blks[0].shape[0]
    blk = lax.broadcasted_iota(jnp.int32, p_blks[0].shape, 0)
    back = q_pos // SEL_BLOCK - blk
    forced = (blk == 0) | ((back >= 0) & (back < N_LOCAL))
    bonus = jnp.where(forced, FORCE_BONUS, 0.0)
    scores = [jnp.where(back >= 0, p + bonus, -jnp.inf) for p in p_blks]
    ranks = [jnp.zeros(blk.shape, jnp.int32) for _ in p_blks]
    for i in range(n_blk):
        for k, score in enumerate(scores):
            si = score[i:i + 1, :]
            beats = (si > score) | ((si == score) & (i < blk))
            ranks[k] = ranks[k] + beats.astype(jnp.int32)
    return [jnp.where((rank < min(TOP_N, n_blk)) & (back >= 0), 0.0, NEG_INF) for rank in ranks]


def _block_expander(n_keys, row_stride):
    li = lax.broadcasted_iota(jnp.int32, (LANES, n_keys), 0)
    ti = lax.broadcasted_iota(jnp.int32, (LANES, n_keys), 1)
    return jnp.where(li == row_stride * (ti // SEL_BLOCK), 1.0, 0.0).astype(BF16)


def _gate_cols(gates, heads):
    return [jnp.concatenate([gates[:, 3 * h + k:3 * h + k + 1] for h in heads], axis=0) for k in range(3)]


def _place_heads(o_heads):
    rows = o_heads[0].shape[0]
    low = _lane_iota((rows, KV_W)) < HEAD_DIM
    slabs = []
    for s in range(N_HEADS // 2):
        pair = []
        for h in (2 * s, 2 * s + 1):
            o = o_heads[h]
            if h // GROUP != h % 2:
                o = pltpu.roll(o, HEAD_DIM, 1)
            pair.append(o)
        slabs.append(jnp.where(low, pair[0], pair[1]))
    return jnp.concatenate(slabs, axis=1)


def _attn_prompt_kernel(q_ref, kv_ref, kvw_ref, gate_ref, wk_ref, wv_ref, out_ref,
                        kc_ref, vc_ref, ks_ref, vs_ref, kw_ref, vw_ref, exp_ref,
                        bias_ref, s_ref, mrun_ref, mb_ref, lrun_ref, acc_ref, psum_ref, oct_ref, ow_ref):
    seq = kv_ref.shape[2]
    n_cmp = seq // CMP_BLOCK
    n_blk = seq // SEL_BLOCK
    n_kt = seq // KT
    n_wt = WIN_KEYS // KT
    rows = GROUP * QB
    qi = pl.program_id(1)
    qs = qi * QB

    @pl.when((pl.program_id(0) == 0) & (qi == 0))
    def _():
        exp_ref[...] = _block_expander(seq, 1)

    @pl.when(qi == 0)
    def _():
        kc_ref[...] = _pool_blocks_t(kv_ref[0, 0:KV_W, :].astype(BF16), wk_ref[...]).T[0:n_cmp].astype(BF16)
        vc_ref[...] = _pool_blocks_t(kv_ref[0, KV_W:2 * KV_W, :].astype(BF16), wv_ref[...]).astype(BF16)
        for t in range(n_kt):
            ks_ref[t] = kv_ref[0, 2 * KV_W:3 * KV_W, t * KT:(t + 1) * KT].astype(BF16)
            vs_ref[t] = kv_ref[0, 3 * KV_W:4 * KV_W, t * KT:(t + 1) * KT].astype(BF16)
            kw_ref[t] = kvw_ref[0, 0:KV_W, t * KT:(t + 1) * KT].astype(BF16)
            vw_ref[t] = kvw_ref[0, KV_W:2 * KV_W, t * KT:(t + 1) * KT].astype(BF16)

    q = q_ref[0]
    q_pos = qs + lax.broadcasted_iota(jnp.int32, (QB, 1), 0)
    q_pos_t = qs + lax.broadcasted_iota(jnp.int32, (1, QB), 1)
    add_bias = lambda s, b: (s.reshape(GROUP, QB, s.shape[-1]) + b[None]).reshape(s.shape)
    n_tiles = (qs + QB + KT - 1) // KT
    qp = [jnp.concatenate(_head_rows(q, kvh), axis=0).astype(BF16) for kvh in range(N_KV_HEADS)]

    win_t0 = jnp.maximum(qs - WINDOW, 0) // KT
    rel = q_pos - (win_t0 * KT + _lane_iota((QB, WIN_KEYS)))
    win_bias = jnp.where((rel >= 0) & (rel <= WINDOW), 0.0, NEG_INF)
    kv_heads = range(N_KV_HEADS)
    gates = [_gate_cols(gate_ref[0], [kvh * GROUP + g for g in range(GROUP)]) for kvh in kv_heads]
    s_w = [add_bias(jnp.concatenate([jnp.dot(qp[kvh], kw_ref[win_t0 + c], preferred_element_type=F32)
                                     for c in range(n_wt)], axis=1), win_bias) for kvh in kv_heads]
    p_w = [jnp.exp2(s_w[kvh] - jnp.max(s_w[kvh], axis=-1, keepdims=True)) for kvh in kv_heads]
    for kvh in kv_heads:
        o_win = sum(_dot_nt(p_w[kvh][:, c * KT:(c + 1) * KT].astype(BF16), vw_ref[win_t0 + c]) for c in range(n_wt))
        ow_ref[kvh] = o_win * (gates[kvh][2] / jnp.sum(p_w[kvh], axis=-1, keepdims=True))

    c_end_t = (lax.broadcasted_iota(jnp.int32, (n_cmp, rows), 0) + 1) * CMP_BLOCK - 1
    ok_t = c_end_t <= jnp.concatenate([q_pos_t] * GROUP, axis=1)
    gate_t = gate_ref[0].T
    per_sel = SEL_BLOCK // CMP_BLOCK
    o_cmp, p_blk, p_ts = [], [], []
    for kvh in kv_heads:
        s_t = jnp.where(ok_t, _dot_nt(kc_ref[...], qp[kvh]), NEG_INF)
        p_t = jnp.where(ok_t, jnp.exp2(s_t - jnp.max(s_t, axis=0, keepdims=True)), 0.0)
        l_t = jnp.sum(p_t, axis=0, keepdims=True)
        p_ts.append(p_t * jnp.where(l_t > 0.0, 1.0 / l_t, 0.0))
    for kvh in kv_heads:
        g_row = jnp.concatenate([gate_t[3 * (kvh * GROUP + g):3 * (kvh * GROUP + g) + 1] for g in range(GROUP)],
                                axis=1)
        oct_ref[kvh] = jnp.dot(vc_ref[:, 0:n_cmp], (p_ts[kvh] * g_row).astype(BF16), preferred_element_type=F32)
        o_cmp.append(oct_ref[kvh].T)
    for kvh in kv_heads:
        psum = sum(p_ts[kvh][:, g * QB:(g + 1) * QB] for g in range(GROUP))
        tiles = []
        for c in range(QB // LANES):
            psum_ref[kvh, c] = psum[:, c * LANES:(c + 1) * LANES]
            tiles.append(sum(psum_ref[kvh, c, pl.ds(j, n_blk, stride=per_sel), :] for j in range(per_sel)))
        p_blk.append(jnp.concatenate(tiles, axis=1))
    blk_bias = _select_bias_t(p_blk, q_pos_t)
    last = n_tiles - 1
    diag_bias = jnp.where(last * KT + _lane_iota((QB, KT)) <= q_pos, 0.0, NEG_INF)
    for kvh in kv_heads:
        bb = jnp.concatenate([blk_bias[kvh], jnp.zeros((LANES - n_blk, QB), F32)], axis=0).T
        key_bias = jnp.dot(bb.astype(BF16), exp_ref[...], preferred_element_type=F32)
        for t in range(n_kt):
            bias_ref[kvh, t] = key_bias[:, t * KT:(t + 1) * KT]
        bias_ref[kvh, last] += diag_bias

    mrun_ref[...] = jnp.full(mrun_ref.shape, NEG_INF, F32)

    def tile_pairs(tiles_fn):
        def pair(u, carry):
            tiles_fn(2 * u, 2)
            return carry

        lax.fori_loop(0, n_tiles // 2, pair, 0)

        @pl.when(n_tiles % 2 == 1)
        def _():
            tiles_fn(last, 1)

    def max_tiles(t0, n):
        raw = [[jnp.dot(qp[kvh], ks_ref[t0 + i], preferred_element_type=F32) for kvh in kv_heads]
               for i in range(n)]
        for kvh in kv_heads:
            m = None
            for i in range(n):
                s = add_bias(raw[i][kvh], bias_ref[kvh, t0 + i])
                s_ref[kvh, t0 + i] = s
                m_i = jnp.maximum(s[:, :LANES], s[:, LANES:])
                m = m_i if m is None else jnp.maximum(m, m_i)
            mrun_ref[kvh] = jnp.maximum(mrun_ref[kvh], m)

    tile_pairs(max_tiles)
    for kvh in kv_heads:
        mb_ref[kvh] = jnp.broadcast_to(jnp.max(mrun_ref[kvh], axis=-1, keepdims=True), (rows, LANES))
    lrun_ref[...] = jnp.zeros(lrun_ref.shape, F32)
    acc_ref[...] = jnp.zeros(acc_ref.shape, F32)

    def sum_tiles(t0, n):
        for kvh in kv_heads:
            mb = mb_ref[kvh]
            l_add, o_add = None, None
            for i in range(n):
                s = s_ref[kvh, t0 + i]
                p = jnp.exp2(jnp.concatenate([s[:, :LANES] - mb, s[:, LANES:] - mb], axis=1))
                l_i = p[:, :LANES] + p[:, LANES:]
                o_i = _dot_nt(p.astype(BF16), vs_ref[t0 + i])
                l_add = l_i if l_add is None else l_add + l_i
                o_add = o_i if o_add is None else o_add + o_i
            lrun_ref[kvh] += l_add
            acc_ref[kvh] += o_add

    tile_pairs(sum_tiles)

    o_heads = []
    for kvh in kv_heads:
        l = jnp.sum(lrun_ref[kvh], axis=-1, keepdims=True)
        o = o_cmp[kvh] + acc_ref[kvh] * (jnp.where(l > 0.0, 1.0 / l, 0.0) * gates[kvh][1]) + ow_ref[kvh]
        o_heads += [o[g * QB:(g + 1) * QB] for g in range(GROUP)]

    out_ref[0] = _place_heads(o_heads)


def _attn_prompt(q, kv_t, kvw_t, gate, w_pool_k, w_pool_v):
    nb, seq, _ = q.shape
    per_q = lambda w: pl.BlockSpec((1, QB, w), lambda b, i: (b, i, 0))
    per_b = lambda w: pl.BlockSpec((1, w, seq), lambda b, i: (b, 0, 0))
    wspec = pl.BlockSpec((seq, LANES), lambda b, i: (0, 0))
    rows = GROUP * QB
    return pl.pallas_call(
        _attn_prompt_kernel,
        grid=(nb, seq // QB),
        in_specs=[per_q(N_HEADS * HEAD_DIM), per_b(N_PAGED_SLOTS * KV_W), per_b(2 * KV_W), per_q(LANES), wspec, wspec],
        out_specs=per_q(N_HEADS * HEAD_DIM),
        out_shape=jax.ShapeDtypeStruct(q.shape, F32),
        scratch_shapes=[pltpu.VMEM((seq // CMP_BLOCK, KV_W), BF16), pltpu.VMEM((KV_W, LANES), BF16)]
        + [pltpu.VMEM((seq // KT, KV_W, KT), BF16)] * 4
        + [pltpu.VMEM((LANES, seq), BF16), pltpu.VMEM((N_KV_HEADS, seq // KT, QB, KT), F32),
           pltpu.VMEM((N_KV_HEADS, seq // KT, rows, KT), F32)]
        + [pltpu.VMEM((N_KV_HEADS, rows, LANES), F32)] * 4
        + [pltpu.VMEM((N_KV_HEADS, QB // LANES, seq // CMP_BLOCK, LANES), F32),
           pltpu.VMEM((N_KV_HEADS, LANES, rows), F32),
           pltpu.VMEM((N_KV_HEADS, rows, LANES), F32)],
        compiler_params=_cparams("arbitrary", "arbitrary"),
        name="attn_prompt",
    )(q, kv_t, kvw_t, gate, w_pool_k, w_pool_v)


def _attn_sample_kernel(pt_ref, q_ref, kvn_ref, kvwn_ref, gate_ref, wk_ref, wv_ref, kvwt_ref, cache_ref, win_ref,
                        out_ref, win_out_ref, past_ref, exp_ref, sem):
    step = pl.program_id(0)
    n_steps = pl.num_programs(0)
    n_pages = pt_ref.shape[1]
    page = cache_ref.shape[2]
    past_len = n_pages * page
    n_blk = past_len // SEL_BLOCK + 1
    w_buf = win_ref.shape[2]
    slot = step % 2

    def page_copy(st, sl, g, p):
        return pltpu.make_async_copy(cache_ref.at[pt_ref[st * DEC_GROUP + g, p]],
                                     past_ref.at[sl, g, :, pl.ds(p * page, page)], sem.at[sl, g, p])

    def start_pages(st, sl):
        for g in range(DEC_GROUP):
            for p in range(n_pages):
                page_copy(st, sl, g, p).start()

    @pl.when(step == 0)
    def _():
        start_pages(0, 0)
        exp_ref[...] = _block_expander(past_len, SEL_BLOCK // CMP_BLOCK)

    @pl.when(step + 1 < n_steps)
    def _():
        start_pages(step + 1, 1 - slot)

    q_pos = jnp.full((N_HEADS, 1), past_len, jnp.int32)

    def new_key_softmax(qp_f, s_past, ok_past, vt_past, k_new, v_new, ok_new):
        s_new = jnp.sum(qp_f * k_new, axis=-1, keepdims=True)
        s_past = jnp.where(ok_past, s_past, NEG_INF)
        s_new = jnp.where(ok_new, s_new, NEG_INF)
        m = jnp.maximum(jnp.max(s_past, axis=-1, keepdims=True), s_new)
        p_past = jnp.where(ok_past, jnp.exp2(s_past - m), 0.0)
        p_new = jnp.where(ok_new, jnp.exp2(s_new - m), 0.0)
        l = jnp.sum(p_past, axis=-1, keepdims=True) + p_new
        o = _dot_nt(p_past.astype(BF16), vt_past) + p_new * v_new
        return o * jnp.where(l > 0.0, 1.0 / l, 0.0)

    qp_f, o_win = [], []
    for g in range(DEC_GROUP):
        b = step * DEC_GROUP + g
        qf = jnp.concatenate([r for kvh in range(N_KV_HEADS) for r in _head_rows(q_ref[g], kvh)], axis=0)
        qp_f.append(qf)
        kvw_new = kvwn_ref[g]
        win = win_ref[g]
        rel = past_len - (past_len - w_buf + _lane_iota((N_HEADS, w_buf)))
        o_win.append(new_key_softmax(qf, jnp.dot(qf.astype(BF16), win[0:KV_W].astype(BF16),
                                                 preferred_element_type=F32),
                                     (rel >= 0) & (rel <= WINDOW), win[KV_W:2 * KV_W].astype(BF16),
                                     kvw_new[:, 0:KV_W], kvw_new[:, KV_W:2 * KV_W], jnp.full((N_HEADS, 1), True)))
        new_col = jnp.sum(jnp.where(_lane_iota(kvwt_ref.shape) == b, kvwt_ref[...], 0.0), axis=1, keepdims=True)
        win_out_ref[g] = jnp.where(_lane_iota(win.shape) == w_buf - 1, new_col, pltpu.roll(win, w_buf - 1, 1))

    for g in range(DEC_GROUP):
        for p in range(n_pages):
            page_copy(step, slot, g, p).wait()

    def slab(g, i):
        return past_ref[slot, g, i * KV_W:(i + 1) * KV_W, :].astype(BF16)

    elems = range(DEC_GROUP)
    qp = [qp_f[g].astype(BF16) for g in elems]
    kc_t = [_pool_blocks_t(slab(g, 0), wk_ref[...]).astype(BF16) for g in elems]
    vc_t = [_pool_blocks_t(slab(g, 1), wv_ref[...]).astype(BF16) for g in elems]
    s_sel = [jnp.dot(qp[g], slab(g, 2), preferred_element_type=F32) for g in elems]
    c_ok = (_lane_iota((N_HEADS, LANES)) + 1) * CMP_BLOCK - 1 <= q_pos
    o_cmp, sel = [], []
    for g in elems:
        s_c = jnp.where(c_ok, jnp.dot(qp[g], kc_t[g], preferred_element_type=F32), NEG_INF)
        p_c = jnp.where(c_ok, jnp.exp2(s_c - jnp.max(s_c, axis=-1, keepdims=True)), 0.0)
        l_c = jnp.sum(p_c, axis=-1, keepdims=True)
        inv_c = jnp.where(l_c > 0.0, 1.0 / l_c, 0.0)
        o_cmp.append(_dot_nt(p_c.astype(BF16), vc_t[g]) * inv_c)
        p_cmp = p_c * inv_c
        psum = jnp.concatenate(
            [jnp.broadcast_to(jnp.sum(p_cmp[kvh * GROUP:(kvh + 1) * GROUP], axis=0, keepdims=True), (GROUP, LANES))
             for kvh in range(N_KV_HEADS)], axis=0)
        sel.append(_select_blocks(psum, q_pos, n_blk))

    new_blk = past_len // SEL_BLOCK
    for g in elems:
        kv_new = kvn_ref[g]
        selexp = jnp.dot(sel[g].astype(BF16), exp_ref[...], preferred_element_type=F32)
        ok_new = sel[g][:, 2 * new_blk:2 * new_blk + 1] > 0.5
        ok_past = (selexp > 0.5) & (_lane_iota((N_HEADS, past_len)) <= q_pos)
        o_sel = new_key_softmax(qp_f[g], s_sel[g], ok_past, slab(g, 3),
                                kv_new[:, 2 * KV_W:3 * KV_W], kv_new[:, 3 * KV_W:4 * KV_W], ok_new)
        g_cmp, g_sel, g_win = _gate_cols(gate_ref[g], range(N_HEADS))
        o = o_cmp[g] * g_cmp + o_sel * g_sel + o_win[g] * g_win
        out_ref[g] = _place_heads([o[h:h + 1] for h in range(N_HEADS)])


def _attn_sample(page_table, q, kv_new, kvw_new, gate, w_pool_k, w_pool_v, cache_t, win_t):
    nb, n_pages = page_table.shape
    n_feat, page = cache_t.shape[1], cache_t.shape[2]
    w_buf = win_t.shape[2]
    past_len = n_pages * page
    assert nb % DEC_GROUP == 0
    row3 = lambda w: pl.BlockSpec((DEC_GROUP, 1, w), lambda b, pt: (b, 0, 0))
    full = lambda shape: pl.BlockSpec(shape, lambda b, pt: (0,) * len(shape))
    win_spec = pl.BlockSpec((DEC_GROUP, 2 * KV_W, w_buf), lambda b, pt: (b, 0, 0))
    grid_spec = pltpu.PrefetchScalarGridSpec(
        num_scalar_prefetch=1,
        grid=(nb // DEC_GROUP,),
        in_specs=[row3(N_HEADS * HEAD_DIM), row3(N_PAGED_SLOTS * KV_W), row3(2 * KV_W), row3(LANES),
                  full((past_len, LANES)), full((past_len, LANES)), full((2 * KV_W, nb)),
                  pl.BlockSpec(memory_space=pl.ANY), win_spec],
        out_specs=[row3(N_HEADS * HEAD_DIM), win_spec],
        scratch_shapes=[pltpu.VMEM((2, DEC_GROUP, n_feat, past_len), F32),
                        pltpu.VMEM((LANES, past_len), BF16),
                        pltpu.SemaphoreType.DMA((2, DEC_GROUP, n_pages))],
    )
    r3 = lambda a: a.reshape(nb, 1, a.shape[-1])
    return pl.pallas_call(
        _attn_sample_kernel,
        grid_spec=grid_spec,
        out_shape=[jax.ShapeDtypeStruct((nb, 1, N_HEADS * HEAD_DIM), F32),
                   jax.ShapeDtypeStruct(win_t.shape, F32)],
        compiler_params=_cparams("arbitrary"),
        name="attn_sample",
    )(page_table, r3(q), r3(kv_new), r3(kvw_new), r3(gate), w_pool_k, w_pool_v, kvw_new.T, cache_t, win_t)


def _rms(x, g):
    return x * lax.rsqrt(jnp.mean(x * x, axis=-1, keepdims=True) + RMS_EPS) * g


def _mlp_kernel(x_ref, conv_ref, attn_ref, wout_ref, gm_ref, wup_ref, wdn_ref, gf_ref, y_ref, h_ref):
    h_ref[...] = (x_ref[...]
                  + jnp.dot(conv_ref[...].astype(BF16), wout_ref[0:C_CONV, :], preferred_element_type=F32)
                  + jnp.dot(attn_ref[...].astype(BF16), wout_ref[C_CONV:, :], preferred_element_type=F32))
    hn = _rms(h_ref[...], gm_ref[...]).astype(BF16)
    acc = None
    for c in range(D_FF // FF_CHUNK):
        m = jnp.dot(hn, wup_ref[:, c * FF_CHUNK:(c + 1) * FF_CHUNK], preferred_element_type=F32)
        m = jnp.maximum(m, 0.0)
        t = jnp.dot((m * m).astype(BF16), wdn_ref[c * FF_CHUNK:(c + 1) * FF_CHUNK, :], preferred_element_type=F32)
        acc = t if acc is None else acc + t
    y_ref[...] = _rms(h_ref[...] + acc, gf_ref[...])


def _mlp(x, conv_y, attn_y, w_out, g_mlp, w_up, w_down, g_final, tm):
    n = x.shape[0]
    row = lambda w: pl.BlockSpec((tm, w), lambda i: (i, 0))
    once = lambda shape: pl.BlockSpec(shape, lambda i: (0,) * len(shape), pipeline_mode=pl.Buffered(1))
    return pl.pallas_call(
        _mlp_kernel,
        grid=(n // tm,),
        in_specs=[row(D_MODEL), row(C_CONV), row(N_HEADS * HEAD_DIM), once((D_MODEL, D_MODEL)), once((1, D_MODEL)),
                  once((D_MODEL, D_FF)), once((D_FF, D_MODEL)), once((1, D_MODEL))],
        out_specs=row(D_MODEL),
        out_shape=jax.ShapeDtypeStruct((n, D_MODEL), F32),
        scratch_shapes=[pltpu.VMEM((tm, D_MODEL), F32)],
        compiler_params=_cparams("parallel"),
        name="mlp",
    )(x, conv_y, attn_y, w_out, g_mlp, w_up, w_down, g_final)


def kernel(x_prompt, x_sample, cache_kv, cache_win, state_conv, page_table, g_attn_norm, w_in, w_dw, b_dw,
           conv_ln_g, conv_ln_b, w_cmp_k, w_cmp_v, w_out, g_mlp_norm, w_up, w_down, g_final):
    depth = w_in.shape[0]
    assert depth == 1, "single-layer trunk"
    nb, seq, _ = x_prompt.shape
    db, dseq, _ = x_sample.shape
    assert dseq == 1
    n_phys, page = cache_kv.shape[1], cache_kv.shape[2]
    w_buf = cache_win.shape[2]

    w_main = jnp.concatenate([w_in[0][:, :COL_KV], w_in[0][:, COL_GATE:],
                              jnp.zeros((D_MODEL, IN_COLS_PAD - IN_COLS), F32)], axis=1).astype(BF16)
    w_kv = jnp.transpose(w_in[0][:, COL_KV:COL_GATE]).astype(BF16)
    w_out_b, w_up_b, w_down_b = w_out[0].astype(BF16), w_up[0].astype(BF16), w_down[0].astype(BF16)
    row = lambda a: a.reshape(1, -1)
    g_attn, g_mlp, g_fin = row(g_attn_norm[0]), row(g_mlp_norm[0]), row(g_final)
    conv_w = (w_dw[0], row(b_dw[0]), row(conv_ln_g[0]), row(conv_ln_b[0]))
    mlp_w = (w_out_b, g_mlp, w_up_b, w_down_b, g_fin)

    xp = x_prompt.reshape(nb * seq, D_MODEL)
    glu, q, gate, kv_t, kvw_t = _project(xp, g_attn, w_main, w_kv, PROJ_ROWS, seq)
    per_b = lambda a: a.reshape(nb, seq, a.shape[-1])
    glu3 = per_b(glu)
    conv_y = _conv_prompt(glu3, *conv_w)
    attn_y = _attn_prompt(per_b(q), kv_t, kvw_t, per_b(gate), _pool_matrix(w_cmp_k[0], seq),
                          _pool_matrix(w_cmp_v[0], seq))
    y_prompt = _mlp(xp, conv_y.reshape(nb * seq, C_CONV), attn_y.reshape(nb * seq, -1), *mlp_w, MLP_ROWS)
    pos_major = lambda a, n_slots: jnp.transpose(
        a.reshape(a.shape[0], n_slots, N_KV_HEADS, HEAD_DIM, a.shape[-1]), (0, 4, 1, 2, 3))[None]
    new_kv_p = pos_major(kv_t, N_PAGED_SLOTS)
    new_win_p = pos_major(kvw_t[:, :, seq - min(WINDOW, seq):], 2)
    new_conv_p = glu3[:, seq - (CONV_WIDTH - 1):][None]

    xs = x_sample.reshape(db, D_MODEL)
    glu_s, q_s, gate_s, kv_s, kvw_s = _project(xs, g_attn, w_main, w_kv, db)
    conv_y_s, conv_st = _conv_sample(jnp.transpose(state_conv[0], (1, 0, 2)), glu_s, *conv_w)
    cache_t = jnp.transpose(cache_kv[0], (0, 2, 3, 4, 1)).reshape(n_phys, N_PAGED_SLOTS * KV_W, page)
    win_t = jnp.transpose(cache_win[0], (0, 2, 3, 4, 1)).reshape(db, 2 * KV_W, w_buf)
    n_past = page_table.shape[1] * page
    attn_s, new_win_t = _attn_sample(page_table, q_s, kv_s, kvw_s, gate_s, _pool_matrix(w_cmp_k[0], n_past),
                                     _pool_matrix(w_cmp_v[0], n_past), cache_t, win_t)
    new_win_s = pos_major(new_win_t, 2)
    y_sample = _mlp(xs, conv_y_s, attn_s.reshape(db, -1), *mlp_w, db)

    return (y_prompt.reshape(nb, seq, D_MODEL), y_sample.reshape(db, 1, D_MODEL), new_kv_p, new_win_p, new_conv_p,
            kv_s.reshape(1, db, 1, N_PAGED_SLOTS, N_KV_HEADS, HEAD_DIM), new_win_s,
            jnp.transpose(conv_st, (1, 0, 2))[None])
```

```python
import functools

import jax
import jax.numpy as jnp
from jax import lax
from jax.experimental import pallas as pl
from jax.experimental.pallas import tpu as pltpu

F32 = jnp.float32
BF16 = jnp.bfloat16

D_MODEL = 1024
C_CONV = 512
CONV_WIDTH = 31
N_HEADS = 8
HEAD_DIM = 64
N_KV_HEADS = 2
GROUP = N_HEADS // N_KV_HEADS
KV_W = N_KV_HEADS * HEAD_DIM
N_PAGED_SLOTS = 4
D_FF = 4 * D_MODEL
CMP_BLOCK = 32
SEL_BLOCK = 64
TOP_N = 16
N_LOCAL = 2
WINDOW = 512
FORCE_BONUS = float(GROUP + 1)
NEG_INF = -1e30
RMS_EPS = 1e-6
LN_EPS = 1e-5
ATTN_SCALE = HEAD_DIM ** -0.5
LOG2E = 1.4426950408889634

COL_Q = 2 * C_CONV
COL_KV = COL_Q + N_HEADS * HEAD_DIM
COL_WIN = COL_KV + N_PAGED_SLOTS * KV_W
COL_GATE = COL_WIN + 2 * KV_W
IN_COLS = COL_GATE + 3 * N_HEADS
LANES = 128
SUBLANES = 8
IN_COLS_PAD = -(-IN_COLS // LANES) * LANES

VMEM_LIMIT = 56 * 1024 * 1024

PROJ_ROWS = 1024
MLP_ROWS = 512
CONV_ROWS = 128
CONV_PAD = 32
NORM_ROWS = 512
QB = 256
KT = 256
WIN_KEYS = WINDOW + QB
assert QB % KT == 0 and WIN_KEYS % KT == 0
DEC_GROUP = 4
FF_CHUNK = 1024


def _cparams(*sem):
    return pltpu.CompilerParams(dimension_semantics=sem, vmem_limit_bytes=VMEM_LIMIT)


def _const_spec(shape):
    return pl.BlockSpec(shape, lambda *_: (0,) * len(shape))


def _dot_nt(a, b):
    return lax.dot_general(a, b, (((1,), (1,)), ((), ())), preferred_element_type=F32)


def _lane_iota(shape):
    return lax.broadcasted_iota(jnp.int32, shape, len(shape) - 1)


def _proj_kernel(x_ref, g_ref, w_ref, wkv_ref, glu_ref, q_ref, gate_ref, kv_ref, kvw_ref, *, kv_t):
    x = x_ref[...]
    ms = jnp.mean(x * x, axis=-1, keepdims=True)
    xn = (x * lax.rsqrt(ms + RMS_EPS) * g_ref[...]).astype(BF16)
    z = jnp.dot(xn, w_ref[...], preferred_element_type=F32)
    glu_ref[...] = z[:, :C_CONV] * jax.nn.sigmoid(z[:, C_CONV:COL_Q])
    q_ref[...] = z[:, COL_Q:COL_KV]
    gate_ref[...] = jax.nn.sigmoid(z[:, COL_KV:])
    n_paged = N_PAGED_SLOTS * KV_W
    if kv_t:
        kv = _dot_nt(wkv_ref[...], xn)
        kv_ref[0] = kv[:n_paged]
        kvw_ref[0] = kv[n_paged:]
    else:
        kv = _dot_nt(xn, wkv_ref[...])
        kv_ref[...] = kv[:, :n_paged]
        kvw_ref[...] = kv[:, n_paged:]


def _project(x, g, w_main, w_kv, tm, seq=None):
    n = x.shape[0]
    row = lambda w: pl.BlockSpec((tm, w), lambda i: (i, 0))
    n_paged = N_PAGED_SLOTS * KV_W
    widths = (C_CONV, N_HEADS * HEAD_DIM, LANES)
    out_specs = [row(w) for w in widths]
    out_shape = [jax.ShapeDtypeStruct((n, w), F32) for w in widths]
    for w in (n_paged, 2 * KV_W):
        if seq is None:
            out_specs.append(row(w))
            out_shape.append(jax.ShapeDtypeStruct((n, w), F32))
        else:
            per_seq = seq // tm
            out_specs.append(pl.BlockSpec((1, w, tm), lambda i: (i // per_seq, 0, i % per_seq)))
            out_shape.append(jax.ShapeDtypeStruct((n // seq, w, seq), F32))
    return pl.pallas_call(
        functools.partial(_proj_kernel, kv_t=seq is not None),
        grid=(n // tm,),
        in_specs=[row(D_MODEL), _const_spec((1, D_MODEL)), _const_spec(w_main.shape), _const_spec(w_kv.shape)],
        out_specs=out_specs,
        out_shape=out_shape,
        compiler_params=_cparams("parallel"),
        name="projection",
    )(x, g, w_main, w_kv)


def _ln_swish(y, lng, lnb):
    mu = jnp.mean(y, axis=-1, keepdims=True)
    d = y - mu
    var = jnp.mean(d * d, axis=-1, keepdims=True)
    y = d * lax.rsqrt(var + LN_EPS) * lng + lnb
    return y * jax.nn.sigmoid(y)


def _conv_prompt_kernel(glu_ref, w_ref, b_ref, lng_ref, lnb_ref, y_ref, ext_ref, stage_ref):
    seq = glu_ref.shape[1]
    n_ct = C_CONV // LANES
    for ct in range(n_ct):
        ext_ref[ct, 0:CONV_PAD, :] = jnp.zeros((CONV_PAD, LANES), F32)
        ext_ref[ct, CONV_PAD:, :] = glu_ref[0, :, ct * LANES:(ct + 1) * LANES]
    shift = CONV_PAD - (CONV_WIDTH - 1)

    def body(c, carry):
        base = pl.multiple_of(c * CONV_ROWS, CONV_ROWS)

        def tile(ct, carry_ct):
            win = ext_ref[ct, pl.ds(base, CONV_ROWS + CONV_PAD), :]
            parts = []
            for r in range(SUBLANES):
                shifted = win if r == 0 else pltpu.roll(win, CONV_ROWS + CONV_PAD - r, 0)
                part = None
                for j in range(CONV_WIDTH):
                    if (j + shift) % SUBLANES == r:
                        a = j + shift - r
                        term = w_ref[ct, j:j + 1, :] * shifted[a:a + CONV_ROWS]
                        part = term if part is None else part + term
                parts.append(part)
            while len(parts) > 1:
                parts = [parts[i] + parts[i + 1] for i in range(0, len(parts) - 1, 2)] + parts[len(parts) & ~1:]
            stage_ref[ct, pl.ds(base, CONV_ROWS), :] = parts[0]
            return carry_ct

        lax.fori_loop(0, n_ct, tile, 0)
        return carry

    lax.fori_loop(0, seq // CONV_ROWS, body, 0)

    def norm(c, carry):
        base = pl.multiple_of(c * NORM_ROWS, NORM_ROWS)
        y = jnp.concatenate([stage_ref[ct, pl.ds(base, NORM_ROWS), :] for ct in range(n_ct)], axis=1) + b_ref[...]
        y_ref[0, pl.ds(base, NORM_ROWS), :] = _ln_swish(y, lng_ref[...], lnb_ref[...])
        return carry

    lax.fori_loop(0, seq // NORM_ROWS, norm, 0)


def _conv_prompt(glu, w_dw, b_dw, lng, lnb):
    nb, seq, _ = glu.shape
    n_ct = C_CONV // LANES
    w_tiles = jnp.transpose(w_dw.reshape(CONV_WIDTH, n_ct, LANES), (1, 0, 2))
    blk = pl.BlockSpec((1, seq, C_CONV), lambda b: (b, 0, 0))
    return pl.pallas_call(
        _conv_prompt_kernel,
        grid=(nb,),
        in_specs=[blk, _const_spec((n_ct, CONV_WIDTH, LANES))] + [_const_spec((1, C_CONV))] * 3,
        out_specs=blk,
        out_shape=jax.ShapeDtypeStruct(glu.shape, F32),
        scratch_shapes=[pltpu.VMEM((n_ct, seq + CONV_PAD, LANES), F32), pltpu.VMEM((n_ct, seq, LANES), F32)],
        compiler_params=_cparams("parallel"),
        name="conv_prompt",
    )(glu, w_tiles, b_dw, lng, lnb)


def _conv_sample_kernel(st_ref, glu_ref, w_ref, b_ref, lng_ref, lnb_ref, y_ref, st_out_ref):
    n_prev = CONV_WIDTH - 1
    glu = glu_ref[...]
    acc = w_ref[n_prev:n_prev + 1, :] * glu
    for j in range(n_prev):
        acc = acc + w_ref[j:j + 1, :] * st_ref[j]
    y_ref[...] = _ln_swish(acc + b_ref[...], lng_ref[...], lnb_ref[...])
    st_out_ref[0:n_prev - 1] = st_ref[1:n_prev]
    st_out_ref[n_prev - 1] = glu


def _conv_sample(state_t, glu, w_dw, b_dw, lng, lnb):
    nb = glu.shape[0]
    return pl.pallas_call(
        _conv_sample_kernel,
        grid=(1,),
        in_specs=[_const_spec(state_t.shape), _const_spec((nb, C_CONV)), _const_spec((CONV_WIDTH, C_CONV))]
        + [_const_spec((1, C_CONV))] * 3,
        out_specs=[_const_spec((nb, C_CONV)), _const_spec(state_t.shape)],
        out_shape=[jax.ShapeDtypeStruct((nb, C_CONV), F32), jax.ShapeDtypeStruct(state_t.shape, F32)],
        compiler_params=_cparams("arbitrary"),
        name="conv_sample",
    )(state_t, glu, w_dw, b_dw, lng, lnb)


def _head_rows(q, kvh):
    rows = q.shape[0]
    half = _lane_iota((rows, KV_W)) // HEAD_DIM
    out = []
    for g in range(GROUP):
        h = kvh * GROUP + g
        slab = q[:, KV_W * (h // 2):KV_W * (h // 2 + 1)]
        if h % 2 != kvh:
            slab = pltpu.roll(slab, HEAD_DIM, 1)
        out.append(jnp.where(half == kvh, slab * (ATTN_SCALE * LOG2E), 0.0))
    return out


def _pool_matrix(w_cmp, n_keys):
    eye = jnp.eye(n_keys // CMP_BLOCK, LANES // N_KV_HEADS, dtype=F32)
    per_head = [(eye[:, None, :] * w_cmp[None, :, h, None]).reshape(n_keys, -1) for h in range(N_KV_HEADS)]
    return jnp.concatenate(per_head, axis=1).astype(BF16)


def _pool_blocks_t(x_t, w_pool):
    res = jnp.dot(x_t, w_pool, preferred_element_type=F32)
    row = lax.broadcasted_iota(jnp.int32, res.shape, 0)
    own = jnp.where(row < HEAD_DIM, res, pltpu.roll(res, HEAD_DIM, 1))
    return jnp.where(_lane_iota(res.shape) < w_pool.shape[0] // CMP_BLOCK, own, 0.0)


def _select_blocks(psum, q_pos, n_blk):
    shape = psum.shape
    lane = _lane_iota(shape)
    even = (lane & 1) == 0
    pair = psum + jnp.where(even, pltpu.roll(psum, LANES - 1, 1), pltpu.roll(psum, 1, 1))
    blk = lane // (SEL_BLOCK // CMP_BLOCK)
    back = q_pos // SEL_BLOCK - blk
    forced = (blk == 0) | ((back >= 0) & (back < N_LOCAL))
    score = jnp.where(back >= 0, pair + jnp.where(forced, FORCE_BONUS, 0.0), -jnp.inf)
    rank = jnp.zeros(shape, jnp.int32)
    for i in range(n_blk):
        si = score[:, 2 * i:2 * i + 1]
        beats = (si > score) | ((si == score) & (i < blk))
        rank = rank + beats.astype(jnp.int32)
    return jnp.where((rank < min(TOP_N, n_blk)) & even, 1.0, 0.0)


def _select_bias_t(p_blks, q_pos):
    n_blk = p_blks[0].shape[0]
    blk = lax.broadcasted_iota(jnp.int32, p_blks[0].shape, 0)
    back = q_pos // SEL_BLOCK - blk
    forced = (blk == 0) | ((back >= 0) & (back < N_LOCAL))
    bonus = jnp.where(forced, FORCE_BONUS, 0.0)
    scores = [jnp.where(back >= 0, p + bonus, -jnp.inf) for p in p_blks]
    ranks = [jnp.zeros(blk.shape, jnp.int32) for _ in p_blks]
    for i in range(n_blk):
        for k, score in enumerate(scores):
            si = score[i:i + 1, :]
            beats = (si > score) | ((si == score) & (i < blk))
            ranks[k] = ranks[k] + beats.astype(jnp.int32)
    return [jnp.where((rank < min(TOP_N, n_blk)) & (back >= 0), 0.0, NEG_INF) for rank in ranks]


def _block_expander(n_keys, row_stride):
    li = lax.broadcasted_iota(jnp.int32, (LANES, n_keys), 0)
    ti = lax.broadcasted_iota(jnp.int32, (LANES, n_keys), 1)
    return jnp.where(li == row_stride * (ti // SEL_BLOCK), 1.0, 0.0).astype(BF16)


def _gate_cols(gates, heads):
    return [jnp.concatenate([gates[:, 3 * h + k:3 * h + k + 1] for h in heads], axis=0) for k in range(3)]


def _place_heads(o_heads):
    rows = o_heads[0].shape[0]
    low = _lane_iota((rows, KV_W)) < HEAD_DIM
    slabs = []
    for s in range(N_HEADS // 2):
        pair = []
        for h in (2 * s, 2 * s + 1):
            o = o_heads[h]
            if h // GROUP != h % 2:
                o = pltpu.roll(o, HEAD_DIM, 1)
            pair.append(o)
        slabs.append(jnp.where(low, pair[0], pair[1]))
    return jnp.concatenate(slabs, axis=1)


def _attn_prompt_kernel(q_ref, kv_ref, kvw_ref, gate_ref, wk_ref, wv_ref, out_ref,
                        kc_ref, vc_ref, ks_ref, vs_ref, kw_ref, vw_ref, exp_ref,
                        bias_ref, s_ref, mrun_ref, mb_ref, lrun_ref, acc_ref, psum_ref, oct_ref, ow_ref):
    seq = kv_ref.shape[2]
    n_cmp = seq // CMP_BLOCK
    n_blk = seq // SEL_BLOCK
    n_kt = seq // KT
    n_wt = WIN_KEYS // KT
    rows = GROUP * QB
    qi = pl.program_id(1)
    qs = qi * QB

    @pl.when((pl.program_id(0) == 0) & (qi == 0))
    def _():
        exp_ref[...] = _block_expander(seq, 1)

    @pl.when(qi == 0)
    def _():
        kc_ref[...] = _pool_blocks_t(kv_ref[0, 0:KV_W, :].astype(BF16), wk_ref[...]).T[0:n_cmp].astype(BF16)
        vc_ref[...] = _pool_blocks_t(kv_ref[0, KV_W:2 * KV_W, :].astype(BF16), wv_ref[...]).astype(BF16)
        for t in range(n_kt):
            ks_ref[t] = kv_ref[0, 2 * KV_W:3 * KV_W, t * KT:(t + 1) * KT].astype(BF16)
            vs_ref[t] = kv_ref[0, 3 * KV_W:4 * KV_W, t * KT:(t + 1) * KT].astype(BF16)
            kw_ref[t] = kvw_ref[0, 0:KV_W, t * KT:(t + 1) * KT].astype(BF16)
            vw_ref[t] = kvw_ref[0, KV_W:2 * KV_W, t * KT:(t + 1) * KT].astype(BF16)

    q = q_ref[0]
    q_pos = qs + lax.broadcasted_iota(jnp.int32, (QB, 1), 0)
    q_pos_t = qs + lax.broadcasted_iota(jnp.int32, (1, QB), 1)
    add_bias = lambda s, b: (s.reshape(GROUP, QB, s.shape[-1]) + b[None]).reshape(s.shape)
    n_tiles = (qs + QB + KT - 1) // KT
    qp = [jnp.concatenate(_head_rows(q, kvh), axis=0).astype(BF16) for kvh in range(N_KV_HEADS)]

    win_t0 = jnp.maximum(qs - WINDOW, 0) // KT
    rel = q_pos - (win_t0 * KT + _lane_iota((QB, WIN_KEYS)))
    win_bias = jnp.where((rel >= 0) & (rel <= WINDOW), 0.0, NEG_INF)
    kv_heads = range(N_KV_HEADS)
    gates = [_gate_cols(gate_ref[0], [kvh * GROUP + g for g in range(GROUP)]) for kvh in kv_heads]
    s_w = [add_bias(jnp.concatenate([jnp.dot(qp[kvh], kw_ref[win_t0 + c], preferred_element_type=F32)
                                     for c in range(n_wt)], axis=1), win_bias) for kvh in kv_heads]
    p_w = [jnp.exp2(s_w[kvh] - jnp.max(s_w[kvh], axis=-1, keepdims=True)) for kvh in kv_heads]
    for kvh in kv_heads:
        o_win = sum(_dot_nt(p_w[kvh][:, c * KT:(c + 1) * KT].astype(BF16), vw_ref[win_t0 + c]) for c in range(n_wt))
        ow_ref[kvh] = o_win * (gates[kvh][2] / jnp.sum(p_w[kvh], axis=-1, keepdims=True))

    c_end_t = (lax.broadcasted_iota(jnp.int32, (n_cmp, rows), 0) + 1) * CMP_BLOCK - 1
    ok_t = c_end_t <= jnp.concatenate([q_pos_t] * GROUP, axis=1)
    gate_t = gate_ref[0].T
    per_sel = SEL_BLOCK // CMP_BLOCK
    o_cmp, p_blk, p_ts = [], [], []
    for kvh in kv_heads:
        s_t = jnp.where(ok_t, _dot_nt(kc_ref[...], qp[kvh]), NEG_INF)
        p_t = jnp.where(ok_t, jnp.exp2(s_t - jnp.max(s_t, axis=0, keepdims=True)), 0.0)
        l_t = jnp.sum(p_t, axis=0, keepdims=True)
        p_ts.append(p_t * jnp.where(l_t > 0.0, 1.0 / l_t, 0.0))
    for kvh in kv_heads:
        g_row = jnp.concatenate([gate_t[3 * (kvh * GROUP + g):3 * (kvh * GROUP + g) + 1] for g in range(GROUP)],
                                axis=1)
        oct_ref[kvh] = jnp.dot(vc_ref[:, 0:n_cmp], (p_ts[kvh] * g_row).astype(BF16), preferred_element_type=F32)
        o_cmp.append(oct_ref[kvh].T)
    for kvh in kv_heads:
        psum = sum(p_ts[kvh][:, g * QB:(g + 1) * QB] for g in range(GROUP))
        tiles = []
        for c in range(QB // LANES):
            psum_ref[kvh, c] = psum[:, c * LANES:(c + 1) * LANES]
            tiles.append(sum(psum_ref[kvh, c, pl.ds(j, n_blk, stride=per_sel), :] for j in range(per_sel)))
        p_blk.append(jnp.concatenate(tiles, axis=1))
    blk_bias = _select_bias_t(p_blk, q_pos_t)
    last = n_tiles - 1
    diag_bias = jnp.where(last * KT + _lane_iota((QB, KT)) <= q_pos, 0.0, NEG_INF)
    for kvh in kv_heads:
        bb = jnp.concatenate([blk_bias[kvh], jnp.zeros((LANES - n_blk, QB), F32)], axis=0).T
        key_bias = jnp.dot(bb.astype(BF16), exp_ref[...], preferred_element_type=F32)
        for t in range(n_kt):
            bias_ref[kvh, t] = key_bias[:, t * KT:(t + 1) * KT]
        bias_ref[kvh, last] += diag_bias

    mrun_ref[...] = jnp.full(mrun_ref.shape, NEG_INF, F32)

    def tile_pairs(tiles_fn):
        def pair(u, carry):
            tiles_fn(2 * u, 2)
            return carry

        lax.fori_loop(0, n_tiles // 2, pair, 0)

        @pl.when(n_tiles % 2 == 1)
        def _():
            tiles_fn(last, 1)

    def max_tiles(t0, n):
        raw = [[jnp.dot(qp[kvh], ks_ref[t0 + i], preferred_element_type=F32) for kvh in kv_heads]
               for i in range(n)]
        for kvh in kv_heads:
            m = None
            for i in range(n):
                s = add_bias(raw[i][kvh], bias_ref[kvh, t0 + i])
                s_ref[kvh, t0 + i] = s
                m_i = jnp.maximum(s[:, :LANES], s[:, LANES:])
                m = m_i if m is None else jnp.maximum(m, m_i)
            mrun_ref[kvh] = jnp.maximum(mrun_ref[kvh], m)

    tile_pairs(max_tiles)
    for kvh in kv_heads:
        mb_ref[kvh] = jnp.broadcast_to(jnp.max(mrun_ref[kvh], axis=-1, keepdims=True), (rows, LANES))
    lrun_ref[...] = jnp.zeros(lrun_ref.shape, F32)
    acc_ref[...] = jnp.zeros(acc_ref.shape, F32)

    def sum_tiles(t0, n):
        for kvh in kv_heads:
            mb = mb_ref[kvh]
            l_add, o_add = None, None
            for i in range(n):
                s = s_ref[kvh, t0 + i]
                p = jnp.exp2(jnp.concatenate([s[:, :LANES] - mb, s[:, LANES:] - mb], axis=1))
                l_i = p[:, :LANES] + p[:, LANES:]
                o_i = _dot_nt(p.astype(BF16), vs_ref[t0 + i])
                l_add = l_i if l_add is None else l_add + l_i
                o_add = o_i if o_add is None else o_add + o_i
            lrun_ref[kvh] += l_add
            acc_ref[kvh] += o_add

    tile_pairs(sum_tiles)

    o_heads = []
    for kvh in kv_heads:
        l = jnp.sum(lrun_ref[kvh], axis=-1, keepdims=True)
        o = o_cmp[kvh] + acc_ref[kvh] * (jnp.where(l > 0.0, 1.0 / l, 0.0) * gates[kvh][1]) + ow_ref[kvh]
        o_heads += [o[g * QB:(g + 1) * QB] for g in range(GROUP)]

    out_ref[0] = _place_heads(o_heads)


def _attn_prompt(q, kv_t, kvw_t, gate, w_pool_k, w_pool_v):
    nb, seq, _ = q.shape
    per_q = lambda w: pl.BlockSpec((1, QB, w), lambda b, i: (b, i, 0))
    per_b = lambda w: pl.BlockSpec((1, w, seq), lambda b, i: (b, 0, 0))
    wspec = pl.BlockSpec((seq, LANES), lambda b, i: (0, 0))
    rows = GROUP * QB
    return pl.pallas_call(
        _attn_prompt_kernel,
        grid=(nb, seq // QB),
        in_specs=[per_q(N_HEADS * HEAD_DIM), per_b(N_PAGED_SLOTS * KV_W), per_b(2 * KV_W), per_q(LANES), wspec, wspec],
        out_specs=per_q(N_HEADS * HEAD_DIM),
        out_shape=jax.ShapeDtypeStruct(q.shape, F32),
        scratch_shapes=[pltpu.VMEM((seq // CMP_BLOCK, KV_W), BF16), pltpu.VMEM((KV_W, LANES), BF16)]
        + [pltpu.VMEM((seq // KT, KV_W, KT), BF16)] * 4
        + [pltpu.VMEM((LANES, seq), BF16), pltpu.VMEM((N_KV_HEADS, seq // KT, QB, KT), F32),
           pltpu.VMEM((N_KV_HEADS, seq // KT, rows, KT), F32)]
        + [pltpu.VMEM((N_KV_HEADS, rows, LANES), F32)] * 4
        + [pltpu.VMEM((N_KV_HEADS, QB // LANES, seq // CMP_BLOCK, LANES), F32),
           pltpu.VMEM((N_KV_HEADS, LANES, rows), F32),
           pltpu.VMEM((N_KV_HEADS, rows, LANES), F32)],
        compiler_params=_cparams("arbitrary", "arbitrary"),
        name="attn_prompt",
    )(q, kv_t, kvw_t, gate, w_pool_k, w_pool_v)


def _attn_sample_kernel(pt_ref, q_ref, kvn_ref, kvwn_ref, gate_ref, wk_ref, wv_ref, kvwt_ref, cache_ref, win_ref,
                        out_ref, win_out_ref, past_ref, exp_ref, sem):
    step = pl.program_id(0)
    n_steps = pl.num_programs(0)
    n_pages = pt_ref.shape[1]
    page = cache_ref.shape[2]
    past_len = n_pages * page
    n_blk = past_len // SEL_BLOCK + 1
    w_buf = win_ref.shape[2]
    slot = step % 2

    def page_copy(st, sl, g, p):
        return pltpu.make_async_copy(cache_ref.at[pt_ref[st * DEC_GROUP + g, p]],
                                     past_ref.at[sl, g, :, pl.ds(p * page, page)], sem.at[sl, g, p])

    def start_pages(st, sl):
        for g in range(DEC_GROUP):
            for p in range(n_pages):
                page_copy(st, sl, g, p).start()

    @pl.when(step == 0)
    def _():
        start_pages(0, 0)
        exp_ref[...] = _block_expander(past_len, SEL_BLOCK // CMP_BLOCK)

    @pl.when(step + 1 < n_steps)
    def _():
        start_pages(step + 1, 1 - slot)

    q_pos = jnp.full((N_HEADS, 1), past_len, jnp.int32)

    def new_key_softmax(qp_f, s_past, ok_past, vt_past, k_new, v_new, ok_new):
        s_new = jnp.sum(qp_f * k_new, axis=-1, keepdims=True)
        s_past = jnp.where(ok_past, s_past, NEG_INF)
        s_new = jnp.where(ok_new, s_new, NEG_INF)
        m = jnp.maximum(jnp.max(s_past, axis=-1, keepdims=True), s_new)
        p_past = jnp.where(ok_past, jnp.exp2(s_past - m), 0.0)
        p_new = jnp.where(ok_new, jnp.exp2(s_new - m), 0.0)
        l = jnp.sum(p_past, axis=-1, keepdims=True) + p_new
        o = _dot_nt(p_past.astype(BF16), vt_past) + p_new * v_new
        return o * jnp.where(l > 0.0, 1.0 / l, 0.0)

    qp_f, o_win = [], []
    for g in range(DEC_GROUP):
        b = step * DEC_GROUP + g
        qf = jnp.concatenate([r for kvh in range(N_KV_HEADS) for r in _head_rows(q_ref[g], kvh)], axis=0)
        qp_f.append(qf)
        kvw_new = kvwn_ref[g]
        win = win_ref[g]
        rel = past_len - (past_len - w_buf + _lane_iota((N_HEADS, w_buf)))
        o_win.append(new_key_softmax(qf, jnp.dot(qf.astype(BF16), win[0:KV_W].astype(BF16),
                                                 preferred_element_type=F32),
                                     (rel >= 0) & (rel <= WINDOW), win[KV_W:2 * KV_W].astype(BF16),
                                     kvw_new[:, 0:KV_W], kvw_new[:, KV_W:2 * KV_W], jnp.full((N_HEADS, 1), True)))
        new_col = jnp.sum(jnp.where(_lane_iota(kvwt_ref.shape) == b, kvwt_ref[...], 0.0), axis=1, keepdims=True)
        win_out_ref[g] = jnp.where(_lane_iota(win.shape) == w_buf - 1, new_col, pltpu.roll(win, w_buf - 1, 1))

    for g in range(DEC_GROUP):
        for p in range(n_pages):
            page_copy(step, slot, g, p).wait()

    def slab(g, i):
        return past_ref[slot, g, i * KV_W:(i + 1) * KV_W, :].astype(BF16)

    elems = range(DEC_GROUP)
    qp = [qp_f[g].astype(BF16) for g in elems]
    kc_t = [_pool_blocks_t(slab(g, 0), wk_ref[...]).astype(BF16) for g in elems]
    vc_t = [_pool_blocks_t(slab(g, 1), wv_ref[...]).astype(BF16) for g in elems]
    s_sel = [jnp.dot(qp[g], slab(g, 2), preferred_element_type=F32) for g in elems]
    c_ok = (_lane_iota((N_HEADS, LANES)) + 1) * CMP_BLOCK - 1 <= q_pos
    o_cmp, sel = [], []
    for g in elems:
        s_c = jnp.where(c_ok, jnp.dot(qp[g], kc_t[g], preferred_element_type=F32), NEG_INF)
        p_c = jnp.where(c_ok, jnp.exp2(s_c - jnp.max(s_c, axis=-1, keepdims=True)), 0.0)
        l_c = jnp.sum(p_c, axis=-1, keepdims=True)
        inv_c = jnp.where(l_c > 0.0, 1.0 / l_c, 0.0)
        o_cmp.append(_dot_nt(p_c.astype(BF16), vc_t[g]) * inv_c)
        p_cmp = p_c * inv_c
        psum = jnp.concatenate(
            [jnp.broadcast_to(jnp.sum(p_cmp[kvh * GROUP:(kvh + 1) * GROUP], axis=0, keepdims=True), (GROUP, LANES))
             for kvh in range(N_KV_HEADS)], axis=0)
        sel.append(_select_blocks(psum, q_pos, n_blk))

    new_blk = past_len // SEL_BLOCK
    for g in elems:
        kv_new = kvn_ref[g]
        selexp = jnp.dot(sel[g].astype(BF16), exp_ref[...], preferred_element_type=F32)
        ok_new = sel[g][:, 2 * new_blk:2 * new_blk + 1] > 0.5
        ok_past = (selexp > 0.5) & (_lane_iota((N_HEADS, past_len)) <= q_pos)
        o_sel = new_key_softmax(qp_f[g], s_sel[g], ok_past, slab(g, 3),
                                kv_new[:, 2 * KV_W:3 * KV_W], kv_new[:, 3 * KV_W:4 * KV_W], ok_new)
        g_cmp, g_sel, g_win = _gate_cols(gate_ref[g], range(N_HEADS))
        o = o_cmp[g] * g_cmp + o_sel * g_sel + o_win[g] * g_win
        out_ref[g] = _place_heads([o[h:h + 1] for h in range(N_HEADS)])


def _attn_sample(page_table, q, kv_new, kvw_new, gate, w_pool_k, w_pool_v, cache_t, win_t):
    nb, n_pages = page_table.shape
    n_feat, page = cache_t.shape[1], cache_t.shape[2]
    w_buf = win_t.shape[2]
    past_len = n_pages * page
    assert nb % DEC_GROUP == 0
    row3 = lambda w: pl.BlockSpec((DEC_GROUP, 1, w), lambda b, pt: (b, 0, 0))
    full = lambda shape: pl.BlockSpec(shape, lambda b, pt: (0,) * len(shape))
    win_spec = pl.BlockSpec((DEC_GROUP, 2 * KV_W, w_buf), lambda b, pt: (b, 0, 0))
    grid_spec = pltpu.PrefetchScalarGridSpec(
        num_scalar_prefetch=1,
        grid=(nb // DEC_GROUP,),
        in_specs=[row3(N_HEADS * HEAD_DIM), row3(N_PAGED_SLOTS * KV_W), row3(2 * KV_W), row3(LANES),
                  full((past_len, LANES)), full((past_len, LANES)), full((2 * KV_W, nb)),
                  pl.BlockSpec(memory_space=pl.ANY), win_spec],
        out_specs=[row3(N_HEADS * HEAD_DIM), win_spec],
        scratch_shapes=[pltpu.VMEM((2, DEC_GROUP, n_feat, past_len), F32),
                        pltpu.VMEM((LANES, past_len), BF16),
                        pltpu.SemaphoreType.DMA((2, DEC_GROUP, n_pages))],
    )
    r3 = lambda a: a.reshape(nb, 1, a.shape[-1])
    return pl.pallas_call(
        _attn_sample_kernel,
        grid_spec=grid_spec,
        out_shape=[jax.ShapeDtypeStruct((nb, 1, N_HEADS * HEAD_DIM), F32),
                   jax.ShapeDtypeStruct(win_t.shape, F32)],
        compiler_params=_cparams("arbitrary"),
        name="attn_sample",
    )(page_table, r3(q), r3(kv_new), r3(kvw_new), r3(gate), w_pool_k, w_pool_v, kvw_new.T, cache_t, win_t)


def _rms(x, g):
    return x * lax.rsqrt(jnp.mean(x * x, axis=-1, keepdims=True) + RMS_EPS) * g


def _mlp_kernel(x_ref, conv_ref, attn_ref, wout_ref, gm_ref, wup_ref, wdn_ref, gf_ref, y_ref, h_ref):
    h_ref[...] = (x_ref[...]
                  + jnp.dot(conv_ref[...].astype(BF16), wout_ref[0:C_CONV, :], preferred_element_type=F32)
                  + jnp.dot(attn_ref[...].astype(BF16), wout_ref[C_CONV:, :], preferred_element_type=F32))
    hn = _rms(h_ref[...], gm_ref[...]).astype(BF16)
    acc = None
    for c in range(D_FF // FF_CHUNK):
        m = jnp.dot(hn, wup_ref[:, c * FF_CHUNK:(c + 1) * FF_CHUNK], preferred_element_type=F32)
        m = jnp.maximum(m, 0.0)
        t = jnp.dot((m * m).astype(BF16), wdn_ref[c * FF_CHUNK:(c + 1) * FF_CHUNK, :], preferred_element_type=F32)
        acc = t if acc is None else acc + t
    y_ref[...] = _rms(h_ref[...] + acc, gf_ref[...])


def _mlp(x, conv_y, attn_y, w_out, g_mlp, w_up, w_down, g_final, tm):
    n = x.shape[0]
    row = lambda w: pl.BlockSpec((tm, w), lambda i: (i, 0))
    once = lambda shape: pl.BlockSpec(shape, lambda i: (0,) * len(shape), pipeline_mode=pl.Buffered(1))
    return pl.pallas_call(
        _mlp_kernel,
        grid=(n // tm,),
        in_specs=[row(D_MODEL), row(C_CONV), row(N_HEADS * HEAD_DIM), once((D_MODEL, D_MODEL)), once((1, D_MODEL)),
                  once((D_MODEL, D_FF)), once((D_FF, D_MODEL)), once((1, D_MODEL))],
        out_specs=row(D_MODEL),
        out_shape=jax.ShapeDtypeStruct((n, D_MODEL), F32),
        scratch_shapes=[pltpu.VMEM((tm, D_MODEL), F32)],
        compiler_params=_cparams("parallel"),
        name="mlp",
    )(x, conv_y, attn_y, w_out, g_mlp, w_up, w_down, g_final)


def kernel(x_prompt, x_sample, cache_kv, cache_win, state_conv, page_table, g_attn_norm, w_in, w_dw, b_dw,
           conv_ln_g, conv_ln_b, w_cmp_k, w_cmp_v, w_out, g_mlp_norm, w_up, w_down, g_final):
    depth = w_in.shape[0]
    assert depth == 1, "single-layer trunk"
    nb, seq, _ = x_prompt.shape
    db, dseq, _ = x_sample.shape
    assert dseq == 1
    n_phys, page = cache_kv.shape[1], cache_kv.shape[2]
    w_buf = cache_win.shape[2]

    w_main = jnp.concatenate([w_in[0][:, :COL_KV], w_in[0][:, COL_GATE:],
                              jnp.zeros((D_MODEL, IN_COLS_PAD - IN_COLS), F32)], axis=1).astype(BF16)
    w_kv = jnp.transpose(w_in[0][:, COL_KV:COL_GATE]).astype(BF16)
    w_out_b, w_up_b, w_down_b = w_out[0].astype(BF16), w_up[0].astype(BF16), w_down[0].astype(BF16)
    row = lambda a: a.reshape(1, -1)
    g_attn, g_mlp, g_fin = row(g_attn_norm[0]), row(g_mlp_norm[0]), row(g_final)
    conv_w = (w_dw[0], row(b_dw[0]), row(conv_ln_g[0]), row(conv_ln_b[0]))
    mlp_w = (w_out_b, g_mlp, w_up_b, w_down_b, g_fin)

    xp = x_prompt.reshape(nb * seq, D_MODEL)
    glu, q, gate, kv_t, kvw_t = _project(xp, g_attn, w_main, w_kv, PROJ_ROWS, seq)
    per_b = lambda a: a.reshape(nb, seq, a.shape[-1])
    glu3 = per_b(glu)
    conv_y = _conv_prompt(glu3, *conv_w)
    attn_y = _attn_prompt(per_b(q), kv_t, kvw_t, per_b(gate), _pool_matrix(w_cmp_k[0], seq),
                          _pool_matrix(w_cmp_v[0], seq))
    y_prompt = _mlp(xp, conv_y.reshape(nb * seq, C_CONV), attn_y.reshape(nb * seq, -1), *mlp_w, MLP_ROWS)
    pos_major = lambda a, n_slots: jnp.transpose(
        a.reshape(a.shape[0], n_slots, N_KV_HEADS, HEAD_DIM, a.shape[-1]), (0, 4, 1, 2, 3))[None]
    new_kv_p = pos_major(kv_t, N_PAGED_SLOTS)
    new_win_p = pos_major(kvw_t[:, :, seq - min(WINDOW, seq):], 2)
    new_conv_p = glu3[:, seq - (CONV_WIDTH - 1):][None]

    xs = x_sample.reshape(db, D_MODEL)
    glu_s, q_s, gate_s, kv_s, kvw_s = _project(xs, g_attn, w_main, w_kv, db)
    conv_y_s, conv_st = _conv_sample(jnp.transpose(state_conv[0], (1, 0, 2)), glu_s, *conv_w)
    cache_t = jnp.transpose(cache_kv[0], (0, 2, 3, 4, 1)).reshape(n_phys, N_PAGED_SLOTS * KV_W, page)
    win_t = jnp.transpose(cache_win[0], (0, 2, 3, 4, 1)).reshape(db, 2 * KV_W, w_buf)
    n_past = page_table.shape[1] * page
    attn_s, new_win_t = _attn_sample(page_table, q_s, kv_s, kvw_s, gate_s, _pool_matrix(w_cmp_k[0], n_past),
                                     _pool_matrix(w_cmp_v[0], n_past), cache_t, win_t)
    new_win_s = pos_major(new_win_t, 2)
    y_sample = _mlp(xs, conv_y_s, attn_s.reshape(db, -1), *mlp_w, db)

    return (y_prompt.reshape(nb, seq, D_MODEL), y_sample.reshape(db, 1, D_MODEL), new_kv_p, new_win_p, new_conv_p,
            kv_s.reshape(1, db, 1, N_PAGED_SLOTS, N_KV_HEADS, HEAD_DIM), new_win_s,
            jnp.transpose(conv_st, (1, 0, 2))[None])
```
